```python
import jax, jax.numpy as jnp
from jax import lax
import numpy as np

D_MODEL = 4096
BATCH = 16
SEQ = 256
DEPTH = 1
DEC_BATCH = 8
DEC_SEQ = 4096
PAST_LEN = 512

GRID_W = 64
EPS = 1e-6
ROPE_BASE = 10000.0
Q_BLOCK = 128
N_MOD = 6
MLA_HEADS = 16
MLA_Q_RANK = 1024
MLA_KV_RANK = 512
MLA_NOPE = 128
MLA_ROPE = 64
MLA_V = 128
GQA_HEADS = 16
GQA_KV_HEADS = 4
HEAD_DIM = 128
PEER_HEADS = 8
N_KEYS = 128
N_EXPERTS = N_KEYS * N_KEYS
PEER_QDIM = 256
PEER_TOPK = 16
PEER_CHUNK = 128
IN_SIZES = (MLA_Q_RANK, MLA_KV_RANK, MLA_ROPE, GQA_HEADS * HEAD_DIM, GQA_KV_HEADS * HEAD_DIM,
            GQA_KV_HEADS * HEAD_DIM, D_MODEL, D_MODEL)
IN_COLS = MLA_Q_RANK + MLA_KV_RANK + MLA_ROPE + GQA_HEADS * HEAD_DIM + 2 * GQA_KV_HEADS * HEAD_DIM + 2 * D_MODEL

kernel_name = "hybrid_mla_gqa_peer_diffusion_step"


def rmsnorm(x, g):
    x32 = x.astype(jnp.float32)
    y = x32 * lax.rsqrt(jnp.mean(x32 * x32, axis=-1, keepdims=True) + EPS)
    return (y * g.astype(jnp.float32)).astype(x.dtype)


def axial_rope_tables(n_tokens, dim):
    n_rows = n_tokens // GRID_W
    rows = jnp.broadcast_to(jnp.arange(n_rows)[:, None], (n_rows, GRID_W)).reshape(-1).astype(jnp.float32)
    cols = jnp.broadcast_to(jnp.arange(GRID_W)[None, :], (n_rows, GRID_W)).reshape(-1).astype(jnp.float32)
    n_freq = dim // 4
    freqs = ROPE_BASE ** (-jnp.arange(n_freq, dtype=jnp.float32) / n_freq)
    ang = jnp.concatenate([rows[:, None] * freqs, cols[:, None] * freqs], axis=-1)
    return (jnp.cos(ang), jnp.sin(ang))


def apply_rope(x, cos, sin):
    shape = (cos.shape[0],) + (1,) * (x.ndim - 3) + (cos.shape[1],)
    cs = cos.reshape(shape)
    sn = sin.reshape(shape)
    xp = x.astype(jnp.float32).reshape(x.shape[:-1] + (x.shape[-1] // 2, 2))
    x1, x2 = xp[..., 0], xp[..., 1]
    out = jnp.stack([x1 * cs - x2 * sn, x1 * sn + x2 * cs], axis=-1).reshape(x.shape)
    return out.astype(x.dtype)


def blocked_attention(q, k, v, scale):
    b, t, h, dq = q.shape
    g = k.shape[2]
    rep = h // g
    dv = v.shape[-1]
    qb = q.reshape(b, t // Q_BLOCK, Q_BLOCK, g, rep, dq).transpose(1, 0, 2, 3, 4, 5)

    def one_block(q_blk):
        s = jnp.einsum('bqgrd,bsgd->bgrqs', q_blk, k).astype(jnp.float32) * scale
        p = jax.nn.softmax(s, axis=-1).astype(v.dtype)
        return jnp.einsum('bgrqs,bsgd->bqgrd', p, v)

    o = lax.map(one_block, qb)
    return o.transpose(1, 0, 2, 3, 4, 5).reshape(b, t, h, dv)


def token_mixer(h, lw, rope, ctx):
    b, t, _ = h.shape
    proj = h @ lw['w_in']
    pts = []
    acc = 0
    for sz in IN_SIZES[:-1]:
        acc += sz
        pts.append(acc)
    q_a, kv_a, k_rope, gq, gk, gv, gate_mla, gate_gqa = jnp.split(proj, pts, axis=-1)
    q = (rmsnorm(q_a, lw['g_q_a']) @ lw['w_q_b']).reshape(b, t, MLA_HEADS, MLA_NOPE + MLA_ROPE)
    q_nope, q_rope = q[..., :MLA_NOPE], q[..., MLA_NOPE:]
    ckv = rmsnorm(kv_a, lw['g_kv_a'])
    gq = rmsnorm(gq.reshape(b, t, GQA_HEADS, HEAD_DIM), lw['g_gqa_q'])
    gk = rmsnorm(gk.reshape(b, t, GQA_KV_HEADS, HEAD_DIM), lw['g_gqa_k'])
    gv = gv.reshape(b, t, GQA_KV_HEADS, HEAD_DIM)
    if rope is not None:
        cos_m, sin_m, cos_g, sin_g = rope
        q_rope = apply_rope(q_rope, cos_m, sin_m)
        k_rope = apply_rope(k_rope, cos_m, sin_m)
        gq = apply_rope(gq, cos_g, sin_g)
        gk = apply_rope(gk, cos_g, sin_g)
    own = (ckv, k_rope, gk, gv)
    if ctx is not None:
        ckv_all = jnp.concatenate([ckv, ctx[0]], axis=1)
        kr_all = jnp.concatenate([k_rope, ctx[1]], axis=1)
        gk_all = jnp.concatenate([gk, ctx[2]], axis=1)
        gv_all = jnp.concatenate([gv, ctx[3]], axis=1)
    else:
        ckv_all, kr_all, gk_all, gv_all = own
    s_len = ckv_all.shape[1]
    kv = (ckv_all @ lw['w_kv_b']).reshape(b, s_len, MLA_HEADS, MLA_NOPE + MLA_V)
    k_mla = jnp.concatenate(
        [kv[..., :MLA_NOPE], jnp.broadcast_to(kr_all[:, :, None, :], (b, s_len, MLA_HEADS, MLA_ROPE))], axis=-1)
    v_mla = kv[..., MLA_NOPE:]
    q_mla = jnp.concatenate([q_nope, q_rope], axis=-1)
    o_mla = blocked_attention(q_mla, k_mla, v_mla, (MLA_NOPE + MLA_ROPE) ** -0.5).reshape(b, t, MLA_HEADS * MLA_V)
    o_gqa = blocked_attention(gq, gk_all, gv_all, HEAD_DIM ** -0.5).reshape(b, t, GQA_HEADS * HEAD_DIM)
    merged = (jax.nn.sigmoid(gate_mla) * (o_mla @ lw['w_mla_o'])
              + jax.nn.sigmoid(gate_gqa) * (o_gqa @ lw['w_gqa_o']))
    return merged @ lw['w_out'], own


def peer(h, lw):
    b, t, d = h.shape
    n = b * t
    xf = h.reshape(n, d)
    q = (xf @ lw['w_peer_q']).reshape(n, PEER_HEADS, 2, PEER_QDIM // 2)
    s = jnp.einsum('nhpk,pmk->nhpm', q, lw['peer_sub_keys']).astype(jnp.float32)
    sv, si = lax.top_k(s, PEER_TOPK)
    cand = sv[..., 0, :, None] + sv[..., 1, None, :]
    cand_idx = si[..., 0, :, None] * N_KEYS + si[..., 1, None, :]
    best, pos = lax.top_k(cand.reshape(n, PEER_HEADS, PEER_TOPK * PEER_TOPK), PEER_TOPK)
    experts = jnp.take_along_axis(cand_idx.reshape(n, PEER_HEADS, PEER_TOPK * PEER_TOPK), pos, axis=-1)
    gates = jax.nn.softmax(best, axis=-1).astype(h.dtype)
    n_chunks = n // PEER_CHUNK

    def one_chunk(args):
        x_c, e_c, g_c = args
        u = jnp.take(lw['peer_u'], e_c, axis=0)
        a = jnp.einsum('cd,ced->ce', x_c, u)
        w = jax.nn.gelu(a, approximate=False) * g_c
        vv = jnp.take(lw['peer_v'], e_c, axis=0)
        return jnp.einsum('ce,ced->cd', w, vv)

    out = lax.map(one_chunk, (xf.reshape(n_chunks, PEER_CHUNK, d),
                              experts.reshape(n_chunks, PEER_CHUNK, PEER_HEADS * PEER_TOPK),
                              gates.reshape(n_chunks, PEER_CHUNK, PEER_HEADS * PEER_TOPK)))
    return out.reshape(b, t, d)


def modulation(cvec, w_mod, b_mod):
    m = jax.nn.silu(cvec) @ w_mod + b_mod
    return jnp.split(m, N_MOD, axis=-1)


def layer(x, mods, lw, rope, ctx):
    shift1, scale1, gate1, shift2, scale2, gate2 = mods
    h = rmsnorm(x, lw['g_norm1']) * (1 + scale1) + shift1
    o, own = token_mixer(h, lw, rope, ctx)
    x = x + gate1 * o
    h2 = rmsnorm(x, lw['g_norm2']) * (1 + scale2) + shift2
    x = x + gate2 * peer(h2, lw)
    return x, own


def setup_inputs(seed: int = 0) -> dict:
    key = jax.random.key(seed)
    ks = jax.random.split(key, 32)
    f32 = jnp.float32

    def nrm(k, shape, scale):
        return jax.random.normal(k, shape, f32) * scale

    def gain(k, shape):
        return 1.0 + 0.01 * jax.random.normal(k, shape, f32)

    return {
        'x_prompt': nrm(ks[0], (BATCH, SEQ, D_MODEL), 1.0),
        'x_sample': nrm(ks[1], (DEC_BATCH, DEC_SEQ, D_MODEL), 1.0),
        'c': nrm(ks[2], (DEC_BATCH, D_MODEL), 1.0),
        'cache_mla_ckv': nrm(ks[3], (DEC_BATCH, DEPTH, PAST_LEN, MLA_KV_RANK), 1.0),
        'cache_mla_krope': nrm(ks[4], (DEC_BATCH, DEPTH, PAST_LEN, MLA_ROPE), 1.0),
        'cache_gqa_k': nrm(ks[5], (DEC_BATCH, DEPTH, PAST_LEN, GQA_KV_HEADS, HEAD_DIM), 1.0),
        'cache_gqa_v': nrm(ks[6], (DEC_BATCH, DEPTH, PAST_LEN, GQA_KV_HEADS, HEAD_DIM), 1.0),
        'c_ctx': nrm(ks[7], (D_MODEL,), 1.0),
        'w_mod': nrm(ks[8], (DEPTH, D_MODEL, N_MOD * D_MODEL), 0.5 * D_MODEL ** -0.5),
        'b_mod': nrm(ks[9], (DEPTH, N_MOD * D_MODEL), 0.01),
        'g_norm1': gain(ks[10], (DEPTH, D_MODEL)),
        'g_norm2': gain(ks[11], (DEPTH, D_MODEL)),
        'w_in': nrm(ks[12], (DEPTH, D_MODEL, IN_COLS), D_MODEL ** -0.5),
        'g_q_a': gain(ks[13], (DEPTH, MLA_Q_RANK)),
        'w_q_b': nrm(ks[14], (DEPTH, MLA_Q_RANK, MLA_HEADS * (MLA_NOPE + MLA_ROPE)), MLA_Q_RANK ** -0.5),
        'g_kv_a': gain(ks[15], (DEPTH, MLA_KV_RANK)),
        'w_kv_b': nrm(ks[16], (DEPTH, MLA_KV_RANK, MLA_HEADS * (MLA_NOPE + MLA_V)), MLA_KV_RANK ** -0.5),
        'g_gqa_q': gain(ks[17], (DEPTH, HEAD_DIM)),
        'g_gqa_k': gain(ks[18], (DEPTH, HEAD_DIM)),
        'w_mla_o': nrm(ks[19], (DEPTH, MLA_HEADS * MLA_V, D_MODEL), (MLA_HEADS * MLA_V) ** -0.5),
        'w_gqa_o': nrm(ks[20], (DEPTH, GQA_HEADS * HEAD_DIM, D_MODEL), (GQA_HEADS * HEAD_DIM) ** -0.5),
        'w_out': nrm(ks[21], (DEPTH, D_MODEL, D_MODEL), D_MODEL ** -0.5),
        'w_peer_q': nrm(ks[22], (DEPTH, D_MODEL, PEER_HEADS * PEER_QDIM), D_MODEL ** -0.5),
        'peer_sub_keys': nrm(ks[23], (DEPTH, 2, N_KEYS, PEER_QDIM // 2), (PEER_QDIM // 2) ** -0.5),
        'peer_u': nrm(ks[24], (DEPTH, N_EXPERTS, D_MODEL), D_MODEL ** -0.5),
        'peer_v': nrm(ks[25], (DEPTH, N_EXPERTS, D_MODEL), PEER_HEADS ** -0.5),
        'g_final': gain(ks[26], (D_MODEL,)),
    }


def reference(x_prompt, x_sample, c, cache_mla_ckv, cache_mla_krope, cache_gqa_k, cache_gqa_v,
              c_ctx, w_mod, b_mod, g_norm1, g_norm2, w_in, g_q_a, w_q_b, g_kv_a, w_kv_b,
              g_gqa_q, g_gqa_k, w_mla_o, w_gqa_o, w_out, w_peer_q, peer_sub_keys, peer_u, peer_v,
              g_final):
    t_lat = x_sample.shape[1]
    rope = axial_rope_tables(t_lat, MLA_ROPE) + axial_rope_tables(t_lat, HEAD_DIM)
    xc = x_prompt
    xl = x_sample
    new_ckv, new_kr, new_k, new_v = [], [], [], []
    for l in range(DEPTH):
        lw = {
            'g_norm1': g_norm1[l], 'g_norm2': g_norm2[l], 'w_in': w_in[l],
            'g_q_a': g_q_a[l], 'w_q_b': w_q_b[l], 'g_kv_a': g_kv_a[l], 'w_kv_b': w_kv_b[l],
            'g_gqa_q': g_gqa_q[l], 'g_gqa_k': g_gqa_k[l],
            'w_mla_o': w_mla_o[l], 'w_gqa_o': w_gqa_o[l], 'w_out': w_out[l],
            'w_peer_q': w_peer_q[l], 'peer_sub_keys': peer_sub_keys[l],
            'peer_u': peer_u[l], 'peer_v': peer_v[l],
        }
        mods_ctx = modulation(c_ctx, w_mod[l], b_mod[l])
        mods_lat = modulation(c[:, None, :], w_mod[l], b_mod[l])
        xc, own_ctx = layer(xc, mods_ctx, lw, None, None)
        new_ckv.append(own_ctx[0])
        new_kr.append(own_ctx[1])
        new_k.append(own_ctx[2])
        new_v.append(own_ctx[3])
        ctx = (cache_mla_ckv[:, l], cache_mla_krope[:, l], cache_gqa_k[:, l], cache_gqa_v[:, l])
        xl, _ = layer(xl, mods_lat, lw, rope, ctx)
    y_prompt = rmsnorm(xc, g_final)
    y_sample = rmsnorm(xl, g_final)
    return (y_prompt, y_sample, jnp.stack(new_ckv, axis=1), jnp.stack(new_kr, axis=1),
            jnp.stack(new_k, axis=1), jnp.stack(new_v, axis=1))
```

```python
import functools
import math

import jax
import jax.numpy as jnp
from jax import lax
from jax.experimental import pallas as pl
from jax.experimental.pallas import tpu as pltpu

F32 = jnp.float32
BF16 = jnp.bfloat16

GRID_W = 64
EPS = 1e-6
ROPE_BASE = 10000.0
N_MOD = 6
MLA_HEADS = 16
MLA_NOPE = 128
MLA_ROPE = 64
MLA_V = 128
MLA_QPAD = 256
GQA_HEADS = 16
GQA_KV_HEADS = 4
HEAD_DIM = 128
PEER_HEADS = 8
N_KEYS = 128
PEER_TOPK = 16
LANE = 128
SUBLANE = 8
VMEM_LIMIT = 56 * 1024 * 1024
INV_SQRT2 = 1.0 / math.sqrt(2.0)

SHIFT1, SCALE1, GATE1, SHIFT2, SCALE2, GATE2 = range(6)


def _tile(dim, pref):
    t = min(dim, pref)
    assert dim % t == 0, (dim, pref)
    return t


def _params(*sem):
    return pltpu.CompilerParams(dimension_semantics=sem, vmem_limit_bytes=VMEM_LIMIT)


def _mod_kernel(c_ref, w_ref, b_ref, o_ref):
    c = c_ref[...]
    a = (c * jax.nn.sigmoid(c)).astype(BF16)
    o_ref[...] = jnp.dot(a, w_ref[...].astype(BF16), preferred_element_type=F32) + b_ref[...]


def _modulation(cvecs, w_mod, b_mod):
    r, d = cvecs.shape
    n = w_mod.shape[1]
    tn = _tile(n, 512)
    return pl.pallas_call(
        _mod_kernel,
        grid=(n // tn,),
        in_specs=[pl.BlockSpec((r, d), lambda j: (0, 0)),
                  pl.BlockSpec((d, tn), lambda j: (0, j)),
                  pl.BlockSpec((1, tn), lambda j: (0, j))],
        out_specs=pl.BlockSpec((r, tn), lambda j: (0, j)),
        out_shape=jax.ShapeDtypeStruct((r, n), F32),
        compiler_params=_params("parallel"),
        name="modulation",
    )(cvecs, w_mod, b_mod.reshape(1, n))


def _norm_mod_kernel(x_ref, g_ref, m_ref, o_ref, *, shift_row, scale_row):
    x = x_ref[...]
    y = x * lax.rsqrt(jnp.mean(x * x, axis=-1, keepdims=True) + EPS) * g_ref[...]
    m = m_ref[0]
    o_ref[...] = (y * (1.0 + m[scale_row:scale_row + 1]) + m[shift_row:shift_row + 1]).astype(BF16)


def _norm_mod(x, g, mods, rows_per_batch, shift_row, scale_row):
    m, d = x.shape
    tm = _tile(rows_per_batch, 256)
    per = rows_per_batch // tm
    return pl.pallas_call(
        functools.partial(_norm_mod_kernel, shift_row=shift_row, scale_row=scale_row),
        grid=(m // tm,),
        in_specs=[pl.BlockSpec((tm, d), lambda i: (i, 0)),
                  pl.BlockSpec((1, d), lambda i: (0, 0)),
                  pl.BlockSpec((1, 8, d), lambda i: (i // per, 0, 0))],
        out_specs=pl.BlockSpec((tm, d), lambda i: (i, 0)),
        out_shape=jax.ShapeDtypeStruct((m, d), BF16),
        compiler_params=_params("parallel"),
        name="norm_mod",
    )(x, g.reshape(1, d), mods)


def _swap_pairs(x):
    ax = x.ndim - 1
    n = x.shape[ax]
    lane = lax.broadcasted_iota(jnp.int32, x.shape, ax)
    nxt = pltpu.roll(x, n - 1, axis=ax)
    prv = pltpu.roll(x, 1, axis=ax)
    return jnp.where((lane & 1) == 0, nxt, prv)


def _mm_kernel(a_ref, b_ref, o_ref, *, scale):
    acc = jnp.dot(a_ref[...], b_ref[...], preferred_element_type=F32)
    if scale is not None:
        acc = acc * scale
    o_ref[...] = acc.astype(o_ref.dtype)


def _mm_rope_kernel(a_ref, b_ref, cos_ref, sin_ref, o_ref, *, scale):
    acc = jnp.dot(a_ref[...], b_ref[...], preferred_element_type=F32)
    acc = acc * cos_ref[...] + _swap_pairs(acc) * sin_ref[...]
    o_ref[...] = (acc * scale).astype(o_ref.dtype)


def _mm(a, b, out_dtype, scale=None, rope=None, rows_per_batch=None, tm=1024, tn=512):
    m, k = a.shape
    n = b.shape[1]
    tm = _tile(m if rows_per_batch is None else rows_per_batch, tm)
    tn = _tile(n, tn)
    in_specs = [pl.BlockSpec((tm, k), lambda i, j: (i, 0)),
                pl.BlockSpec((k, tn), lambda i, j: (0, j))]
    args = [a, b]
    if rope is None:
        body = functools.partial(_mm_kernel, scale=scale)
    else:
        per = rows_per_batch // tm
        body = functools.partial(_mm_rope_kernel, scale=scale)
        in_specs += [pl.BlockSpec((tm, tn), lambda i, j: (i % per, 0))] * 2
        args += list(rope)
    return pl.pallas_call(
        body,
        grid=(m // tm, n // tn),
        in_specs=in_specs,
        out_specs=pl.BlockSpec((tm, tn), lambda i, j: (i, j)),
        out_shape=jax.ShapeDtypeStruct((m, n), out_dtype),
        compiler_params=_params("parallel", "parallel"),
        name="matmul",
    )(*args)


def _mm_res_kernel(a_ref, b_ref, x_ref, m_ref, o_ref, *, gate_row):
    acc = jnp.dot(a_ref[...], b_ref[...], preferred_element_type=F32)
    o_ref[...] = x_ref[...] + m_ref[0][gate_row:gate_row + 1] * acc


def _mm_residual(a, b, x, mods, rows_per_batch, gate_row, tm=1024, tn=512):
    m, k = a.shape
    n = b.shape[1]
    tm = _tile(rows_per_batch, tm)
    tn = _tile(n, tn)
    per = rows_per_batch // tm
    return pl.pallas_call(
        functools.partial(_mm_res_kernel, gate_row=gate_row),
        grid=(m // tm, n // tn),
        in_specs=[pl.BlockSpec((tm, k), lambda i, j: (i, 0)),
                  pl.BlockSpec((k, tn), lambda i, j: (0, j)),
                  pl.BlockSpec((tm, tn), lambda i, j: (i, j)),
                  pl.BlockSpec((1, 8, tn), lambda i, j: (i // per, 0, j))],
        out_specs=pl.BlockSpec((tm, tn), lambda i, j: (i, j)),
        out_shape=jax.ShapeDtypeStruct((m, n), F32),
        compiler_params=_params("parallel", "parallel"),
        name="matmul_residual",
    )(a, b, x, mods)


def _merge_kernel(oa_ref, ob_ref, wa_ref, wb_ref, ga_ref, gb_ref, o_ref):
    ya = jnp.dot(oa_ref[...], wa_ref[...], preferred_element_type=F32)
    yb = jnp.dot(ob_ref[...], wb_ref[...], preferred_element_type=F32)
    o_ref[...] = (jax.nn.sigmoid(ga_ref[...]) * ya + jax.nn.sigmoid(gb_ref[...]) * yb).astype(o_ref.dtype)


def _merge(o_mla, o_gqa, w_mla_o, w_gqa_o, proj, col_gate_mla, col_gate_gqa, tm=1024, tn=512):
    m, k = o_mla.shape
    n = w_mla_o.shape[1]
    tm = _tile(m, tm)
    tn = _tile(n, tn)
    ca, cb = col_gate_mla // tn, col_gate_gqa // tn
    assert ca * tn == col_gate_mla and cb * tn == col_gate_gqa
    return pl.pallas_call(
        _merge_kernel,
        grid=(m // tm, n // tn),
        in_specs=[pl.BlockSpec((tm, k), lambda i, j: (i, 0)),
                  pl.BlockSpec((tm, k), lambda i, j: (i, 0)),
                  pl.BlockSpec((k, tn), lambda i, j: (0, j)),
                  pl.BlockSpec((k, tn), lambda i, j: (0, j)),
                  pl.BlockSpec((tm, tn), lambda i, j: (i, ca + j)),
                  pl.BlockSpec((tm, tn), lambda i, j: (i, cb + j))],
        out_specs=pl.BlockSpec((tm, tn), lambda i, j: (i, j)),
        out_shape=jax.ShapeDtypeStruct((m, n), BF16),
        compiler_params=_params("parallel", "parallel"),
        name="merge",
    )(o_mla, o_gqa, w_mla_o, w_gqa_o, proj, proj)


def _rms(x, g):
    return x * lax.rsqrt(jnp.mean(x * x, axis=-1, keepdims=True) + EPS) * g


def _prep_kernel(*refs, q_rank, kv_rank, use_rope):
    if use_rope:
        (p_ref, kr_ref, gqa_ref, gkv_ref, ggq_ref, ggk_ref, cg_ref, sg_ref, cm_ref, sm_ref,
         qa_o, ckv_o, ckvx_o, gq_o, gk32_o, gk_o, gv_o) = refs
    else:
        (p_ref, kr_ref, gqa_ref, gkv_ref, ggq_ref, ggk_ref,
         qa_o, ckv_o, ckvx_o, gq_o, gk32_o, gk_o, gv_o) = refs
    c0 = q_rank
    c1 = c0 + kv_rank
    c2 = c1 + GQA_HEADS * HEAD_DIM
    c3 = c2 + GQA_KV_HEADS * HEAD_DIM
    qa_o[...] = _rms(p_ref[:, 0:c0], gqa_ref[...]).astype(BF16)
    ckv = _rms(p_ref[:, c0:c1], gkv_ref[...])
    ckv_o[...] = ckv
    ckvx_o[:, 0:kv_rank] = ckv.astype(BF16)
    kr = kr_ref[...]
    if use_rope:
        kr = kr * cm_ref[...] + _swap_pairs(kr) * sm_ref[...]
    ckvx_o[:, kv_rank:kv_rank + LANE] = kr.astype(BF16)
    q_scale = HEAD_DIM ** -0.5
    for h in range(GQA_HEADS):
        x = _rms(p_ref[:, c1 + h * HEAD_DIM:c1 + (h + 1) * HEAD_DIM], ggq_ref[...])
        if use_rope:
            x = x * cg_ref[...] + _swap_pairs(x) * sg_ref[...]
        gq_o[:, h * HEAD_DIM:(h + 1) * HEAD_DIM] = (x * q_scale).astype(BF16)
    for h in range(GQA_KV_HEADS):
        x = _rms(p_ref[:, c2 + h * HEAD_DIM:c2 + (h + 1) * HEAD_DIM], ggk_ref[...])
        if use_rope:
            x = x * cg_ref[...] + _swap_pairs(x) * sg_ref[...]
        gk32_o[:, h * HEAD_DIM:(h + 1) * HEAD_DIM] = x
        gk_o[:, h * HEAD_DIM:(h + 1) * HEAD_DIM] = x.astype(BF16)
    gv_o[...] = p_ref[:, c3:c3 + GQA_KV_HEADS * HEAD_DIM].astype(BF16)


def _prep(proj, kr, g_q_a, g_kv_a, g_gqa_q, g_gqa_k, rope, rows_per_batch):
    m = proj.shape[0]
    q_rank, kv_rank = g_q_a.shape[0], g_kv_a.shape[0]
    width = q_rank + kv_rank + (GQA_HEADS + 2 * GQA_KV_HEADS) * HEAD_DIM
    tm = _tile(rows_per_batch, 256)
    per = rows_per_batch // tm
    nq, nk = GQA_HEADS * HEAD_DIM, GQA_KV_HEADS * HEAD_DIM
    row = lambda i: (i, 0)
    const = lambda i: (0, 0)
    in_specs = [pl.BlockSpec((tm, width), row), pl.BlockSpec((tm, LANE), row),
                pl.BlockSpec((1, q_rank), const), pl.BlockSpec((1, kv_rank), const),
                pl.BlockSpec((1, HEAD_DIM), const), pl.BlockSpec((1, HEAD_DIM), const)]
    args = [proj, kr, g_q_a.reshape(1, -1), g_kv_a.reshape(1, -1), g_gqa_q.reshape(1, -1), g_gqa_k.reshape(1, -1)]
    if rope is not None:
        in_specs += [pl.BlockSpec((tm, LANE), lambda i: (i % per, 0))] * 4
        args += list(rope)
    outs = [((m, q_rank), BF16), ((m, kv_rank), F32), ((m, kv_rank + LANE), BF16), ((m, nq), BF16),
            ((m, nk), F32), ((m, nk), BF16), ((m, nk), BF16)]
    return pl.pallas_call(
        functools.partial(_prep_kernel, q_rank=q_rank, kv_rank=kv_rank, use_rope=rope is not None),
        grid=(m // tm,),
        in_specs=in_specs,
        out_specs=[pl.BlockSpec((tm, s[1]), row) for s, _ in outs],
        out_shape=[jax.ShapeDtypeStruct(s, dt) for s, dt in outs],
        compiler_params=_params("parallel"),
        name="prep",
    )(*args)


def _attn_kernel(q_ref, k_ref, v_ref, o_ref, *, rep, dq, dv):
    k = k_ref[...]
    v = v_ref[...]
    for r in range(rep):
        q = q_ref[:, r * dq:(r + 1) * dq]
        s = lax.dot_general(q, k, (((1,), (1,)), ((), ())), preferred_element_type=F32)
        p = jnp.exp(s - jnp.max(s, axis=-1, keepdims=True))
        l = jnp.sum(p, axis=-1, keepdims=True)
        o = jnp.dot(p.astype(BF16), v, preferred_element_type=F32)
        o_ref[:, r * dv:(r + 1) * dv] = (o / l).astype(o_ref.dtype)


def _attention(q, k, v, n_groups, rep, dq, dv, k_col0, v_col0, tq=256):
    b, t, _ = q.shape
    s_len = k.shape[1]
    tq = _tile(t, tq)
    kb, vb = k_col0 // dq, v_col0 // dv
    assert kb * dq == k_col0 and vb * dv == v_col0
    return pl.pallas_call(
        functools.partial(_attn_kernel, rep=rep, dq=dq, dv=dv),
        grid=(b, n_groups, t // tq),
        in_specs=[pl.BlockSpec((None, tq, rep * dq), lambda bi, g, qi: (bi, qi, g)),
                  pl.BlockSpec((None, s_len, dq), lambda bi, g, qi: (bi, 0, kb + g)),
                  pl.BlockSpec((None, s_len, dv), lambda bi, g, qi: (bi, 0, vb + g))],
        out_specs=pl.BlockSpec((None, tq, rep * dv), lambda bi, g, qi: (bi, qi, g)),
        out_shape=jax.ShapeDtypeStruct((b, t, n_groups * rep * dv), BF16),
        compiler_params=_params("parallel", "parallel", "parallel"),
        name="attention",
    )(q, k, v)


def _top16(s):
    n, c = s.shape
    iota = lax.broadcasted_iota(jnp.int32, (n, c), 0).astype(F32)
    row = lax.broadcasted_iota(jnp.int32, (PEER_TOPK, c), 0)

    def body(r, carry):
        s, vals, idxs = carry
        mx = jnp.max(s, axis=0, keepdims=True)
        am = jnp.min(jnp.where(s == mx, iota, float(n)), axis=0, keepdims=True)
        vals = jnp.where(row == r, mx, vals)
        idxs = jnp.where(row == r, am, idxs)
        s = jnp.where(iota == am, -jnp.inf, s)
        return s, vals, idxs

    init = (s, jnp.zeros((PEER_TOPK, c), F32), jnp.zeros((PEER_TOPK, c), F32))
    _, vals, idxs = lax.fori_loop(0, PEER_TOPK, body, init)
    return vals, idxs


def _pick(table, pos):
    out = jnp.zeros(pos.shape, F32)
    for a in range(PEER_TOPK):
        out = jnp.where(pos == a, table[a:a + 1], out)
    return out


def _route_kernel(q_ref, k_ref, oi_ref, oj_ref, og_ref, *, tn):
    q = q_ref[...]
    half = q.shape[1] // 2
    nt = (((1,), (1,)), ((), ()))
    s0 = lax.dot_general(k_ref[0], q[:, :half], nt, preferred_element_type=F32)
    s1 = lax.dot_general(k_ref[1], q[:, half:], nt, preferred_element_type=F32)
    for c in range(tn // LANE):
        sl = slice(c * LANE, (c + 1) * LANE)
        v0, i0 = _top16(s0[:, sl])
        v1, i1 = _top16(s1[:, sl])
        cand = jnp.concatenate([v0[a:a + 1] + v1 for a in range(PEER_TOPK)], axis=0)
        best, pos = _top16(cand)
        pos = pos.astype(jnp.int32)
        e = jnp.exp(best - best[0:1])
        oi_ref[:, sl] = _pick(i0, pos >> 4)
        oj_ref[:, sl] = _pick(i1, pos & (PEER_TOPK - 1))
        og_ref[:, sl] = e / jnp.sum(e, axis=0, keepdims=True)


def _route(qp, sub_keys, tn=256):
    m = qp.shape[0]
    tn = _tile(m, tn)
    qd = qp.shape[1] // PEER_HEADS
    shp = jax.ShapeDtypeStruct((PEER_HEADS * PEER_TOPK, m), F32)
    spec = pl.BlockSpec((PEER_TOPK, tn), lambda t, h: (h, t))
    return pl.pallas_call(
        functools.partial(_route_kernel, tn=tn),
        grid=(m // tn, PEER_HEADS),
        in_specs=[pl.BlockSpec((tn, qd), lambda t, h: (t, h)),
                  pl.BlockSpec(sub_keys.shape, lambda t, h: (0, 0, 0))],
        out_specs=[spec, spec, spec],
        out_shape=[shp, shp, shp],
        compiler_params=_params("parallel", "parallel"),
        name="peer_route",
    )(qp, sub_keys)


def _wbuild_kernel(i_ref, j_ref, g_ref, o_ref, it_s, jt_s, gt_s):
    it_s[...] = i_ref[...].T
    jt_s[...] = j_ref[...].T
    gt_s[...] = g_ref[...].T
    ne = i_ref.shape[0]
    iota_a = lax.broadcasted_iota(jnp.int32, (2 * N_KEYS, 2 * ne), 0).astype(F32)
    iota_g = lax.broadcasted_iota(jnp.int32, (N_KEYS, 2 * ne), 0).astype(F32)

    def body(m, carry):
        ri = it_s[pl.ds(2 * m, 2), :]
        rj = jt_s[pl.ds(2 * m, 2), :]
        rg = gt_s[pl.ds(2 * m, 2), :]
        row_i = jnp.concatenate([ri[0:1], ri[1:2] + float(N_KEYS)], axis=1)
        row_j = jnp.concatenate([rj[0:1], rj[1:2]], axis=1)
        row_g = jnp.concatenate([rg[0:1], rg[1:2]], axis=1)
        a2 = jnp.where(iota_a == row_i, 1.0, 0.0).astype(BF16)
        g2 = jnp.where(iota_g == row_j, row_g, 0.0).astype(BF16)
        w = lax.dot_general(a2, g2, (((1,), (1,)), ((), ())), preferred_element_type=F32)
        o_ref[pl.ds(2 * m, 2)] = w.reshape(2, N_KEYS, N_KEYS)
        return carry

    lax.fori_loop(0, o_ref.shape[0] // 2, body, 0)


def _wbuild(idx_i, idx_j, gates):
    ne, m = idx_i.shape
    tw = LANE
    spec = pl.BlockSpec((ne, tw), lambda t: (0, t))
    return pl.pallas_call(
        _wbuild_kernel,
        grid=(m // tw,),
        in_specs=[spec, spec, spec],
        out_specs=pl.BlockSpec((tw, N_KEYS, N_KEYS), lambda t: (t, 0, 0)),
        out_shape=jax.ShapeDtypeStruct((m, N_KEYS, N_KEYS), F32),
        scratch_shapes=[pltpu.VMEM((tw, ne), F32)] * 3,
        compiler_params=_params("parallel"),
        name="peer_gate_map",
    )(idx_i, idx_j, gates)


def _peer_kernel(x_ref, ut_ref, v_ref, w_ref, o_ref, g_s, *, tn, rows):
    c = pl.program_id(1)

    @pl.when(c == 0)
    def _():
        o_ref[...] = jnp.zeros_like(o_ref)

    s = jnp.dot(x_ref[...], ut_ref[...], preferred_element_type=F32)
    base = (c % (SUBLANE // rows)) * rows
    for il in range(rows):
        wg = w_ref[:, base + il, :]
        sl = s[:, il * N_KEYS:(il + 1) * N_KEYS]
        g = 0.5 * sl * (1.0 + lax.erf(sl * INV_SQRT2)) * wg
        g_s[:, il * N_KEYS:(il + 1) * N_KEYS] = g.astype(BF16)
    o_ref[...] += jnp.dot(g_s[...], v_ref[...], preferred_element_type=F32)


def _peer_mix(h2, u_t, v, wmap, tn=512, rows=4):
    m, d = h2.shape
    n_exp = v.shape[0]
    tn = _tile(m, tn)
    ec = rows * N_KEYS
    per = SUBLANE // rows
    return pl.pallas_call(
        functools.partial(_peer_kernel, tn=tn, rows=rows),
        grid=(m // tn, n_exp // ec),
        in_specs=[pl.BlockSpec((tn, d), lambda t, c: (t, 0)),
                  pl.BlockSpec((d, ec), lambda t, c: (0, c)),
                  pl.BlockSpec((ec, d), lambda t, c: (c, 0)),
                  pl.BlockSpec((tn, SUBLANE, N_KEYS), lambda t, c: (t, c // per, 0))],
        out_specs=pl.BlockSpec((tn, d), lambda t, c: (t, 0)),
        out_shape=jax.ShapeDtypeStruct((m, d), F32),
        scratch_shapes=[pltpu.VMEM((tn, ec), BF16)],
        compiler_params=_params("parallel", "arbitrary"),
        name="peer_mix",
    )(h2, u_t, v, wmap)


def _final_kernel(x_ref, p_ref, m_ref, g_ref, o_ref, *, gate_row):
    x = x_ref[...] + m_ref[0][gate_row:gate_row + 1] * p_ref[...]
    o_ref[...] = _rms(x, g_ref[...])


def _final(x, po, mods, g_final, rows_per_batch, gate_row):
    m, d = x.shape
    tm = _tile(rows_per_batch, 256)
    per = rows_per_batch // tm
    row = lambda i: (i, 0)
    return pl.pallas_call(
        functools.partial(_final_kernel, gate_row=gate_row),
        grid=(m // tm,),
        in_specs=[pl.BlockSpec((tm, d), row), pl.BlockSpec((tm, d), row),
                  pl.BlockSpec((1, 8, d), lambda i: (i // per, 0, 0)),
                  pl.BlockSpec((1, d), lambda i: (0, 0))],
        out_specs=pl.BlockSpec((tm, d), row),
        out_shape=jax.ShapeDtypeStruct((m, d), F32),
        compiler_params=_params("parallel"),
        name="final_norm",
    )(x, po, mods, g_final.reshape(1, d))


def _rope_tables(n_tokens, dim):
    n_rows = n_tokens // GRID_W
    rows = jnp.broadcast_to(jnp.arange(n_rows)[:, None], (n_rows, GRID_W)).reshape(-1).astype(F32)
    cols = jnp.broadcast_to(jnp.arange(GRID_W)[None, :], (n_rows, GRID_W)).reshape(-1).astype(F32)
    n_freq = dim // 4
    freqs = ROPE_BASE ** (-jnp.arange(n_freq, dtype=F32) / n_freq)
    ang = jnp.concatenate([rows[:, None] * freqs, cols[:, None] * freqs], axis=-1)
    cos, sin = jnp.cos(ang), jnp.sin(ang)
    cos2 = jnp.repeat(cos, 2, axis=-1)
    sin2 = jnp.stack([-sin, sin], axis=-1).reshape(n_tokens, dim)
    return cos2, sin2


def _layer_weights(w_in, w_q_b, w_kv_b, w_mla_o, w_gqa_o, w_out, w_peer_q, sub_keys, peer_u, peer_v):
    q_rank, kv_rank = w_q_b.shape[0], w_kv_b.shape[0]
    k0 = q_rank + kv_rank
    lw = {}
    lw['w_in'] = jnp.concatenate([w_in[:, :k0], w_in[:, k0 + MLA_ROPE:]], axis=1).astype(BF16)
    lw['w_kr'] = jnp.pad(w_in[:, k0:k0 + MLA_ROPE], ((0, 0), (0, LANE - MLA_ROPE))).astype(BF16)
    wq = w_q_b.reshape(q_rank, MLA_HEADS, MLA_NOPE + MLA_ROPE)
    wq = jnp.pad(wq, ((0, 0), (0, 0), (0, MLA_QPAD - MLA_NOPE - MLA_ROPE)))
    lw['w_q_b'] = wq.reshape(q_rank, MLA_HEADS * MLA_QPAD).astype(BF16)
    wkv = w_kv_b.reshape(kv_rank, MLA_HEADS, MLA_NOPE + MLA_V)
    wk = jnp.zeros((kv_rank + LANE, MLA_HEADS, MLA_QPAD), F32)
    wk = wk.at[:kv_rank, :, :MLA_NOPE].set(wkv[:, :, :MLA_NOPE])
    eye = jnp.broadcast_to(jnp.eye(MLA_ROPE, dtype=F32)[:, None, :], (MLA_ROPE, MLA_HEADS, MLA_ROPE))
    wk = wk.at[kv_rank:kv_rank + MLA_ROPE, :, MLA_NOPE:MLA_NOPE + MLA_ROPE].set(eye)
    wv = jnp.pad(wkv[:, :, MLA_NOPE:], ((0, LANE), (0, 0), (0, 0)))
    lw['w_kv'] = jnp.concatenate([wk.reshape(kv_rank + LANE, -1), wv.reshape(kv_rank + LANE, -1)],
                                 axis=1).astype(BF16)
    lw['w_mla_o'] = w_mla_o.astype(BF16)
    lw['w_gqa_o'] = w_gqa_o.astype(BF16)
    lw['w_out'] = w_out.astype(BF16)
    lw['w_peer_q'] = w_peer_q.astype(BF16)
    lw['sub_keys'] = sub_keys.astype(BF16)
    lw['peer_ut'] = peer_u.T.astype(BF16)
    lw['peer_v'] = peer_v.astype(BF16)
    return lw


def _layer(x, mods, lw, gains, rope, ctx):
    b, t, d = x.shape
    m = b * t
    x2 = x.reshape(m, d)
    g_norm1, g_norm2, g_q_a, g_kv_a, g_gqa_q, g_gqa_k = gains
    q_rank, kv_rank = g_q_a.shape[0], g_kv_a.shape[0]
    nq, nk = GQA_HEADS * HEAD_DIM, GQA_KV_HEADS * HEAD_DIM
    col_gate_mla = q_rank + kv_rank + nq + 2 * nk
    col_gate_gqa = col_gate_mla + d

    rpb = t if mods.shape[0] == b else m
    h = _norm_mod(x2, g_norm1, mods, rpb, SHIFT1, SCALE1)
    proj = _mm(h, lw['w_in'], F32)
    kr = _mm(h, lw['w_kr'], F32)
    prep_rope = None if rope is None else rope[:4]
    qa, ckv32, ckvx, gq, gk32, gk, gv = _prep(proj, kr, g_q_a, g_kv_a, g_gqa_q, g_gqa_k, prep_rope, t)

    mla_scale = (MLA_NOPE + MLA_ROPE) ** -0.5
    if rope is None:
        q_mla = _mm(qa, lw['w_q_b'], BF16, scale=mla_scale)
    else:
        q_mla = _mm(qa, lw['w_q_b'], BF16, scale=mla_scale, rope=rope[4:], rows_per_batch=t)

    ckvx = ckvx.reshape(b, t, -1)
    gk_all = gk.reshape(b, t, nk)
    gv_all = gv.reshape(b, t, nk)
    if ctx is not None:
        c_ckv, c_kr, c_k, c_v = ctx
        past = c_ckv.shape[1]
        c_x = jnp.concatenate([c_ckv, c_kr, jnp.zeros((b, past, LANE - MLA_ROPE), F32)], axis=-1).astype(BF16)
        ckvx = jnp.concatenate([ckvx, c_x], axis=1)
        gk_all = jnp.concatenate([gk_all, c_k.reshape(b, past, nk).astype(BF16)], axis=1)
        gv_all = jnp.concatenate([gv_all, c_v.reshape(b, past, nk).astype(BF16)], axis=1)
    s_len = ckvx.shape[1]
    kv = _mm(ckvx.reshape(b * s_len, -1), lw['w_kv'], BF16).reshape(b, s_len, -1)

    o_mla = _attention(q_mla.reshape(b, t, -1), kv, kv, MLA_HEADS, 1, MLA_QPAD, MLA_V,
                       0, MLA_HEADS * MLA_QPAD)
    o_gqa = _attention(gq.reshape(b, t, -1), gk_all, gv_all, GQA_KV_HEADS, GQA_HEADS // GQA_KV_HEADS,
                       HEAD_DIM, HEAD_DIM, 0, 0)
    merged = _merge(o_mla.reshape(m, -1), o_gqa.reshape(m, -1), lw['w_mla_o'], lw['w_gqa_o'],
                    proj, col_gate_mla, col_gate_gqa)
    x1 = _mm_residual(merged, lw['w_out'], x2, mods, rpb, GATE1)

    h2 = _norm_mod(x1, g_norm2, mods, rpb, SHIFT2, SCALE2)
    qp = _mm(h2, lw['w_peer_q'], BF16)
    idx_i, idx_j, gates = _route(qp, lw['sub_keys'])
    wmap = _wbuild(idx_i, idx_j, gates)
    po = _peer_mix(h2, lw['peer_ut'], lw['peer_v'], wmap)
    own = (ckv32.reshape(b, t, kv_rank), kr[:, :MLA_ROPE].reshape(b, t, MLA_ROPE),
           gk32.reshape(b, t, GQA_KV_HEADS, HEAD_DIM),
           proj[:, col_gate_mla - nk:col_gate_mla].reshape(b, t, GQA_KV_HEADS, HEAD_DIM))
    return x1, po, own


def kernel(x_prompt, x_sample, c, cache_mla_ckv, cache_mla_krope, cache_gqa_k, cache_gqa_v, c_ctx, w_mod, b_mod, g_norm1, g_norm2, w_in, g_q_a, w_q_b, g_kv_a, w_kv_b, g_gqa_q, g_gqa_k, w_mla_o, w_gqa_o, w_out, w_peer_q, peer_sub_keys, peer_u, peer_v, g_final):
    depth = w_in.shape[0]
    d = x_prompt.shape[-1]
    bc, tc = x_prompt.shape[:2]
    bl, tl = x_sample.shape[:2]

    cg, sg = _rope_tables(tl, HEAD_DIM)
    cm, sm = _rope_tables(tl, MLA_ROPE)
    one = jnp.ones((tl, LANE - MLA_ROPE), F32)
    zero = jnp.zeros((tl, LANE - MLA_ROPE), F32)
    cm_k, sm_k = jnp.concatenate([cm, one], 1), jnp.concatenate([sm, zero], 1)
    ones_n, zeros_n = jnp.ones((tl, MLA_NOPE), F32), jnp.zeros((tl, MLA_NOPE), F32)
    cq = jnp.tile(jnp.concatenate([ones_n, cm, one], 1), (1, 2))
    sq = jnp.tile(jnp.concatenate([zeros_n, sm, zero], 1), (1, 2))
    rope = (cg, sg, cm_k, sm_k, cq, sq)

    n_rows = bl + 1
    pad_rows = -n_rows % 8
    cvecs = jnp.concatenate([c, c_ctx[None, :], jnp.zeros((pad_rows, d), F32)], axis=0)

    xc, xl = x_prompt, x_sample
    new = [[], [], [], []]
    for l in range(depth):
        lw = _layer_weights(w_in[l], w_q_b[l], w_kv_b[l], w_mla_o[l], w_gqa_o[l], w_out[l],
                            w_peer_q[l], peer_sub_keys[l], peer_u[l], peer_v[l])
        gains = (g_norm1[l], g_norm2[l], g_q_a[l], g_kv_a[l], g_gqa_q[l], g_gqa_k[l])
        mod = _modulation(cvecs, w_mod[l], b_mod[l]).reshape(n_rows + pad_rows, N_MOD, d)
        mod = jnp.pad(mod, ((0, 0), (0, 8 - N_MOD), (0, 0)))
        mods_lat, mods_ctx = mod[:bl], mod[bl:bl + 1]
        last = l == depth - 1

        x1, po, own = _layer(xc, mods_ctx, lw, gains, None, None)
        for acc, o in zip(new, own):
            acc.append(o)
        xc = _finish(x1, po, mods_ctx, g_final, bc * tc, last).reshape(bc, tc, d)

        ctx = (cache_mla_ckv[:, l], cache_mla_krope[:, l], cache_gqa_k[:, l], cache_gqa_v[:, l])
        x1, po, _ = _layer(xl, mods_lat, lw, gains, rope, ctx)
        xl = _finish(x1, po, mods_lat, g_final, tl, last).reshape(bl, tl, d)

    return (xc, xl, jnp.stack(new[0], axis=1), jnp.stack(new[1], axis=1),
            jnp.stack(new[2], axis=1), jnp.stack(new[3], axis=1))


def _finish(x1, po, mods, g_final, rows_per_batch, last):
    d = x1.shape[1]
    g = g_final if last else None
    if g is None:
        raise NotImplementedError("only the final layer's residual is fused with the output norm")
    return _final(x1, po, mods, g, rows_per_batch, GATE2)
```

```python
import functools
import math

import jax
import jax.numpy as jnp
from jax import lax
from jax.experimental import pallas as pl
from jax.experimental.pallas import tpu as pltpu

F32 = jnp.float32
BF16 = jnp.bfloat16

GRID_W = 64
EPS = 1e-6
ROPE_BASE = 10000.0
N_MOD = 6
MLA_HEADS = 16
MLA_NOPE = 128
MLA_ROPE = 64
MLA_V = 128
MLA_QPAD = 256
GQA_HEADS = 16
GQA_KV_HEADS = 4
HEAD_DIM = 128
PEER_HEADS = 8
N_KEYS = 128
PEER_TOPK = 16
LANE = 128
SUBLANE = 8
VMEM_LIMIT = 56 * 1024 * 1024
INV_SQRT2 = 1.0 / math.sqrt(2.0)
LOG2E = 1.0 / math.log(2.0)

SHIFT1, SCALE1, GATE1, SHIFT2, SCALE2, GATE2 = range(6)


def _tile(dim, pref):
    t = min(dim, pref)
    assert dim % t == 0, (dim, pref)
    return t


def _params(*sem):
    return pltpu.CompilerParams(dimension_semantics=sem, vmem_limit_bytes=VMEM_LIMIT)


def _mod_kernel(c_ref, w_ref, b_ref, o_ref):
    c = c_ref[...]
    a = (c * jax.nn.sigmoid(c)).astype(BF16)
    o_ref[...] = jnp.dot(a, w_ref[...].astype(BF16), preferred_element_type=F32) + b_ref[...]


def _modulation(cvecs, w_mod, b_mod):
    r, d = cvecs.shape
    n = w_mod.shape[1]
    tn = _tile(n, 512)
    return pl.pallas_call(
        _mod_kernel,
        grid=(n // tn,),
        in_specs=[pl.BlockSpec((r, d), lambda j: (0, 0)),
                  pl.BlockSpec((d, tn), lambda j: (0, j)),
                  pl.BlockSpec((1, tn), lambda j: (0, j))],
        out_specs=pl.BlockSpec((r, tn), lambda j: (0, j)),
        out_shape=jax.ShapeDtypeStruct((r, n), F32),
        compiler_params=_params("parallel"),
        name="modulation",
    )(cvecs, w_mod, b_mod.reshape(1, n))


def _norm_mod_kernel(x_ref, g_ref, m_ref, o_ref, *, shift_row, scale_row):
    x = x_ref[...]
    y = x * lax.rsqrt(jnp.mean(x * x, axis=-1, keepdims=True) + EPS) * g_ref[...]
    m = m_ref[0]
    o_ref[...] = (y * (1.0 + m[scale_row:scale_row + 1]) + m[shift_row:shift_row + 1]).astype(BF16)


def _norm_mod(x, g, mods, rows_per_batch, shift_row, scale_row):
    m, d = x.shape
    tm = _tile(rows_per_batch, 256)
    per = rows_per_batch // tm
    return pl.pallas_call(
        functools.partial(_norm_mod_kernel, shift_row=shift_row, scale_row=scale_row),
        grid=(m // tm,),
        in_specs=[pl.BlockSpec((tm, d), lambda i: (i, 0)),
                  pl.BlockSpec((1, d), lambda i: (0, 0)),
                  pl.BlockSpec((1, 8, d), lambda i: (i // per, 0, 0))],
        out_specs=pl.BlockSpec((tm, d), lambda i: (i, 0)),
        out_shape=jax.ShapeDtypeStruct((m, d), BF16),
        compiler_params=_params("parallel"),
        name="norm_mod",
    )(x, g.reshape(1, d), mods)


def _swap_pairs(x):
    ax = x.ndim - 1
    n = x.shape[ax]
    lane = lax.broadcasted_iota(jnp.int32, x.shape, ax)
    nxt = pltpu.roll(x, n - 1, axis=ax)
    prv = pltpu.roll(x, 1, axis=ax)
    return jnp.where((lane & 1) == 0, nxt, prv)


def _mm_kernel(a_ref, b_ref, o_ref, *, scale):
    acc = jnp.dot(a_ref[...], b_ref[...], preferred_element_type=F32)
    if scale is not None:
        acc = acc * scale
    o_ref[...] = acc.astype(o_ref.dtype)


def _mm_rope_kernel(a_ref, b_ref, cos_ref, sin_ref, o_ref, *, scale):
    acc = jnp.dot(a_ref[...], b_ref[...], preferred_element_type=F32)
    acc = acc * cos_ref[...] + _swap_pairs(acc) * sin_ref[...]
    o_ref[...] = (acc * scale).astype(o_ref.dtype)


def _mm(a, b, out_dtype, scale=None, rope=None, rows_per_batch=None, tm=1024, tn=512):
    m, k = a.shape
    n = b.shape[1]
    tm = _tile(m if rows_per_batch is None else rows_per_batch, tm)
    tn = _tile(n, tn)
    in_specs = [pl.BlockSpec((tm, k), lambda i, j: (i, 0)),
                pl.BlockSpec((k, tn), lambda i, j: (0, j))]
    args = [a, b]
    if rope is None:
        body = functools.partial(_mm_kernel, scale=scale)
    else:
        per = rows_per_batch // tm
        body = functools.partial(_mm_rope_kernel, scale=scale)
        in_specs += [pl.BlockSpec((tm, tn), lambda i, j: (i % per, 0))] * 2
        args += list(rope)
    return pl.pallas_call(
        body,
        grid=(m // tm, n // tn),
        in_specs=in_specs,
        out_specs=pl.BlockSpec((tm, tn), lambda i, j: (i, j)),
        out_shape=jax.ShapeDtypeStruct((m, n), out_dtype),
        compiler_params=_params("parallel", "parallel"),
        name="matmul",
    )(*args)


def _mm_res_kernel(a_ref, b_ref, x_ref, m_ref, o_ref, *, gate_row):
    acc = jnp.dot(a_ref[...], b_ref[...], preferred_element_type=F32)
    o_ref[...] = x_ref[...] + m_ref[0][gate_row:gate_row + 1] * acc


def _mm_residual(a, b, x, mods, rows_per_batch, gate_row, tm=1024, tn=512):
    m, k = a.shape
    n = b.shape[1]
    tm = _tile(rows_per_batch, tm)
    tn = _tile(n, tn)
    per = rows_per_batch // tm
    return pl.pallas_call(
        functools.partial(_mm_res_kernel, gate_row=gate_row),
        grid=(m // tm, n // tn),
        in_specs=[pl.BlockSpec((tm, k), lambda i, j: (i, 0)),
                  pl.BlockSpec((k, tn), lambda i, j: (0, j)),
                  pl.BlockSpec((tm, tn), lambda i, j: (i, j)),
                  pl.BlockSpec((1, 8, tn), lambda i, j: (i // per, 0, j))],
        out_specs=pl.BlockSpec((tm, tn), lambda i, j: (i, j)),
        out_shape=jax.ShapeDtypeStruct((m, n), F32),
        compiler_params=_params("parallel", "parallel"),
        name="matmul_residual",
    )(a, b, x, mods)


def _merge_kernel(oa_ref, ob_ref, wa_ref, wb_ref, ga_ref, gb_ref, o_ref):
    ya = jnp.dot(oa_ref[...], wa_ref[...], preferred_element_type=F32)
    yb = jnp.dot(ob_ref[...], wb_ref[...], preferred_element_type=F32)
    o_ref[...] = (jax.nn.sigmoid(ga_ref[...]) * ya + jax.nn.sigmoid(gb_ref[...]) * yb).astype(o_ref.dtype)


def _merge(o_mla, o_gqa, w_mla_o, w_gqa_o, proj, col_gate_mla, col_gate_gqa, tm=1024, tn=512):
    m, k = o_mla.shape
    n = w_mla_o.shape[1]
    tm = _tile(m, tm)
    tn = _tile(n, tn)
    ca, cb = col_gate_mla // tn, col_gate_gqa // tn
    assert ca * tn == col_gate_mla and cb * tn == col_gate_gqa
    return pl.pallas_call(
        _merge_kernel,
        grid=(m // tm, n // tn),
        in_specs=[pl.BlockSpec((tm, k), lambda i, j: (i, 0)),
                  pl.BlockSpec((tm, k), lambda i, j: (i, 0)),
                  pl.BlockSpec((k, tn), lambda i, j: (0, j)),
                  pl.BlockSpec((k, tn), lambda i, j: (0, j)),
                  pl.BlockSpec((tm, tn), lambda i, j: (i, ca + j)),
                  pl.BlockSpec((tm, tn), lambda i, j: (i, cb + j))],
        out_specs=pl.BlockSpec((tm, tn), lambda i, j: (i, j)),
        out_shape=jax.ShapeDtypeStruct((m, n), BF16),
        compiler_params=_params("parallel", "parallel"),
        name="merge",
    )(o_mla, o_gqa, w_mla_o, w_gqa_o, proj, proj)


def _rms(x, g):
    return x * lax.rsqrt(jnp.mean(x * x, axis=-1, keepdims=True) + EPS) * g


def _prep_kernel(*refs, q_rank, kv_rank, use_rope):
    if use_rope:
        (p_ref, kr_ref, gqa_ref, gkv_ref, ggq_ref, ggk_ref, cg_ref, sg_ref, cm_ref, sm_ref,
         qa_o, ckv_o, ckvx_o, gq_o, gk32_o, gk_o, gv_o) = refs
    else:
        (p_ref, kr_ref, gqa_ref, gkv_ref, ggq_ref, ggk_ref,
         qa_o, ckv_o, ckvx_o, gq_o, gk32_o, gk_o, gv_o) = refs
    c0 = q_rank
    c1 = c0 + kv_rank
    c2 = c1 + GQA_HEADS * HEAD_DIM
    c3 = c2 + GQA_KV_HEADS * HEAD_DIM
    qa_o[...] = _rms(p_ref[:, 0:c0], gqa_ref[...]).astype(BF16)
    ckv = _rms(p_ref[:, c0:c1], gkv_ref[...])
    ckv_o[...] = ckv
    ckvx_o[:, 0:kv_rank] = ckv.astype(BF16)
    kr = kr_ref[...]
    if use_rope:
        kr = kr * cm_ref[...] + _swap_pairs(kr) * sm_ref[...]
    ckvx_o[:, kv_rank:kv_rank + LANE] = kr.astype(BF16)
    q_scale = HEAD_DIM ** -0.5 * LOG2E
    for h in range(GQA_HEADS):
        x = _rms(p_ref[:, c1 + h * HEAD_DIM:c1 + (h + 1) * HEAD_DIM], ggq_ref[...])
        if use_rope:
            x = x * cg_ref[...] + _swap_pairs(x) * sg_ref[...]
        gq_o[:, h * HEAD_DIM:(h + 1) * HEAD_DIM] = (x * q_scale).astype(BF16)
    for h in range(GQA_KV_HEADS):
        x = _rms(p_ref[:, c2 + h * HEAD_DIM:c2 + (h + 1) * HEAD_DIM], ggk_ref[...])
        if use_rope:
            x = x * cg_ref[...] + _swap_pairs(x) * sg_ref[...]
        gk32_o[:, h * HEAD_DIM:(h + 1) * HEAD_DIM] = x
        gk_o[:, h * HEAD_DIM:(h + 1) * HEAD_DIM] = x.astype(BF16)
    gv_o[...] = p_ref[:, c3:c3 + GQA_KV_HEADS * HEAD_DIM].astype(BF16)


def _prep(proj, kr, g_q_a, g_kv_a, g_gqa_q, g_gqa_k, rope, rows_per_batch):
    m = proj.shape[0]
    q_rank, kv_rank = g_q_a.shape[0], g_kv_a.shape[0]
    width = q_rank + kv_rank + (GQA_HEADS + 2 * GQA_KV_HEADS) * HEAD_DIM
    tm = _tile(rows_per_batch, 256)
    per = rows_per_batch // tm
    nq, nk = GQA_HEADS * HEAD_DIM, GQA_KV_HEADS * HEAD_DIM
    row = lambda i: (i, 0)
    const = lambda i: (0, 0)
    in_specs = [pl.BlockSpec((tm, width), row), pl.BlockSpec((tm, LANE), row),
                pl.BlockSpec((1, q_rank), const), pl.BlockSpec((1, kv_rank), const),
                pl.BlockSpec((1, HEAD_DIM), const), pl.BlockSpec((1, HEAD_DIM), const)]
    args = [proj, kr, g_q_a.reshape(1, -1), g_kv_a.reshape(1, -1), g_gqa_q.reshape(1, -1), g_gqa_k.reshape(1, -1)]
    if rope is not None:
        in_specs += [pl.BlockSpec((tm, LANE), lambda i: (i % per, 0))] * 4
        args += list(rope)
    outs = [((m, q_rank), BF16), ((m, kv_rank), F32), ((m, kv_rank + LANE), BF16), ((m, nq), BF16),
            ((m, nk), F32), ((m, nk), BF16), ((m, nk), BF16)]
    return pl.pallas_call(
        functools.partial(_prep_kernel, q_rank=q_rank, kv_rank=kv_rank, use_rope=rope is not None),
        grid=(m // tm,),
        in_specs=in_specs,
        out_specs=[pl.BlockSpec((tm, s[1]), row) for s, _ in outs],
        out_shape=[jax.ShapeDtypeStruct(s, dt) for s, dt in outs],
        compiler_params=_params("parallel"),
        name="prep",
    )(*args)


def _attn_kernel(q_ref, k_ref, v_ref, o_ref, *, rep, dq, dv, sub):
    k = k_ref[...]
    v = v_ref[...]
    for r in range(rep):
        for i in range(q_ref.shape[0] // sub):
            rows = slice(i * sub, (i + 1) * sub)
            q = q_ref[rows, r * dq:(r + 1) * dq]
            s = lax.dot_general(q, k, (((1,), (1,)), ((), ())), preferred_element_type=F32)
            p = jnp.exp2(s - jnp.max(s, axis=-1, keepdims=True))
            l = jnp.sum(p, axis=-1, keepdims=True)
            o = jnp.dot(p.astype(BF16), v, preferred_element_type=F32)
            o_ref[rows, r * dv:(r + 1) * dv] = (o / l).astype(o_ref.dtype)


def _attention(q, k, v, n_groups, rep, dq, dv, k_col0, v_col0, tq):
    b, t, _ = q.shape
    s_len = k.shape[1]
    tq = _tile(t, tq)
    sub = _tile(tq, 256)
    kb, vb = k_col0 // dq, v_col0 // dv
    assert kb * dq == k_col0 and vb * dv == v_col0
    return pl.pallas_call(
        functools.partial(_attn_kernel, rep=rep, dq=dq, dv=dv, sub=sub),
        grid=(b, n_groups, t // tq),
        in_specs=[pl.BlockSpec((None, tq, rep * dq), lambda bi, g, qi: (bi, qi, g)),
                  pl.BlockSpec((None, s_len, dq), lambda bi, g, qi: (bi, 0, kb + g)),
                  pl.BlockSpec((None, s_len, dv), lambda bi, g, qi: (bi, 0, vb + g))],
        out_specs=pl.BlockSpec((None, tq, rep * dv), lambda bi, g, qi: (bi, qi, g)),
        out_shape=jax.ShapeDtypeStruct((b, t, n_groups * rep * dv), BF16),
        compiler_params=_params("parallel", "parallel", "parallel"),
        name="attention",
    )(q, k, v)


def _top16(s, idx):
    n, c = s.shape
    row = lax.broadcasted_iota(jnp.int32, (PEER_TOPK, c), 0)
    big = float(PEER_TOPK * PEER_TOPK)

    def body(r, carry):
        s, vals, idxs = carry
        mx = jnp.max(s, axis=0, keepdims=True)
        am = jnp.min(jnp.where(s == mx, idx, big), axis=0, keepdims=True)
        vals = jnp.where(row == r, mx, vals)
        idxs = jnp.where(row == r, am, idxs)
        s = jnp.where(idx == am, -jnp.inf, s)
        return s, vals, idxs

    init = (s, jnp.zeros((PEER_TOPK, c), F32), jnp.zeros((PEER_TOPK, c), F32))
    _, vals, idxs = lax.fori_loop(0, PEER_TOPK, body, init)
    return vals, idxs


def _pick(table, pos):
    out = jnp.zeros(pos.shape, F32)
    for a in range(PEER_TOPK):
        out = jnp.where(pos == a, table[a:a + 1], out)
    return out


_CAND_ROWS = [(0, 0), (0, 8), (1, 0), (2, 0), (3, 0), (4, 0), (5, 0), (6, 0), (7, 0)]


def _route_kernel(q_ref, k_ref, oi_ref, oj_ref, og_ref):
    q = q_ref[...]
    tn = q.shape[0]
    half = q.shape[1] // 2
    nt = (((1,), (1,)), ((), ()))
    s0 = lax.dot_general(k_ref[0], q[:, :half], nt, preferred_element_type=F32)
    s1 = lax.dot_general(k_ref[1], q[:, half:], nt, preferred_element_type=F32)
    s01 = jnp.concatenate([s0, s1], axis=1)
    v01, i01 = _top16(s01, lax.broadcasted_iota(jnp.int32, s01.shape, 0).astype(F32))
    v0, v1, i0, i1 = v01[:, :tn], v01[:, tn:], i01[:, :tn], i01[:, tn:]
    sub = lax.broadcasted_iota(jnp.int32, (SUBLANE, tn), 0).astype(F32)
    cand = [v0[a:a + 1] + v1[b:b + SUBLANE] for a, b in _CAND_ROWS]
    flat = [sub + float(a * PEER_TOPK + b) for a, b in _CAND_ROWS]
    cand.append(v0[SUBLANE:] + v1[0:1])
    flat.append((sub + float(SUBLANE)) * float(PEER_TOPK))
    best, pos = _top16(jnp.concatenate(cand, axis=0), jnp.concatenate(flat, axis=0))
    pos = pos.astype(jnp.int32)
    e = jnp.exp(best - best[0:1])
    oi_ref[...] = _pick(i0, pos >> 4)
    oj_ref[...] = _pick(i1, pos & (PEER_TOPK - 1))
    og_ref[...] = e / jnp.sum(e, axis=0, keepdims=True)


def _route(qp, sub_keys, tn=256):
    m = qp.shape[0]
    tn = _tile(m, tn)
    qd = qp.shape[1] // PEER_HEADS
    shp = jax.ShapeDtypeStruct((PEER_HEADS * PEER_TOPK, m), F32)
    spec = pl.BlockSpec((PEER_TOPK, tn), lambda t, h: (h, t))
    return pl.pallas_call(
        _route_kernel,
        grid=(m // tn, PEER_HEADS),
        in_specs=[pl.BlockSpec((tn, qd), lambda t, h: (t, h)),
                  pl.BlockSpec(sub_keys.shape, lambda t, h: (0, 0, 0))],
        out_specs=[spec, spec, spec],
        out_shape=[shp, shp, shp],
        compiler_params=_params("parallel", "parallel"),
        name="peer_route",
    )(qp, sub_keys)


def _wbuild_kernel(i_ref, j_ref, g_ref, o_ref, it_s, jt_s, gt_s):
    it_s[...] = i_ref[...].T
    jt_s[...] = j_ref[...].T
    gt_s[...] = g_ref[...].T
    ne = i_ref.shape[0]
    iota_a = lax.broadcasted_iota(jnp.int32, (2 * N_KEYS, 2 * ne), 0).astype(F32).astype(BF16)
    iota_g = lax.broadcasted_iota(jnp.int32, (N_KEYS, 2 * ne), 0).astype(F32).astype(BF16)
    one, zero = jnp.ones((), BF16), jnp.zeros((), BF16)

    def body(m, carry):
        ri = it_s[pl.ds(2 * m, 2), :]
        rj = jt_s[pl.ds(2 * m, 2), :]
        rg = gt_s[pl.ds(2 * m, 2), :]
        row_i = jnp.concatenate([ri[0:1], ri[1:2] + float(N_KEYS)], axis=1).astype(BF16)
        row_j = jnp.concatenate([rj[0:1], rj[1:2]], axis=1).astype(BF16)
        row_g = jnp.concatenate([rg[0:1], rg[1:2]], axis=1).astype(BF16)
        a2 = jnp.where(iota_a == row_i, one, zero)
        g2 = jnp.where(iota_g == row_j, row_g, zero)
        w = lax.dot_general(a2, g2, (((1,), (1,)), ((), ())), preferred_element_type=F32)
        o_ref[pl.ds(2 * m, 2)] = w.reshape(2, N_KEYS, N_KEYS)
        return carry

    lax.fori_loop(0, o_ref.shape[0] // 2, body, 0, unroll=4)


def _wbuild(idx_i, idx_j, gates):
    ne, m = idx_i.shape
    tw = LANE
    spec = pl.BlockSpec((ne, tw), lambda t: (0, t))
    return pl.pallas_call(
        _wbuild_kernel,
        grid=(m // tw,),
        in_specs=[spec, spec, spec],
        out_specs=pl.BlockSpec((tw, N_KEYS, N_KEYS), lambda t: (t, 0, 0)),
        out_shape=jax.ShapeDtypeStruct((m, N_KEYS, N_KEYS), F32),
        scratch_shapes=[pltpu.VMEM((tw, ne), F32)] * 3,
        compiler_params=_params("parallel"),
        name="peer_gate_map",
    )(idx_i, idx_j, gates)


def _peer_kernel(x_ref, ut_ref, v_ref, w_ref, o_ref, g_s, *, tn, rows):
    c = pl.program_id(1)

    @pl.when(c == 0)
    def _():
        o_ref[...] = jnp.zeros_like(o_ref)

    s = jnp.dot(x_ref[...], ut_ref[...], preferred_element_type=F32)
    base = (c % (SUBLANE // rows)) * rows
    for il in range(rows):
        wg = w_ref[:, base + il, :]
        sl = s[:, il * N_KEYS:(il + 1) * N_KEYS]
        g = 0.5 * sl * (1.0 + lax.erf(sl * INV_SQRT2)) * wg
        g_s[:, il * N_KEYS:(il + 1) * N_KEYS] = g.astype(BF16)
    o_ref[...] += jnp.dot(g_s[...], v_ref[...], preferred_element_type=F32)


def _peer_mix(h2, u_t, v, wmap, tn=512, rows=4):
    m, d = h2.shape
    n_exp = v.shape[0]
    tn = _tile(m, tn)
    ec = rows * N_KEYS
    per = SUBLANE // rows
    return pl.pallas_call(
        functools.partial(_peer_kernel, tn=tn, rows=rows),
        grid=(m // tn, n_exp // ec),
        in_specs=[pl.BlockSpec((tn, d), lambda t, c: (t, 0)),
                  pl.BlockSpec((d, ec), lambda t, c: (0, c)),
                  pl.BlockSpec((ec, d), lambda t, c: (c, 0)),
                  pl.BlockSpec((tn, SUBLANE, N_KEYS), lambda t, c: (t, c // per, 0))],
        out_specs=pl.BlockSpec((tn, d), lambda t, c: (t, 0)),
        out_shape=jax.ShapeDtypeStruct((m, d), F32),
        scratch_shapes=[pltpu.VMEM((tn, ec), BF16)],
        compiler_params=_params("parallel", "arbitrary"),
        name="peer_mix",
    )(h2, u_t, v, wmap)


def _final_kernel(x_ref, p_ref, m_ref, g_ref, o_ref, *, gate_row):
    x = x_ref[...] + m_ref[0][gate_row:gate_row + 1] * p_ref[...]
    o_ref[...] = _rms(x, g_ref[...])


def _final(x, po, mods, g_final, rows_per_batch, gate_row):
    m, d = x.shape
    tm = _tile(rows_per_batch, 256)
    per = rows_per_batch // tm
    row = lambda i: (i, 0)
    return pl.pallas_call(
        functools.partial(_final_kernel, gate_row=gate_row),
        grid=(m // tm,),
        in_specs=[pl.BlockSpec((tm, d), row), pl.BlockSpec((tm, d), row),
                  pl.BlockSpec((1, 8, d), lambda i: (i // per, 0, 0)),
                  pl.BlockSpec((1, d), lambda i: (0, 0))],
        out_specs=pl.BlockSpec((tm, d), row),
        out_shape=jax.ShapeDtypeStruct((m, d), F32),
        compiler_params=_params("parallel"),
        name="final_norm",
    )(x, po, mods, g_final.reshape(1, d))


def _rope_tables(n_tokens, dim):
    n_rows = n_tokens // GRID_W
    rows = jnp.broadcast_to(jnp.arange(n_rows)[:, None], (n_rows, GRID_W)).reshape(-1).astype(F32)
    cols = jnp.broadcast_to(jnp.arange(GRID_W)[None, :], (n_rows, GRID_W)).reshape(-1).astype(F32)
    n_freq = dim // 4
    freqs = ROPE_BASE ** (-jnp.arange(n_freq, dtype=F32) / n_freq)
    ang = jnp.concatenate([rows[:, None] * freqs, cols[:, None] * freqs], axis=-1)
    cos, sin = jnp.cos(ang), jnp.sin(ang)
    cos2 = jnp.repeat(cos, 2, axis=-1)
    sin2 = jnp.stack([-sin, sin], axis=-1).reshape(n_tokens, dim)
    return cos2, sin2


def _layer_weights(w_in, w_q_b, w_kv_b, w_mla_o, w_gqa_o, w_out, w_peer_q, sub_keys, peer_u, peer_v):
    q_rank, kv_rank = w_q_b.shape[0], w_kv_b.shape[0]
    k0 = q_rank + kv_rank
    lw = {}
    lw['w_in'] = jnp.concatenate([w_in[:, :k0], w_in[:, k0 + MLA_ROPE:]], axis=1).astype(BF16)
    lw['w_kr'] = jnp.pad(w_in[:, k0:k0 + MLA_ROPE], ((0, 0), (0, LANE - MLA_ROPE))).astype(BF16)
    wq = w_q_b.reshape(q_rank, MLA_HEADS, MLA_NOPE + MLA_ROPE)
    wq = jnp.pad(wq, ((0, 0), (0, 0), (0, MLA_QPAD - MLA_NOPE - MLA_ROPE)))
    lw['w_q_b'] = wq.reshape(q_rank, MLA_HEADS * MLA_QPAD).astype(BF16)
    wkv = w_kv_b.reshape(kv_rank, MLA_HEADS, MLA_NOPE + MLA_V)
    wk = jnp.zeros((kv_rank + LANE, MLA_HEADS, MLA_QPAD), F32)
    wk = wk.at[:kv_rank, :, :MLA_NOPE].set(wkv[:, :, :MLA_NOPE])
    eye = jnp.broadcast_to(jnp.eye(MLA_ROPE, dtype=F32)[:, None, :], (MLA_ROPE, MLA_HEADS, MLA_ROPE))
    wk = wk.at[kv_rank:kv_rank + MLA_ROPE, :, MLA_NOPE:MLA_NOPE + MLA_ROPE].set(eye)
    wv = jnp.pad(wkv[:, :, MLA_NOPE:], ((0, LANE), (0, 0), (0, 0)))
    lw['w_kv'] = jnp.concatenate([wk.reshape(kv_rank + LANE, -1), wv.reshape(kv_rank + LANE, -1)],
                                 axis=1).astype(BF16)
    lw['w_mla_o'] = w_mla_o.astype(BF16)
    lw['w_gqa_o'] = w_gqa_o.astype(BF16)
    lw['w_out'] = w_out.astype(BF16)
    lw['w_peer_q'] = w_peer_q.astype(BF16)
    lw['sub_keys'] = sub_keys.astype(BF16)
    lw['peer_ut'] = peer_u.T.astype(BF16)
    lw['peer_v'] = peer_v.astype(BF16)
    return lw


def _layer(x, mods, lw, gains, rope, ctx):
    b, t, d = x.shape
    m = b * t
    x2 = x.reshape(m, d)
    g_norm1, g_norm2, g_q_a, g_kv_a, g_gqa_q, g_gqa_k = gains
    q_rank, kv_rank = g_q_a.shape[0], g_kv_a.shape[0]
    nq, nk = GQA_HEADS * HEAD_DIM, GQA_KV_HEADS * HEAD_DIM
    col_gate_mla = q_rank + kv_rank + nq + 2 * nk
    col_gate_gqa = col_gate_mla + d

    rpb = t if mods.shape[0] == b else m
    h = _norm_mod(x2, g_norm1, mods, rpb, SHIFT1, SCALE1)
    proj = _mm(h, lw['w_in'], F32)
    kr = _mm(h, lw['w_kr'], F32)
    prep_rope = None if rope is None else rope[:4]
    qa, ckv32, ckvx, gq, gk32, gk, gv = _prep(proj, kr, g_q_a, g_kv_a, g_gqa_q, g_gqa_k, prep_rope, t)

    mla_scale = (MLA_NOPE + MLA_ROPE) ** -0.5 * LOG2E
    if rope is None:
        q_mla = _mm(qa, lw['w_q_b'], BF16, scale=mla_scale)
    else:
        q_mla = _mm(qa, lw['w_q_b'], BF16, scale=mla_scale, rope=rope[4:], rows_per_batch=t)

    ckvx = ckvx.reshape(b, t, -1)
    gk_all = gk.reshape(b, t, nk)
    gv_all = gv.reshape(b, t, nk)
    if ctx is not None:
        c_ckv, c_kr, c_k, c_v = ctx
        past = c_ckv.shape[1]
        c_x = jnp.concatenate([c_ckv, c_kr, jnp.zeros((b, past, LANE - MLA_ROPE), F32)], axis=-1).astype(BF16)
        ckvx = jnp.concatenate([ckvx, c_x], axis=1)
        gk_all = jnp.concatenate([gk_all, c_k.reshape(b, past, nk).astype(BF16)], axis=1)
        gv_all = jnp.concatenate([gv_all, c_v.reshape(b, past, nk).astype(BF16)], axis=1)
    s_len = ckvx.shape[1]
    kv = _mm(ckvx.reshape(b * s_len, -1), lw['w_kv'], BF16).reshape(b, s_len, -1)

    o_mla = _attention(q_mla.reshape(b, t, -1), kv, kv, MLA_HEADS, 1, MLA_QPAD, MLA_V,
                       0, MLA_HEADS * MLA_QPAD, tq=512)
    o_gqa = _attention(gq.reshape(b, t, -1), gk_all, gv_all, GQA_KV_HEADS, GQA_HEADS // GQA_KV_HEADS,
                       HEAD_DIM, HEAD_DIM, 0, 0, tq=256)
    merged = _merge(o_mla.reshape(m, -1), o_gqa.reshape(m, -1), lw['w_mla_o'], lw['w_gqa_o'],
                    proj, col_gate_mla, col_gate_gqa)
    x1 = _mm_residual(merged, lw['w_out'], x2, mods, rpb, GATE1)

    h2 = _norm_mod(x1, g_norm2, mods, rpb, SHIFT2, SCALE2)
    qp = _mm(h2, lw['w_peer_q'], BF16)
    idx_i, idx_j, gates = _route(qp, lw['sub_keys'])
    wmap = _wbuild(idx_i, idx_j, gates)
    po = _peer_mix(h2, lw['peer_ut'], lw['peer_v'], wmap)
    own = (ckv32.reshape(b, t, kv_rank), kr[:, :MLA_ROPE].reshape(b, t, MLA_ROPE),
           gk32.reshape(b, t, GQA_KV_HEADS, HEAD_DIM),
           proj[:, col_gate_mla - nk:col_gate_mla].reshape(b, t, GQA_KV_HEADS, HEAD_DIM))
    return x1, po, own


def kernel(x_prompt, x_sample, c, cache_mla_ckv, cache_mla_krope, cache_gqa_k, cache_gqa_v, c_ctx, w_mod, b_mod, g_norm1, g_norm2, w_in, g_q_a, w_q_b, g_kv_a, w_kv_b, g_gqa_q, g_gqa_k, w_mla_o, w_gqa_o, w_out, w_peer_q, peer_sub_keys, peer_u, peer_v, g_final):
    depth = w_in.shape[0]
    d = x_prompt.shape[-1]
    bc, tc = x_prompt.shape[:2]
    bl, tl = x_sample.shape[:2]

    cg, sg = _rope_tables(tl, HEAD_DIM)
    cm, sm = _rope_tables(tl, MLA_ROPE)
    one = jnp.ones((tl, LANE - MLA_ROPE), F32)
    zero = jnp.zeros((tl, LANE - MLA_ROPE), F32)
    cm_k, sm_k = jnp.concatenate([cm, one], 1), jnp.concatenate([sm, zero], 1)
    ones_n, zeros_n = jnp.ones((tl, MLA_NOPE), F32), jnp.zeros((tl, MLA_NOPE), F32)
    cq = jnp.tile(jnp.concatenate([ones_n, cm, one], 1), (1, 2))
    sq = jnp.tile(jnp.concatenate([zeros_n, sm, zero], 1), (1, 2))
    rope = (cg, sg, cm_k, sm_k, cq, sq)

    n_rows = bl + 1
    pad_rows = -n_rows % 8
    cvecs = jnp.concatenate([c, c_ctx[None, :], jnp.zeros((pad_rows, d), F32)], axis=0)

    xc, xl = x_prompt, x_sample
    new = [[], [], [], []]
    for l in range(depth):
        lw = _layer_weights(w_in[l], w_q_b[l], w_kv_b[l], w_mla_o[l], w_gqa_o[l], w_out[l],
                            w_peer_q[l], peer_sub_keys[l], peer_u[l], peer_v[l])
        gains = (g_norm1[l], g_norm2[l], g_q_a[l], g_kv_a[l], g_gqa_q[l], g_gqa_k[l])
        mod = _modulation(cvecs, w_mod[l], b_mod[l]).reshape(n_rows + pad_rows, N_MOD, d)
        mod = jnp.pad(mod, ((0, 0), (0, 8 - N_MOD), (0, 0)))
        mods_lat, mods_ctx = mod[:bl], mod[bl:bl + 1]
        last = l == depth - 1

        x1, po, own = _layer(xc, mods_ctx, lw, gains, None, None)
        for acc, o in zip(new, own):
            acc.append(o)
        xc = _finish(x1, po, mods_ctx, g_final, bc * tc, last).reshape(bc, tc, d)

        ctx = (cache_mla_ckv[:, l], cache_mla_krope[:, l], cache_gqa_k[:, l], cache_gqa_v[:, l])
        x1, po, _ = _layer(xl, mods_lat, lw, gains, rope, ctx)
        xl = _finish(x1, po, mods_lat, g_final, tl, last).reshape(bl, tl, d)

    return (xc, xl, jnp.stack(new[0], axis=1), jnp.stack(new[1], axis=1),
            jnp.stack(new[2], axis=1), jnp.stack(new[3], axis=1))


def _finish(x1, po, mods, g_final, rows_per_batch, last):
    d = x1.shape[1]
    g = g_final if last else None
    if g is None:
        raise NotImplementedError("only the final layer's residual is fused with the output norm")
    return _final(x1, po, mods, g, rows_per_batch, GATE2)
```

```python
import functools
import math

import jax
import jax.numpy as jnp
from jax import lax
from jax.experimental import pallas as pl
from jax.experimental.pallas import tpu as pltpu

F32 = jnp.float32
BF16 = jnp.bfloat16

GRID_W = 64
EPS = 1e-6
ROPE_BASE = 10000.0
N_MOD = 6
MLA_HEADS = 16
MLA_NOPE = 128
MLA_ROPE = 64
MLA_V = 128
MLA_QPAD = 256
GQA_HEADS = 16
GQA_KV_HEADS = 4
HEAD_DIM = 128
PEER_HEADS = 8
N_KEYS = 128
PEER_TOPK = 16
LANE = 128
SUBLANE = 8
VMEM_LIMIT = 56 * 1024 * 1024
INV_SQRT2 = 1.0 / math.sqrt(2.0)
LOG2E = 1.0 / math.log(2.0)

SHIFT1, SCALE1, GATE1, SHIFT2, SCALE2, GATE2 = range(6)


def _tile(dim, pref):
    t = min(dim, pref)
    assert dim % t == 0, (dim, pref)
    return t


def _params(*sem):
    return pltpu.CompilerParams(dimension_semantics=sem, vmem_limit_bytes=VMEM_LIMIT)


def _mod_kernel(c_ref, w_ref, b_ref, o_ref):
    c = c_ref[...]
    a = (c * jax.nn.sigmoid(c)).astype(BF16)
    o_ref[...] = jnp.dot(a, w_ref[...].astype(BF16), preferred_element_type=F32) + b_ref[...]


def _modulation(cvecs, w_mod, b_mod):
    r, d = cvecs.shape
    n = w_mod.shape[1]
    tn = _tile(n, 512)
    return pl.pallas_call(
        _mod_kernel,
        grid=(n // tn,),
        in_specs=[pl.BlockSpec((r, d), lambda j: (0, 0)),
                  pl.BlockSpec((d, tn), lambda j: (0, j)),
                  pl.BlockSpec((1, tn), lambda j: (0, j))],
        out_specs=pl.BlockSpec((r, tn), lambda j: (0, j)),
        out_shape=jax.ShapeDtypeStruct((r, n), F32),
        compiler_params=_params("parallel"),
        name="modulation",
    )(cvecs, w_mod, b_mod.reshape(1, n))


def _norm_mod_kernel(x_ref, g_ref, m_ref, o_ref, *, shift_row, scale_row):
    x = x_ref[...]
    y = x * lax.rsqrt(jnp.mean(x * x, axis=-1, keepdims=True) + EPS) * g_ref[...]
    m = m_ref[0]
    o_ref[...] = (y * (1.0 + m[scale_row:scale_row + 1]) + m[shift_row:shift_row + 1]).astype(BF16)


def _norm_mod(x, g, mods, rows_per_batch, shift_row, scale_row):
    m, d = x.shape
    tm = _tile(rows_per_batch, 256)
    per = rows_per_batch // tm
    return pl.pallas_call(
        functools.partial(_norm_mod_kernel, shift_row=shift_row, scale_row=scale_row),
        grid=(m // tm,),
        in_specs=[pl.BlockSpec((tm, d), lambda i: (i, 0)),
                  pl.BlockSpec((1, d), lambda i: (0, 0)),
                  pl.BlockSpec((1, 8, d), lambda i: (i // per, 0, 0))],
        out_specs=pl.BlockSpec((tm, d), lambda i: (i, 0)),
        out_shape=jax.ShapeDtypeStruct((m, d), BF16),
        compiler_params=_params("parallel"),
        name="norm_mod",
    )(x, g.reshape(1, d), mods)


def _swap_pairs(x):
    ax = x.ndim - 1
    n = x.shape[ax]
    lane = lax.broadcasted_iota(jnp.int32, x.shape, ax)
    nxt = pltpu.roll(x, n - 1, axis=ax)
    prv = pltpu.roll(x, 1, axis=ax)
    return jnp.where((lane & 1) == 0, nxt, prv)


def _mm_kernel(a_ref, b_ref, o_ref, *, scale, sigmoid):
    acc = jnp.dot(a_ref[...], b_ref[...], preferred_element_type=F32)
    if scale is not None:
        acc = acc * scale
    if sigmoid:
        acc = jax.nn.sigmoid(acc)
    o_ref[...] = acc.astype(o_ref.dtype)


def _mm_rope_kernel(a_ref, b_ref, cos_ref, sin_ref, o_ref, *, scale):
    acc = jnp.dot(a_ref[...], b_ref[...], preferred_element_type=F32)
    acc = acc * cos_ref[...] + _swap_pairs(acc) * sin_ref[...]
    o_ref[...] = (acc * scale).astype(o_ref.dtype)


def _mm(a, b, out_dtype, scale=None, sigmoid=False, rope=None, rows_per_batch=None, tm=1024, tn=512):
    m, k = a.shape
    n = b.shape[1]
    tm = _tile(m if rows_per_batch is None else rows_per_batch, tm)
    tn = _tile(n, tn)
    in_specs = [pl.BlockSpec((tm, k), lambda i, j: (i, 0)),
                pl.BlockSpec((k, tn), lambda i, j: (0, j))]
    args = [a, b]
    if rope is None:
        body = functools.partial(_mm_kernel, scale=scale, sigmoid=sigmoid)
    else:
        per = rows_per_batch // tm
        body = functools.partial(_mm_rope_kernel, scale=scale)
        in_specs += [pl.BlockSpec((tm, tn), lambda i, j: (i % per, 0))] * 2
        args += list(rope)
    return pl.pallas_call(
        body,
        grid=(m // tm, n // tn),
        in_specs=in_specs,
        out_specs=pl.BlockSpec((tm, tn), lambda i, j: (i, j)),
        out_shape=jax.ShapeDtypeStruct((m, n), out_dtype),
        compiler_params=_params("parallel", "parallel"),
        name="matmul",
    )(*args)


def _mm_res_kernel(a_ref, b_ref, x_ref, m_ref, o_ref, *, gate_row):
    acc = jnp.dot(a_ref[...], b_ref[...], preferred_element_type=F32)
    o_ref[...] = x_ref[...] + m_ref[0][gate_row:gate_row + 1] * acc


def _mm_residual(a, b, x, mods, rows_per_batch, gate_row, tm=1024, tn=512):
    m, k = a.shape
    n = b.shape[1]
    tm = _tile(rows_per_batch, tm)
    tn = _tile(n, tn)
    per = rows_per_batch // tm
    return pl.pallas_call(
        functools.partial(_mm_res_kernel, gate_row=gate_row),
        grid=(m // tm, n // tn),
        in_specs=[pl.BlockSpec((tm, k), lambda i, j: (i, 0)),
                  pl.BlockSpec((k, tn), lambda i, j: (0, j)),
                  pl.BlockSpec((tm, tn), lambda i, j: (i, j)),
                  pl.BlockSpec((1, 8, tn), lambda i, j: (i // per, 0, j))],
        out_specs=pl.BlockSpec((tm, tn), lambda i, j: (i, j)),
        out_shape=jax.ShapeDtypeStruct((m, n), F32),
        compiler_params=_params("parallel", "parallel"),
        name="matmul_residual",
    )(a, b, x, mods)


def _merge_kernel(oa_ref, ob_ref, wa_ref, wb_ref, ga_ref, gb_ref, o_ref):
    ya = jnp.dot(oa_ref[...], wa_ref[...], preferred_element_type=F32)
    yb = jnp.dot(ob_ref[...], wb_ref[...], preferred_element_type=F32)
    o_ref[...] = (ga_ref[...] * ya + gb_ref[...] * yb).astype(o_ref.dtype)


def _merge(o_mla, o_gqa, w_mla_o, w_gqa_o, gates, tm=1024, tn=512):
    m, k = o_mla.shape
    n = w_mla_o.shape[1]
    tm = _tile(m, tm)
    tn = _tile(n, tn)
    ca, cb = 0, n // tn
    return pl.pallas_call(
        _merge_kernel,
        grid=(m // tm, n // tn),
        in_specs=[pl.BlockSpec((tm, k), lambda i, j: (i, 0)),
                  pl.BlockSpec((tm, k), lambda i, j: (i, 0)),
                  pl.BlockSpec((k, tn), lambda i, j: (0, j)),
                  pl.BlockSpec((k, tn), lambda i, j: (0, j)),
                  pl.BlockSpec((tm, tn), lambda i, j: (i, ca + j)),
                  pl.BlockSpec((tm, tn), lambda i, j: (i, cb + j))],
        out_specs=pl.BlockSpec((tm, tn), lambda i, j: (i, j)),
        out_shape=jax.ShapeDtypeStruct((m, n), BF16),
        compiler_params=_params("parallel", "parallel"),
        name="merge",
    )(o_mla, o_gqa, w_mla_o, w_gqa_o, gates, gates)


def _rms(x, g):
    return x * lax.rsqrt(jnp.mean(x * x, axis=-1, keepdims=True) + EPS) * g


def _prep_kernel(*refs, q_rank, kv_rank, use_rope):
    if use_rope:
        (p_ref, kr_ref, gqa_ref, gkv_ref, ggq_ref, ggk_ref, cg_ref, sg_ref, cm_ref, sm_ref,
         qa_o, ckv_o, ckvx_o, gq_o, gk32_o, gk_o, gv_o) = refs
    else:
        (p_ref, kr_ref, gqa_ref, gkv_ref, ggq_ref, ggk_ref,
         qa_o, ckv_o, ckvx_o, gq_o, gk32_o, gk_o, gv_o) = refs
    c0 = q_rank
    c1 = c0 + kv_rank
    c2 = c1 + GQA_HEADS * HEAD_DIM
    c3 = c2 + GQA_KV_HEADS * HEAD_DIM
    qa_o[...] = _rms(p_ref[:, 0:c0], gqa_ref[...]).astype(BF16)
    ckv = _rms(p_ref[:, c0:c1], gkv_ref[...])
    ckv_o[...] = ckv
    ckvx_o[:, 0:kv_rank] = ckv.astype(BF16)
    kr = kr_ref[...]
    if use_rope:
        kr = kr * cm_ref[...] + _swap_pairs(kr) * sm_ref[...]
    ckvx_o[:, kv_rank:kv_rank + LANE] = kr.astype(BF16)
    q_scale = HEAD_DIM ** -0.5 * LOG2E
    for h in range(GQA_HEADS):
        x = _rms(p_ref[:, c1 + h * HEAD_DIM:c1 + (h + 1) * HEAD_DIM], ggq_ref[...])
        if use_rope:
            x = x * cg_ref[...] + _swap_pairs(x) * sg_ref[...]
        gq_o[:, h * HEAD_DIM:(h + 1) * HEAD_DIM] = (x * q_scale).astype(BF16)
    for h in range(GQA_KV_HEADS):
        x = _rms(p_ref[:, c2 + h * HEAD_DIM:c2 + (h + 1) * HEAD_DIM], ggk_ref[...])
        if use_rope:
            x = x * cg_ref[...] + _swap_pairs(x) * sg_ref[...]
        gk32_o[:, h * HEAD_DIM:(h + 1) * HEAD_DIM] = x
        gk_o[:, h * HEAD_DIM:(h + 1) * HEAD_DIM] = x.astype(BF16)
    gv_o[...] = p_ref[:, c3:c3 + GQA_KV_HEADS * HEAD_DIM].astype(BF16)


def _prep(proj, kr, g_q_a, g_kv_a, g_gqa_q, g_gqa_k, rope, rows_per_batch):
    m = proj.shape[0]
    q_rank, kv_rank = g_q_a.shape[0], g_kv_a.shape[0]
    width = q_rank + kv_rank + (GQA_HEADS + 2 * GQA_KV_HEADS) * HEAD_DIM
    tm = _tile(rows_per_batch, 256)
    per = rows_per_batch // tm
    nq, nk = GQA_HEADS * HEAD_DIM, GQA_KV_HEADS * HEAD_DIM
    row = lambda i: (i, 0)
    const = lambda i: (0, 0)
    in_specs = [pl.BlockSpec((tm, width), row), pl.BlockSpec((tm, LANE), row),
                pl.BlockSpec((1, q_rank), const), pl.BlockSpec((1, kv_rank), const),
                pl.BlockSpec((1, HEAD_DIM), const), pl.BlockSpec((1, HEAD_DIM), const)]
    args = [proj, kr, g_q_a.reshape(1, -1), g_kv_a.reshape(1, -1), g_gqa_q.reshape(1, -1), g_gqa_k.reshape(1, -1)]
    if rope is not None:
        in_specs += [pl.BlockSpec((tm, LANE), lambda i: (i % per, 0))] * 4
        args += list(rope)
    outs = [((m, q_rank), BF16), ((m, kv_rank), F32), ((m, kv_rank + LANE), BF16), ((m, nq), BF16),
            ((m, nk), F32), ((m, nk), BF16), ((m, nk), BF16)]
    return pl.pallas_call(
        functools.partial(_prep_kernel, q_rank=q_rank, kv_rank=kv_rank, use_rope=rope is not None),
        grid=(m // tm,),
        in_specs=in_specs,
        out_specs=[pl.BlockSpec((tm, s[1]), row) for s, _ in outs],
        out_shape=[jax.ShapeDtypeStruct(s, dt) for s, dt in outs],
        compiler_params=_params("parallel"),
        name="prep",
    )(*args)


def _attn_kernel(q_ref, k_ref, v_ref, o_ref, *, rep, dq, dv, sub):
    k = k_ref[...]
    v = v_ref[...]
    for r in range(rep):
        for i in range(q_ref.shape[0] // sub):
            rows = slice(i * sub, (i + 1) * sub)
            q = q_ref[rows, r * dq:(r + 1) * dq]
            s = lax.dot_general(q, k, (((1,), (1,)), ((), ())), preferred_element_type=F32)
            p = jnp.exp2(s - jnp.max(s, axis=-1, keepdims=True))
            l = jnp.sum(p, axis=-1, keepdims=True)
            o = jnp.dot(p.astype(BF16), v, preferred_element_type=F32)
            o_ref[rows, r * dv:(r + 1) * dv] = (o / l).astype(o_ref.dtype)


def _attention(q, k, v, n_groups, rep, dq, dv, k_col0, v_col0, tq):
    b, t, _ = q.shape
    s_len = k.shape[1]
    tq = _tile(t, tq)
    sub = _tile(tq, 256)
    kb, vb = k_col0 // dq, v_col0 // dv
    assert kb * dq == k_col0 and vb * dv == v_col0
    return pl.pallas_call(
        functools.partial(_attn_kernel, rep=rep, dq=dq, dv=dv, sub=sub),
        grid=(b, n_groups, t // tq),
        in_specs=[pl.BlockSpec((None, tq, rep * dq), lambda bi, g, qi: (bi, qi, g)),
                  pl.BlockSpec((None, s_len, dq), lambda bi, g, qi: (bi, 0, kb + g)),
                  pl.BlockSpec((None, s_len, dv), lambda bi, g, qi: (bi, 0, vb + g))],
        out_specs=pl.BlockSpec((None, tq, rep * dv), lambda bi, g, qi: (bi, qi, g)),
        out_shape=jax.ShapeDtypeStruct((b, t, n_groups * rep * dv), BF16),
        compiler_params=_params("parallel", "parallel", "parallel"),
        name="attention",
    )(q, k, v)


def _top16(s, idx):
    n, c = s.shape
    row = lax.broadcasted_iota(jnp.int32, (PEER_TOPK, c), 0)
    big = float(PEER_TOPK * PEER_TOPK)

    def body(r, carry):
        s, vals, idxs = carry
        mx = jnp.max(s, axis=0, keepdims=True)
        am = jnp.min(jnp.where(s == mx, idx, big), axis=0, keepdims=True)
        vals = jnp.where(row == r, mx, vals)
        idxs = jnp.where(row == r, am, idxs)
        s = jnp.where(idx == am, -jnp.inf, s)
        return s, vals, idxs

    init = (s, jnp.zeros((PEER_TOPK, c), F32), jnp.zeros((PEER_TOPK, c), F32))
    _, vals, idxs = lax.fori_loop(0, PEER_TOPK, body, init)
    return vals, idxs


def _pick(table, pos):
    out = jnp.zeros(pos.shape, F32)
    for a in range(PEER_TOPK):
        out = jnp.where(pos == a, table[a:a + 1], out)
    return out


_CAND_ROWS = [(0, 0), (0, 8), (1, 0), (2, 0), (3, 0), (4, 0), (5, 0), (6, 0), (7, 0)]


def _route_kernel(q_ref, k_ref, oi_ref, oj_ref, og_ref):
    q = q_ref[...]
    tn = q.shape[0]
    half = q.shape[1] // 2
    nt = (((1,), (1,)), ((), ()))
    s0 = lax.dot_general(k_ref[0], q[:, :half], nt, preferred_element_type=F32)
    s1 = lax.dot_general(k_ref[1], q[:, half:], nt, preferred_element_type=F32)
    s01 = jnp.concatenate([s0, s1], axis=1)
    v01, i01 = _top16(s01, lax.broadcasted_iota(jnp.int32, s01.shape, 0).astype(F32))
    v0, v1, i0, i1 = v01[:, :tn], v01[:, tn:], i01[:, :tn], i01[:, tn:]
    sub = lax.broadcasted_iota(jnp.int32, (SUBLANE, tn), 0).astype(F32)
    cand = [v0[a:a + 1] + v1[b:b + SUBLANE] for a, b in _CAND_ROWS]
    flat = [sub + float(a * PEER_TOPK + b) for a, b in _CAND_ROWS]
    cand.append(v0[SUBLANE:] + v1[0:1])
    flat.append((sub + float(SUBLANE)) * float(PEER_TOPK))
    best, pos = _top16(jnp.concatenate(cand, axis=0), jnp.concatenate(flat, axis=0))
    pos = pos.astype(jnp.int32)
    e = jnp.exp(best - best[0:1])
    oi_ref[...] = _pick(i0, pos >> 4)
    oj_ref[...] = _pick(i1, pos & (PEER_TOPK - 1))
    og_ref[...] = e / jnp.sum(e, axis=0, keepdims=True)


def _route(qp, sub_keys, tn=256):
    m = qp.shape[0]
    tn = _tile(m, tn)
    qd = qp.shape[1] // PEER_HEADS
    shp = jax.ShapeDtypeStruct((PEER_HEADS * PEER_TOPK, m), F32)
    spec = pl.BlockSpec((PEER_TOPK, tn), lambda t, h: (h, t))
    return pl.pallas_call(
        _route_kernel,
        grid=(m // tn, PEER_HEADS),
        in_specs=[pl.BlockSpec((tn, qd), lambda t, h: (t, h)),
                  pl.BlockSpec(sub_keys.shape, lambda t, h: (0, 0, 0))],
        out_specs=[spec, spec, spec],
        out_shape=[shp, shp, shp],
        compiler_params=_params("parallel", "parallel"),
        name="peer_route",
    )(qp, sub_keys)


def _wbuild_kernel(i_ref, j_ref, g_ref, o_ref, it_s, jt_s, gt_s):
    it_s[...] = i_ref[...].T
    jt_s[...] = j_ref[...].T
    gt_s[...] = g_ref[...].T
    ne = i_ref.shape[0]
    iota_a = lax.broadcasted_iota(jnp.int32, (2 * N_KEYS, 2 * ne), 0).astype(F32).astype(BF16)
    iota_g = lax.broadcasted_iota(jnp.int32, (N_KEYS, 2 * ne), 0).astype(F32).astype(BF16)
    one, zero = jnp.ones((), BF16), jnp.zeros((), BF16)

    def body(m, carry):
        ri = it_s[pl.ds(2 * m, 2), :]
        rj = jt_s[pl.ds(2 * m, 2), :]
        rg = gt_s[pl.ds(2 * m, 2), :]
        row_i = jnp.concatenate([ri[0:1], ri[1:2] + float(N_KEYS)], axis=1).astype(BF16)
        row_j = jnp.concatenate([rj[0:1], rj[1:2]], axis=1).astype(BF16)
        row_g = jnp.concatenate([rg[0:1], rg[1:2]], axis=1).astype(BF16)
        a2 = jnp.where(iota_a == row_i, one, zero)
        g2 = jnp.where(iota_g == row_j, row_g, zero)
        w = lax.dot_general(a2, g2, (((1,), (1,)), ((), ())), preferred_element_type=F32)
        o_ref[pl.ds(2 * m, 2)] = w.reshape(2, N_KEYS, N_KEYS)
        return carry

    lax.fori_loop(0, o_ref.shape[0] // 2, body, 0, unroll=4)


def _wbuild(idx_i, idx_j, gates):
    ne, m = idx_i.shape
    tw = LANE
    spec = pl.BlockSpec((ne, tw), lambda t: (0, t))
    return pl.pallas_call(
        _wbuild_kernel,
        grid=(m // tw,),
        in_specs=[spec, spec, spec],
        out_specs=pl.BlockSpec((tw, N_KEYS, N_KEYS), lambda t: (t, 0, 0)),
        out_shape=jax.ShapeDtypeStruct((m, N_KEYS, N_KEYS), F32),
        scratch_shapes=[pltpu.VMEM((tw, ne), F32)] * 3,
        compiler_params=_params("parallel"),
        name="peer_gate_map",
    )(idx_i, idx_j, gates)


def _peer_kernel(x_ref, ut_ref, vp_ref, vc_ref, w_ref, o_ref, ga_s, gb_s, *, rows, n_pairs):
    k = pl.program_id(1)
    ec = rows * N_KEYS
    base = (k % (SUBLANE // (2 * rows))) * (2 * rows)

    def gates(half, dst):
        s = jnp.dot(x_ref[...], ut_ref[:, half * ec:(half + 1) * ec], preferred_element_type=F32)
        for il in range(rows):
            wg = w_ref[:, base + half * rows + il, :]
            sl = s[:, il * N_KEYS:(il + 1) * N_KEYS]
            g = 0.5 * sl * (1.0 + lax.erf(sl * INV_SQRT2)) * wg
            dst[:, il * N_KEYS:(il + 1) * N_KEYS] = g.astype(BF16)

    def mix(src, v_ref):
        return jnp.dot(src[...], v_ref[...], preferred_element_type=F32)

    @pl.when(k == 0)
    def _():
        gates(0, ga_s)
        gates(1, gb_s)
        o_ref[...] = mix(ga_s, vc_ref)

    @pl.when(jnp.logical_and(k > 0, k < n_pairs))
    def _():
        gates(0, ga_s)
        o_ref[...] += mix(gb_s, vp_ref)
        gates(1, gb_s)
        o_ref[...] += mix(ga_s, vc_ref)

    @pl.when(k == n_pairs)
    def _():
        o_ref[...] += mix(gb_s, vp_ref)


def _peer_mix(h2, u_t, v, wmap, tn=512, rows=2):
    m, d = h2.shape
    n_exp = v.shape[0]
    tn = _tile(m, tn)
    ec = rows * N_KEYS
    n_pairs = n_exp // (2 * ec)
    per = SUBLANE // (2 * rows)
    last = n_pairs - 1
    return pl.pallas_call(
        functools.partial(_peer_kernel, rows=rows, n_pairs=n_pairs),
        grid=(m // tn, n_pairs + 1),
        in_specs=[pl.BlockSpec((tn, d), lambda t, k: (t, 0)),
                  pl.BlockSpec((d, 2 * ec), lambda t, k: (0, jnp.minimum(k, last))),
                  pl.BlockSpec((ec, d), lambda t, k: (jnp.maximum(2 * k - 1, 0), 0)),
                  pl.BlockSpec((ec, d), lambda t, k: (jnp.minimum(2 * k, 2 * last + 1), 0)),
                  pl.BlockSpec((tn, SUBLANE, N_KEYS), lambda t, k: (t, jnp.minimum(k, last) // per, 0))],
        out_specs=pl.BlockSpec((tn, d), lambda t, k: (t, 0)),
        out_shape=jax.ShapeDtypeStruct((m, d), F32),
        scratch_shapes=[pltpu.VMEM((tn, ec), BF16), pltpu.VMEM((tn, ec), BF16)],
        compiler_params=_params("parallel", "arbitrary"),
        name="peer_mix",
    )(h2, u_t, v, v, wmap)


def _final_kernel(x_ref, p_ref, m_ref, g_ref, o_ref, *, gate_row):
    x = x_ref[...] + m_ref[0][gate_row:gate_row + 1] * p_ref[...]
    o_ref[...] = _rms(x, g_ref[...])


def _final(x, po, mods, g_final, rows_per_batch, gate_row):
    m, d = x.shape
    tm = _tile(rows_per_batch, 256)
    per = rows_per_batch // tm
    row = lambda i: (i, 0)
    return pl.pallas_call(
        functools.partial(_final_kernel, gate_row=gate_row),
        grid=(m // tm,),
        in_specs=[pl.BlockSpec((tm, d), row), pl.BlockSpec((tm, d), row),
                  pl.BlockSpec((1, 8, d), lambda i: (i // per, 0, 0)),
                  pl.BlockSpec((1, d), lambda i: (0, 0))],
        out_specs=pl.BlockSpec((tm, d), row),
        out_shape=jax.ShapeDtypeStruct((m, d), F32),
        compiler_params=_params("parallel"),
        name="final_norm",
    )(x, po, mods, g_final.reshape(1, d))


def _rope_tables(n_tokens, dim):
    n_rows = n_tokens // GRID_W
    rows = jnp.broadcast_to(jnp.arange(n_rows)[:, None], (n_rows, GRID_W)).reshape(-1).astype(F32)
    cols = jnp.broadcast_to(jnp.arange(GRID_W)[None, :], (n_rows, GRID_W)).reshape(-1).astype(F32)
    n_freq = dim // 4
    freqs = ROPE_BASE ** (-jnp.arange(n_freq, dtype=F32) / n_freq)
    ang = jnp.concatenate([rows[:, None] * freqs, cols[:, None] * freqs], axis=-1)
    cos, sin = jnp.cos(ang), jnp.sin(ang)
    cos2 = jnp.repeat(cos, 2, axis=-1)
    sin2 = jnp.stack([-sin, sin], axis=-1).reshape(n_tokens, dim)
    return cos2, sin2


def _layer_weights(w_in, w_q_b, w_kv_b, w_mla_o, w_gqa_o, w_out, w_peer_q, sub_keys, peer_u, peer_v):
    q_rank, kv_rank = w_q_b.shape[0], w_kv_b.shape[0]
    k0 = q_rank + kv_rank
    lw = {}
    g0 = k0 + MLA_ROPE + (GQA_HEADS + 2 * GQA_KV_HEADS) * HEAD_DIM
    lw['w_in'] = jnp.concatenate([w_in[:, :k0], w_in[:, k0 + MLA_ROPE:g0]], axis=1).astype(BF16)
    lw['w_gates'] = w_in[:, g0:].astype(BF16)
    lw['w_kr'] = jnp.pad(w_in[:, k0:k0 + MLA_ROPE], ((0, 0), (0, LANE - MLA_ROPE))).astype(BF16)
    wq = w_q_b.reshape(q_rank, MLA_HEADS, MLA_NOPE + MLA_ROPE)
    wq = jnp.pad(wq, ((0, 0), (0, 0), (0, MLA_QPAD - MLA_NOPE - MLA_ROPE)))
    lw['w_q_b'] = wq.reshape(q_rank, MLA_HEADS * MLA_QPAD).astype(BF16)
    wkv = w_kv_b.reshape(kv_rank, MLA_HEADS, MLA_NOPE + MLA_V)
    wk = jnp.zeros((kv_rank + LANE, MLA_HEADS, MLA_QPAD), F32)
    wk = wk.at[:kv_rank, :, :MLA_NOPE].set(wkv[:, :, :MLA_NOPE])
    eye = jnp.broadcast_to(jnp.eye(MLA_ROPE, dtype=F32)[:, None, :], (MLA_ROPE, MLA_HEADS, MLA_ROPE))
    wk = wk.at[kv_rank:kv_rank + MLA_ROPE, :, MLA_NOPE:MLA_NOPE + MLA_ROPE].set(eye)
    wv = jnp.pad(wkv[:, :, MLA_NOPE:], ((0, LANE), (0, 0), (0, 0)))
    lw['w_kv'] = jnp.concatenate([wk.reshape(kv_rank + LANE, -1), wv.reshape(kv_rank + LANE, -1)],
                                 axis=1).astype(BF16)
    lw['w_mla_o'] = w_mla_o.astype(BF16)
    lw['w_gqa_o'] = w_gqa_o.astype(BF16)
    lw['w_out'] = w_out.astype(BF16)
    lw['w_peer_q'] = w_peer_q.astype(BF16)
    lw['sub_keys'] = sub_keys.astype(BF16)
    lw['peer_ut'] = peer_u.T.astype(BF16)
    lw['peer_v'] = peer_v.astype(BF16)
    return lw


def _layer(x, mods, lw, gains, rope, ctx):
    b, t, d = x.shape
    m = b * t
    x2 = x.reshape(m, d)
    g_norm1, g_norm2, g_q_a, g_kv_a, g_gqa_q, g_gqa_k = gains
    q_rank, kv_rank = g_q_a.shape[0], g_kv_a.shape[0]
    nq, nk = GQA_HEADS * HEAD_DIM, GQA_KV_HEADS * HEAD_DIM
    col_gate_mla = q_rank + kv_rank + nq + 2 * nk

    rpb = t if mods.shape[0] == b else m
    h = _norm_mod(x2, g_norm1, mods, rpb, SHIFT1, SCALE1)
    proj = _mm(h, lw['w_in'], F32)
    gates = _mm(h, lw['w_gates'], BF16, sigmoid=True)
    kr = _mm(h, lw['w_kr'], F32)
    prep_rope = None if rope is None else rope[:4]
    qa, ckv32, ckvx, gq, gk32, gk, gv = _prep(proj, kr, g_q_a, g_kv_a, g_gqa_q, g_gqa_k, prep_rope, t)

    mla_scale = (MLA_NOPE + MLA_ROPE) ** -0.5 * LOG2E
    if rope is None:
        q_mla = _mm(qa, lw['w_q_b'], BF16, scale=mla_scale)
    else:
        q_mla = _mm(qa, lw['w_q_b'], BF16, scale=mla_scale, rope=rope[4:], rows_per_batch=t)

    ckvx = ckvx.reshape(b, t, -1)
    gk_all = gk.reshape(b, t, nk)
    gv_all = gv.reshape(b, t, nk)
    if ctx is not None:
        c_ckv, c_kr, c_k, c_v = ctx
        past = c_ckv.shape[1]
        c_x = jnp.concatenate([c_ckv, c_kr, jnp.zeros((b, past, LANE - MLA_ROPE), F32)], axis=-1).astype(BF16)
        ckvx = jnp.concatenate([ckvx, c_x], axis=1)
        gk_all = jnp.concatenate([gk_all, c_k.reshape(b, past, nk).astype(BF16)], axis=1)
        gv_all = jnp.concatenate([gv_all, c_v.reshape(b, past, nk).astype(BF16)], axis=1)
    s_len = ckvx.shape[1]
    kv = _mm(ckvx.reshape(b * s_len, -1), lw['w_kv'], BF16).reshape(b, s_len, -1)

    o_mla = _attention(q_mla.reshape(b, t, -1), kv, kv, MLA_HEADS, 1, MLA_QPAD, MLA_V,
                       0, MLA_HEADS * MLA_QPAD, tq=512)
    o_gqa = _attention(gq.reshape(b, t, -1), gk_all, gv_all, GQA_KV_HEADS, GQA_HEADS // GQA_KV_HEADS,
                       HEAD_DIM, HEAD_DIM, 0, 0, tq=256)
    merged = _merge(o_mla.reshape(m, -1), o_gqa.reshape(m, -1), lw['w_mla_o'], lw['w_gqa_o'],
                    gates)
    x1 = _mm_residual(merged, lw['w_out'], x2, mods, rpb, GATE1)

    h2 = _norm_mod(x1, g_norm2, mods, rpb, SHIFT2, SCALE2)
    qp = _mm(h2, lw['w_peer_q'], BF16)
    idx_i, idx_j, gates = _route(qp, lw['sub_keys'])
    wmap = _wbuild(idx_i, idx_j, gates)
    po = _peer_mix(h2, lw['peer_ut'], lw['peer_v'], wmap)
    own = (ckv32.reshape(b, t, kv_rank), kr[:, :MLA_ROPE].reshape(b, t, MLA_ROPE),
           gk32.reshape(b, t, GQA_KV_HEADS, HEAD_DIM),
           proj[:, col_gate_mla - nk:col_gate_mla].reshape(b, t, GQA_KV_HEADS, HEAD_DIM))
    return x1, po, own


def kernel(x_prompt, x_sample, c, cache_mla_ckv, cache_mla_krope, cache_gqa_k, cache_gqa_v, c_ctx, w_mod, b_mod, g_norm1, g_norm2, w_in, g_q_a, w_q_b, g_kv_a, w_kv_b, g_gqa_q, g_gqa_k, w_mla_o, w_gqa_o, w_out, w_peer_q, peer_sub_keys, peer_u, peer_v, g_final):
    depth = w_in.shape[0]
    d = x_prompt.shape[-1]
    bc, tc = x_prompt.shape[:2]
    bl, tl = x_sample.shape[:2]

    cg, sg = _rope_tables(tl, HEAD_DIM)
    cm, sm = _rope_tables(tl, MLA_ROPE)
    one = jnp.ones((tl, LANE - MLA_ROPE), F32)
    zero = jnp.zeros((tl, LANE - MLA_ROPE), F32)
    cm_k, sm_k = jnp.concatenate([cm, one], 1), jnp.concatenate([sm, zero], 1)
    ones_n, zeros_n = jnp.ones((tl, MLA_NOPE), F32), jnp.zeros((tl, MLA_NOPE), F32)
    cq = jnp.tile(jnp.concatenate([ones_n, cm, one], 1), (1, 2))
    sq = jnp.tile(jnp.concatenate([zeros_n, sm, zero], 1), (1, 2))
    rope = (cg, sg, cm_k, sm_k, cq, sq)

    n_rows = bl + 1
    pad_rows = -n_rows % 8
    cvecs = jnp.concatenate([c, c_ctx[None, :], jnp.zeros((pad_rows, d), F32)], axis=0)

    xc, xl = x_prompt, x_sample
    new = [[], [], [], []]
    for l in range(depth):
        lw = _layer_weights(w_in[l], w_q_b[l], w_kv_b[l], w_mla_o[l], w_gqa_o[l], w_out[l],
                            w_peer_q[l], peer_sub_keys[l], peer_u[l], peer_v[l])
        gains = (g_norm1[l], g_norm2[l], g_q_a[l], g_kv_a[l], g_gqa_q[l], g_gqa_k[l])
        mod = _modulation(cvecs, w_mod[l], b_mod[l]).reshape(n_rows + pad_rows, N_MOD, d)
        mod = jnp.pad(mod, ((0, 0), (0, 8 - N_MOD), (0, 0)))
        mods_lat, mods_ctx = mod[:bl], mod[bl:bl + 1]
        last = l == depth - 1

        x1, po, own = _layer(xc, mods_ctx, lw, gains, None, None)
        for acc, o in zip(new, own):
            acc.append(o)
        xc = _finish(x1, po, mods_ctx, g_final, bc * tc, last).reshape(bc, tc, d)

        ctx = (cache_mla_ckv[:, l], cache_mla_krope[:, l], cache_gqa_k[:, l], cache_gqa_v[:, l])
        x1, po, _ = _layer(xl, mods_lat, lw, gains, rope, ctx)
        xl = _finish(x1, po, mods_lat, g_final, tl, last).reshape(bl, tl, d)

    return (xc, xl, jnp.stack(new[0], axis=1), jnp.stack(new[1], axis=1),
            jnp.stack(new[2], axis=1), jnp.stack(new[3], axis=1))


def _finish(x1, po, mods, g_final, rows_per_batch, last):
    d = x1.shape[1]
    g = g_final if last else None
    if g is None:
        raise NotImplementedError("only the final layer's residual is fused with the output norm")
    return _final(x1, po, mods, g, rows_per_batch, GATE2)
```

```python
import functools
import math

import jax
import jax.numpy as jnp
from jax import lax
from jax.experimental import pallas as pl
from jax.experimental.pallas import tpu as pltpu

F32 = jnp.float32
BF16 = jnp.bfloat16

GRID_W = 64
EPS = 1e-6
ROPE_BASE = 10000.0
N_MOD = 6
MLA_HEADS = 16
MLA_NOPE = 128
MLA_ROPE = 64
MLA_V = 128
MLA_QPAD = 256
GQA_HEADS = 16
GQA_KV_HEADS = 4
HEAD_DIM = 128
PEER_HEADS = 8
N_KEYS = 128
PEER_TOPK = 16
PEER_CHUNK_ROWS = 2
LANE = 128
SUBLANE = 8
VMEM_LIMIT = 56 * 1024 * 1024
INV_SQRT2 = 1.0 / math.sqrt(2.0)
LOG2E = 1.0 / math.log(2.0)

SHIFT1, SCALE1, GATE1, SHIFT2, SCALE2, GATE2 = range(6)


def _tile(dim, pref):
    t = min(dim, pref)
    assert dim % t == 0, (dim, pref)
    return t


def _params(*sem):
    return pltpu.CompilerParams(dimension_semantics=sem, vmem_limit_bytes=VMEM_LIMIT)


def _mod_kernel(c_ref, w_ref, b_ref, o_ref):
    c = c_ref[...]
    a = (c * jax.nn.sigmoid(c)).astype(BF16)
    o_ref[...] = jnp.dot(a, w_ref[...].astype(BF16), preferred_element_type=F32) + b_ref[...]


def _modulation(cvecs, w_mod, b_mod):
    r, d = cvecs.shape
    n = w_mod.shape[1]
    tn = _tile(n, 512)
    return pl.pallas_call(
        _mod_kernel,
        grid=(n // tn,),
        in_specs=[pl.BlockSpec((r, d), lambda j: (0, 0)),
                  pl.BlockSpec((d, tn), lambda j: (0, j)),
                  pl.BlockSpec((1, tn), lambda j: (0, j))],
        out_specs=pl.BlockSpec((r, tn), lambda j: (0, j)),
        out_shape=jax.ShapeDtypeStruct((r, n), F32),
        compiler_params=_params("parallel"),
        name="modulation",
    )(cvecs, w_mod, b_mod.reshape(1, n))


def _norm_mod_kernel(x_ref, g_ref, m_ref, o_ref, *, shift_row, scale_row):
    x = x_ref[...]
    y = x * lax.rsqrt(jnp.mean(x * x, axis=-1, keepdims=True) + EPS) * g_ref[...]
    m = m_ref[0]
    o_ref[...] = (y * (1.0 + m[scale_row:scale_row + 1]) + m[shift_row:shift_row + 1]).astype(BF16)


def _norm_mod(x, g, mods, rows_per_batch, shift_row, scale_row):
    m, d = x.shape
    tm = _tile(rows_per_batch, 256)
    per = rows_per_batch // tm
    return pl.pallas_call(
        functools.partial(_norm_mod_kernel, shift_row=shift_row, scale_row=scale_row),
        grid=(m // tm,),
        in_specs=[pl.BlockSpec((tm, d), lambda i: (i, 0)),
                  pl.BlockSpec((1, d), lambda i: (0, 0)),
                  pl.BlockSpec((1, 8, d), lambda i: (i // per, 0, 0))],
        out_specs=pl.BlockSpec((tm, d), lambda i: (i, 0)),
        out_shape=jax.ShapeDtypeStruct((m, d), BF16),
        compiler_params=_params("parallel"),
        name="norm_mod",
    )(x, g.reshape(1, d), mods)


def _swap_pairs(x):
    ax = x.ndim - 1
    n = x.shape[ax]
    lane = lax.broadcasted_iota(jnp.int32, x.shape, ax)
    nxt = pltpu.roll(x, n - 1, axis=ax)
    prv = pltpu.roll(x, 1, axis=ax)
    return jnp.where((lane & 1) == 0, nxt, prv)


def _mm_kernel(a_ref, b_ref, o_ref, *, scale, sigmoid):
    acc = jnp.dot(a_ref[...], b_ref[...], preferred_element_type=F32)
    if scale is not None:
        acc = acc * scale
    if sigmoid:
        acc = jax.nn.sigmoid(acc)
    o_ref[...] = acc.astype(o_ref.dtype)


def _mm_rope_kernel(a_ref, b_ref, cos_ref, sin_ref, o_ref, *, scale):
    acc = jnp.dot(a_ref[...], b_ref[...], preferred_element_type=F32)
    acc = acc * cos_ref[...] + _swap_pairs(acc) * sin_ref[...]
    o_ref[...] = (acc * scale).astype(o_ref.dtype)


def _mm(a, b, out_dtype, scale=None, sigmoid=False, rope=None, rows_per_batch=None, tm=1024, tn=512):
    m, k = a.shape
    n = b.shape[1]
    tm = _tile(m if rows_per_batch is None else rows_per_batch, tm)
    tn = _tile(n, tn)
    in_specs = [pl.BlockSpec((tm, k), lambda i, j: (i, 0)),
                pl.BlockSpec((k, tn), lambda i, j: (0, j))]
    args = [a, b]
    if rope is None:
        body = functools.partial(_mm_kernel, scale=scale, sigmoid=sigmoid)
    else:
        per = rows_per_batch // tm
        body = functools.partial(_mm_rope_kernel, scale=scale)
        in_specs += [pl.BlockSpec((tm, tn), lambda i, j: (i % per, 0))] * 2
        args += list(rope)
    return pl.pallas_call(
        body,
        grid=(m // tm, n // tn),
        in_specs=in_specs,
        out_specs=pl.BlockSpec((tm, tn), lambda i, j: (i, j)),
        out_shape=jax.ShapeDtypeStruct((m, n), out_dtype),
        compiler_params=_params("parallel", "parallel"),
        name="matmul",
    )(*args)


def _mm_res_kernel(a_ref, b_ref, x_ref, m_ref, o_ref, *, gate_row):
    acc = jnp.dot(a_ref[...], b_ref[...], preferred_element_type=F32)
    o_ref[...] = x_ref[...] + m_ref[0][gate_row:gate_row + 1] * acc


def _mm_residual(a, b, x, mods, rows_per_batch, gate_row, tm=1024, tn=512):
    m, k = a.shape
    n = b.shape[1]
    tm = _tile(rows_per_batch, tm)
    tn = _tile(n, tn)
    per = rows_per_batch // tm
    return pl.pallas_call(
        functools.partial(_mm_res_kernel, gate_row=gate_row),
        grid=(m // tm, n // tn),
        in_specs=[pl.BlockSpec((tm, k), lambda i, j: (i, 0)),
                  pl.BlockSpec((k, tn), lambda i, j: (0, j)),
                  pl.BlockSpec((tm, tn), lambda i, j: (i, j)),
                  pl.BlockSpec((1, 8, tn), lambda i, j: (i // per, 0, j))],
        out_specs=pl.BlockSpec((tm, tn), lambda i, j: (i, j)),
        out_shape=jax.ShapeDtypeStruct((m, n), F32),
        compiler_params=_params("parallel", "parallel"),
        name="matmul_residual",
    )(a, b, x, mods)


def _merge_kernel(oa_ref, ob_ref, wa_ref, wb_ref, ga_ref, gb_ref, o_ref):
    ya = jnp.dot(oa_ref[...], wa_ref[...], preferred_element_type=F32)
    yb = jnp.dot(ob_ref[...], wb_ref[...], preferred_element_type=F32)
    o_ref[...] = (ga_ref[...] * ya + gb_ref[...] * yb).astype(o_ref.dtype)


def _merge(o_mla, o_gqa, w_mla_o, w_gqa_o, gates, tm=1024, tn=512):
    m, k = o_mla.shape
    n = w_mla_o.shape[1]
    tm = _tile(m, tm)
    tn = _tile(n, tn)
    ca, cb = 0, n // tn
    return pl.pallas_call(
        _merge_kernel,
        grid=(m // tm, n // tn),
        in_specs=[pl.BlockSpec((tm, k), lambda i, j: (i, 0)),
                  pl.BlockSpec((tm, k), lambda i, j: (i, 0)),
                  pl.BlockSpec((k, tn), lambda i, j: (0, j)),
                  pl.BlockSpec((k, tn), lambda i, j: (0, j)),
                  pl.BlockSpec((tm, tn), lambda i, j: (i, ca + j)),
                  pl.BlockSpec((tm, tn), lambda i, j: (i, cb + j))],
        out_specs=pl.BlockSpec((tm, tn), lambda i, j: (i, j)),
        out_shape=jax.ShapeDtypeStruct((m, n), BF16),
        compiler_params=_params("parallel", "parallel"),
        name="merge",
    )(o_mla, o_gqa, w_mla_o, w_gqa_o, gates, gates)


def _rms(x, g):
    return x * lax.rsqrt(jnp.mean(x * x, axis=-1, keepdims=True) + EPS) * g


def _prep_kernel(*refs, q_rank, kv_rank, use_rope):
    if use_rope:
        (p_ref, kr_ref, gqa_ref, gkv_ref, ggq_ref, ggk_ref, cg_ref, sg_ref, cm_ref, sm_ref,
         qa_o, ckv_o, ckvx_o, gq_o, gk32_o, gk_o, gv_o) = refs
    else:
        (p_ref, kr_ref, gqa_ref, gkv_ref, ggq_ref, ggk_ref,
         qa_o, ckv_o, ckvx_o, gq_o, gk32_o, gk_o, gv_o) = refs
    c0 = q_rank
    c1 = c0 + kv_rank
    c2 = c1 + GQA_HEADS * HEAD_DIM
    c3 = c2 + GQA_KV_HEADS * HEAD_DIM
    qa_o[...] = _rms(p_ref[:, 0:c0], gqa_ref[...]).astype(BF16)
    ckv = _rms(p_ref[:, c0:c1], gkv_ref[...])
    ckv_o[...] = ckv
    ckvx_o[:, 0:kv_rank] = ckv.astype(BF16)
    kr = kr_ref[...]
    if use_rope:
        kr = kr * cm_ref[...] + _swap_pairs(kr) * sm_ref[...]
    ckvx_o[:, kv_rank:kv_rank + LANE] = kr.astype(BF16)
    q_scale = HEAD_DIM ** -0.5 * LOG2E
    for h in range(GQA_HEADS):
        x = _rms(p_ref[:, c1 + h * HEAD_DIM:c1 + (h + 1) * HEAD_DIM], ggq_ref[...])
        if use_rope:
            x = x * cg_ref[...] + _swap_pairs(x) * sg_ref[...]
        gq_o[:, h * HEAD_DIM:(h + 1) * HEAD_DIM] = (x * q_scale).astype(BF16)
    for h in range(GQA_KV_HEADS):
        x = _rms(p_ref[:, c2 + h * HEAD_DIM:c2 + (h + 1) * HEAD_DIM], ggk_ref[...])
        if use_rope:
            x = x * cg_ref[...] + _swap_pairs(x) * sg_ref[...]
        gk32_o[:, h * HEAD_DIM:(h + 1) * HEAD_DIM] = x
        gk_o[:, h * HEAD_DIM:(h + 1) * HEAD_DIM] = x.astype(BF16)
    gv_o[...] = p_ref[:, c3:c3 + GQA_KV_HEADS * HEAD_DIM].astype(BF16)


def _prep(proj, kr, g_q_a, g_kv_a, g_gqa_q, g_gqa_k, rope, rows_per_batch):
    m = proj.shape[0]
    q_rank, kv_rank = g_q_a.shape[0], g_kv_a.shape[0]
    width = q_rank + kv_rank + (GQA_HEADS + 2 * GQA_KV_HEADS) * HEAD_DIM
    tm = _tile(rows_per_batch, 256)
    per = rows_per_batch // tm
    nq, nk = GQA_HEADS * HEAD_DIM, GQA_KV_HEADS * HEAD_DIM
    row = lambda i: (i, 0)
    const = lambda i: (0, 0)
    in_specs = [pl.BlockSpec((tm, width), row), pl.BlockSpec((tm, LANE), row),
                pl.BlockSpec((1, q_rank), const), pl.BlockSpec((1, kv_rank), const),
                pl.BlockSpec((1, HEAD_DIM), const), pl.BlockSpec((1, HEAD_DIM), const)]
    args = [proj, kr, g_q_a.reshape(1, -1), g_kv_a.reshape(1, -1), g_gqa_q.reshape(1, -1), g_gqa_k.reshape(1, -1)]
    if rope is not None:
        in_specs += [pl.BlockSpec((tm, LANE), lambda i: (i % per, 0))] * 4
        args += list(rope)
    outs = [((m, q_rank), BF16), ((m, kv_rank), F32), ((m, kv_rank + LANE), BF16), ((m, nq), BF16),
            ((m, nk), F32), ((m, nk), BF16), ((m, nk), BF16)]
    return pl.pallas_call(
        functools.partial(_prep_kernel, q_rank=q_rank, kv_rank=kv_rank, use_rope=rope is not None),
        grid=(m // tm,),
        in_specs=in_specs,
        out_specs=[pl.BlockSpec((tm, s[1]), row) for s, _ in outs],
        out_shape=[jax.ShapeDtypeStruct(s, dt) for s, dt in outs],
        compiler_params=_params("parallel"),
        name="prep",
    )(*args)


def _attn_kernel(q_ref, k_ref, v_ref, o_ref, *, rep, dq, dv, sub):
    k = k_ref[...]
    v = v_ref[...]
    for r in range(rep):
        for i in range(q_ref.shape[0] // sub):
            rows = slice(i * sub, (i + 1) * sub)
            q = q_ref[rows, r * dq:(r + 1) * dq]
            s = lax.dot_general(q, k, (((1,), (1,)), ((), ())), preferred_element_type=F32)
            p = jnp.exp2(s - jnp.max(s, axis=-1, keepdims=True))
            l = jnp.sum(p, axis=-1, keepdims=True)
            o = jnp.dot(p.astype(BF16), v, preferred_element_type=F32)
            o_ref[rows, r * dv:(r + 1) * dv] = (o / l).astype(o_ref.dtype)


def _attention(q, k, v, n_groups, rep, dq, dv, k_col0, v_col0, tq):
    b, t, _ = q.shape
    s_len = k.shape[1]
    tq = _tile(t, tq)
    sub = _tile(tq, 256)
    kb, vb = k_col0 // dq, v_col0 // dv
    assert kb * dq == k_col0 and vb * dv == v_col0
    return pl.pallas_call(
        functools.partial(_attn_kernel, rep=rep, dq=dq, dv=dv, sub=sub),
        grid=(b, n_groups, t // tq),
        in_specs=[pl.BlockSpec((None, tq, rep * dq), lambda bi, g, qi: (bi, qi, g)),
                  pl.BlockSpec((None, s_len, dq), lambda bi, g, qi: (bi, 0, kb + g)),
                  pl.BlockSpec((None, s_len, dv), lambda bi, g, qi: (bi, 0, vb + g))],
        out_specs=pl.BlockSpec((None, tq, rep * dv), lambda bi, g, qi: (bi, qi, g)),
        out_shape=jax.ShapeDtypeStruct((b, t, n_groups * rep * dv), BF16),
        compiler_params=_params("parallel", "parallel", "parallel"),
        name="attention",
    )(q, k, v)


def _top16(s, idx):
    n, c = s.shape
    row = lax.broadcasted_iota(jnp.int32, (PEER_TOPK, c), 0)
    big = float(PEER_TOPK * PEER_TOPK)

    def body(r, carry):
        s, vals, idxs = carry
        mx = jnp.max(s, axis=0, keepdims=True)
        am = jnp.min(jnp.where(s == mx, idx, big), axis=0, keepdims=True)
        vals = jnp.where(row == r, mx, vals)
        idxs = jnp.where(row == r, am, idxs)
        s = jnp.where(idx == am, -jnp.inf, s)
        return s, vals, idxs

    init = (s, jnp.zeros((PEER_TOPK, c), F32), jnp.zeros((PEER_TOPK, c), F32))
    _, vals, idxs = lax.fori_loop(0, PEER_TOPK, body, init)
    return vals, idxs


def _pick(table, pos):
    out = jnp.zeros(pos.shape, F32)
    for a in range(PEER_TOPK):
        out = jnp.where(pos == a, table[a:a + 1], out)
    return out


_CAND_ROWS = [(0, 0), (0, 8), (1, 0), (2, 0), (3, 0), (4, 0), (5, 0), (6, 0), (7, 0)]


def _route_kernel(q_ref, k_ref, oi_ref, oj_ref, og_ref):
    q = q_ref[...]
    tn = q.shape[0]
    half = q.shape[1] // 2
    nt = (((1,), (1,)), ((), ()))
    s0 = lax.dot_general(k_ref[0], q[:, :half], nt, preferred_element_type=F32)
    s1 = lax.dot_general(k_ref[1], q[:, half:], nt, preferred_element_type=F32)
    s01 = jnp.concatenate([s0, s1], axis=1)
    v01, i01 = _top16(s01, lax.broadcasted_iota(jnp.int32, s01.shape, 0).astype(F32))
    v0, v1, i0, i1 = v01[:, :tn], v01[:, tn:], i01[:, :tn], i01[:, tn:]
    sub = lax.broadcasted_iota(jnp.int32, (SUBLANE, tn), 0).astype(F32)
    cand = [v0[a:a + 1] + v1[b:b + SUBLANE] for a, b in _CAND_ROWS]
    flat = [sub + float(a * PEER_TOPK + b) for a, b in _CAND_ROWS]
    cand.append(v0[SUBLANE:] + v1[0:1])
    flat.append((sub + float(SUBLANE)) * float(PEER_TOPK))
    best, pos = _top16(jnp.concatenate(cand, axis=0), jnp.concatenate(flat, axis=0))
    pos = pos.astype(jnp.int32)
    e = jnp.exp(best - best[0:1])
    oi_ref[...] = _pick(i0, pos >> 4)
    oj_ref[...] = _pick(i1, pos & (PEER_TOPK - 1))
    og_ref[...] = e / jnp.sum(e, axis=0, keepdims=True)


def _route(qp, sub_keys, tn=256):
    m = qp.shape[0]
    tn = _tile(m, tn)
    qd = qp.shape[1] // PEER_HEADS
    shp = jax.ShapeDtypeStruct((PEER_HEADS * PEER_TOPK, m), F32)
    spec = pl.BlockSpec((PEER_TOPK, tn), lambda t, h: (h, t))
    return pl.pallas_call(
        _route_kernel,
        grid=(m // tn, PEER_HEADS),
        in_specs=[pl.BlockSpec((tn, qd), lambda t, h: (t, h)),
                  pl.BlockSpec(sub_keys.shape, lambda t, h: (0, 0, 0))],
        out_specs=[spec, spec, spec],
        out_shape=[shp, shp, shp],
        compiler_params=_params("parallel", "parallel"),
        name="peer_route",
    )(qp, sub_keys)


def _wbuild_kernel(i_ref, j_ref, g_ref, o_ref, it_s, jt_s, gt_s):
    it_s[...] = i_ref[...].T
    jt_s[...] = j_ref[...].T
    gt_s[...] = g_ref[...].T
    ne = i_ref.shape[0]
    iota_a = lax.broadcasted_iota(jnp.int32, (2 * N_KEYS, 2 * ne), 0).astype(F32).astype(BF16)
    iota_g = lax.broadcasted_iota(jnp.int32, (N_KEYS, 2 * ne), 0).astype(F32).astype(BF16)
    one, zero = jnp.ones((), BF16), jnp.zeros((), BF16)

    def body(m, carry):
        ri = it_s[pl.ds(2 * m, 2), :]
        rj = jt_s[pl.ds(2 * m, 2), :]
        rg = gt_s[pl.ds(2 * m, 2), :]
        row_i = jnp.concatenate([ri[0:1], ri[1:2] + float(N_KEYS)], axis=1).astype(BF16)
        row_j = jnp.concatenate([rj[0:1], rj[1:2]], axis=1).astype(BF16)
        row_g = jnp.concatenate([rg[0:1], rg[1:2]], axis=1).astype(BF16)
        a2 = jnp.where(iota_a == row_i, one, zero)
        g2 = jnp.where(iota_g == row_j, row_g, zero)
        w = lax.dot_general(a2, g2, (((1,), (1,)), ((), ())), preferred_element_type=F32)
        for t in range(2):
            o_ref[:, 2 * m + t] = w[t * N_KEYS:(t + 1) * N_KEYS].reshape(N_KEYS // SUBLANE, SUBLANE, N_KEYS)
        return carry

    lax.fori_loop(0, o_ref.shape[1] // 2, body, 0, unroll=4)


def _wbuild(idx_i, idx_j, gates):
    ne, m = idx_i.shape
    tw = LANE
    spec = pl.BlockSpec((ne, tw), lambda t: (0, t))
    return pl.pallas_call(
        _wbuild_kernel,
        grid=(m // tw,),
        in_specs=[spec, spec, spec],
        out_specs=pl.BlockSpec((N_KEYS // SUBLANE, tw, SUBLANE, N_KEYS), lambda t: (0, t, 0, 0)),
        out_shape=jax.ShapeDtypeStruct((N_KEYS // SUBLANE, m, SUBLANE, N_KEYS), F32),
        scratch_shapes=[pltpu.VMEM((tw, ne), F32)] * 3,
        compiler_params=_params("parallel"),
        name="peer_gate_map",
    )(idx_i, idx_j, gates)


def _peer_kernel(x_ref, ut_ref, vp_ref, vc_ref, w_ref, o_ref, ga_s, gb_s, *, rows, n_pairs):
    k = pl.program_id(1)
    ec = rows * N_KEYS
    base = (k % (SUBLANE // (2 * rows))) * (2 * rows)

    def gates(half, dst):
        s = jnp.dot(x_ref[...], ut_ref[:, half * ec:(half + 1) * ec], preferred_element_type=F32)
        for il in range(rows):
            wg = w_ref[:, base + half * rows + il, :]
            sl = s[:, il * N_KEYS:(il + 1) * N_KEYS]
            g = 0.5 * sl * (1.0 + lax.erf(sl * INV_SQRT2)) * wg
            dst[:, il * N_KEYS:(il + 1) * N_KEYS] = g.astype(BF16)

    def mix(src, v_ref):
        return jnp.dot(src[...], v_ref[...], preferred_element_type=F32)

    @pl.when(k == 0)
    def _():
        gates(0, ga_s)
        gates(1, gb_s)
        o_ref[...] = mix(ga_s, vc_ref)

    @pl.when(jnp.logical_and(k > 0, k < n_pairs))
    def _():
        gates(0, ga_s)
        o_ref[...] += mix(gb_s, vp_ref)
        gates(1, gb_s)
        o_ref[...] += mix(ga_s, vc_ref)

    @pl.when(k == n_pairs)
    def _():
        o_ref[...] += mix(gb_s, vp_ref)


def _peer_mix(h2, u_t, v, wmap, tn=1024):
    m, d = h2.shape
    n_exp = v.shape[0]
    tn = _tile(m, tn)
    rows = PEER_CHUNK_ROWS
    ec = rows * N_KEYS
    n_pairs = n_exp // (2 * ec)
    assert u_t.shape == (n_pairs, d, 2 * ec)
    per = SUBLANE // (2 * rows)
    last = n_pairs - 1
    return pl.pallas_call(
        functools.partial(_peer_kernel, rows=rows, n_pairs=n_pairs),
        grid=(m // tn, n_pairs + 1),
        in_specs=[pl.BlockSpec((tn, d), lambda t, k: (t, 0), pipeline_mode=pl.Buffered(1)),
                  pl.BlockSpec((None, d, 2 * ec), lambda t, k: (jnp.minimum(k, last), 0, 0)),
                  pl.BlockSpec((ec, d), lambda t, k: (jnp.maximum(2 * k - 1, 0), 0)),
                  pl.BlockSpec((ec, d), lambda t, k: (jnp.minimum(2 * k, 2 * last + 1), 0)),
                  pl.BlockSpec((None, tn, SUBLANE, N_KEYS), lambda t, k: (jnp.minimum(k, last) // per, t, 0, 0))],
        out_specs=pl.BlockSpec((tn, d), lambda t, k: (t, 0), pipeline_mode=pl.Buffered(1)),
        out_shape=jax.ShapeDtypeStruct((m, d), F32),
        scratch_shapes=[pltpu.VMEM((tn, ec), BF16), pltpu.VMEM((tn, ec), BF16)],
        compiler_params=_params("parallel", "arbitrary"),
        name="peer_mix",
    )(h2, u_t, v, v, wmap)


def _final_kernel(x_ref, p_ref, m_ref, g_ref, o_ref, *, gate_row):
    x = x_ref[...] + m_ref[0][gate_row:gate_row + 1] * p_ref[...]
    o_ref[...] = _rms(x, g_ref[...])


def _final(x, po, mods, g_final, rows_per_batch, gate_row):
    m, d = x.shape
    tm = _tile(rows_per_batch, 256)
    per = rows_per_batch // tm
    row = lambda i: (i, 0)
    return pl.pallas_call(
        functools.partial(_final_kernel, gate_row=gate_row),
        grid=(m // tm,),
        in_specs=[pl.BlockSpec((tm, d), row), pl.BlockSpec((tm, d), row),
                  pl.BlockSpec((1, 8, d), lambda i: (i // per, 0, 0)),
                  pl.BlockSpec((1, d), lambda i: (0, 0))],
        out_specs=pl.BlockSpec((tm, d), row),
        out_shape=jax.ShapeDtypeStruct((m, d), F32),
        compiler_params=_params("parallel"),
        name="final_norm",
    )(x, po, mods, g_final.reshape(1, d))


def _rope_tables(n_tokens, dim):
    n_rows = n_tokens // GRID_W
    rows = jnp.broadcast_to(jnp.arange(n_rows)[:, None], (n_rows, GRID_W)).reshape(-1).astype(F32)
    cols = jnp.broadcast_to(jnp.arange(GRID_W)[None, :], (n_rows, GRID_W)).reshape(-1).astype(F32)
    n_freq = dim // 4
    freqs = ROPE_BASE ** (-jnp.arange(n_freq, dtype=F32) / n_freq)
    ang = jnp.concatenate([rows[:, None] * freqs, cols[:, None] * freqs], axis=-1)
    cos, sin = jnp.cos(ang), jnp.sin(ang)
    cos2 = jnp.repeat(cos, 2, axis=-1)
    sin2 = jnp.stack([-sin, sin], axis=-1).reshape(n_tokens, dim)
    return cos2, sin2


def _layer_weights(w_in, w_q_b, w_kv_b, w_mla_o, w_gqa_o, w_out, w_peer_q, sub_keys, peer_u, peer_v):
    q_rank, kv_rank = w_q_b.shape[0], w_kv_b.shape[0]
    k0 = q_rank + kv_rank
    lw = {}
    g0 = k0 + MLA_ROPE + (GQA_HEADS + 2 * GQA_KV_HEADS) * HEAD_DIM
    lw['w_in'] = jnp.concatenate([w_in[:, :k0], w_in[:, k0 + MLA_ROPE:g0]], axis=1).astype(BF16)
    lw['w_gates'] = w_in[:, g0:].astype(BF16)
    lw['w_kr'] = jnp.pad(w_in[:, k0:k0 + MLA_ROPE], ((0, 0), (0, LANE - MLA_ROPE))).astype(BF16)
    wq = w_q_b.reshape(q_rank, MLA_HEADS, MLA_NOPE + MLA_ROPE)
    wq = jnp.pad(wq, ((0, 0), (0, 0), (0, MLA_QPAD - MLA_NOPE - MLA_ROPE)))
    lw['w_q_b'] = wq.reshape(q_rank, MLA_HEADS * MLA_QPAD).astype(BF16)
    wkv = w_kv_b.reshape(kv_rank, MLA_HEADS, MLA_NOPE + MLA_V)
    wk = jnp.zeros((kv_rank + LANE, MLA_HEADS, MLA_QPAD), F32)
    wk = wk.at[:kv_rank, :, :MLA_NOPE].set(wkv[:, :, :MLA_NOPE])
    eye = jnp.broadcast_to(jnp.eye(MLA_ROPE, dtype=F32)[:, None, :], (MLA_ROPE, MLA_HEADS, MLA_ROPE))
    wk = wk.at[kv_rank:kv_rank + MLA_ROPE, :, MLA_NOPE:MLA_NOPE + MLA_ROPE].set(eye)
    wv = jnp.pad(wkv[:, :, MLA_NOPE:], ((0, LANE), (0, 0), (0, 0)))
    lw['w_kv'] = jnp.concatenate([wk.reshape(kv_rank + LANE, -1), wv.reshape(kv_rank + LANE, -1)],
                                 axis=1).astype(BF16)
    lw['w_mla_o'] = w_mla_o.astype(BF16)
    lw['w_gqa_o'] = w_gqa_o.astype(BF16)
    lw['w_out'] = w_out.astype(BF16)
    lw['w_peer_q'] = w_peer_q.astype(BF16)
    lw['sub_keys'] = sub_keys.astype(BF16)
    pair = 2 * PEER_CHUNK_ROWS * N_KEYS
    lw['peer_ut'] = peer_u.reshape(-1, pair, peer_u.shape[1]).transpose(0, 2, 1).astype(BF16)
    lw['peer_v'] = peer_v.astype(BF16)
    return lw


def _layer(x, mods, lw, gains, rope, ctx):
    b, t, d = x.shape
    m = b * t
    x2 = x.reshape(m, d)
    g_norm1, g_norm2, g_q_a, g_kv_a, g_gqa_q, g_gqa_k = gains
    q_rank, kv_rank = g_q_a.shape[0], g_kv_a.shape[0]
    nq, nk = GQA_HEADS * HEAD_DIM, GQA_KV_HEADS * HEAD_DIM
    col_gate_mla = q_rank + kv_rank + nq + 2 * nk

    rpb = t if mods.shape[0] == b else m
    h = _norm_mod(x2, g_norm1, mods, rpb, SHIFT1, SCALE1)
    proj = _mm(h, lw['w_in'], F32)
    gates = _mm(h, lw['w_gates'], BF16, sigmoid=True)
    kr = _mm(h, lw['w_kr'], F32)
    prep_rope = None if rope is None else rope[:4]
    qa, ckv32, ckvx, gq, gk32, gk, gv = _prep(proj, kr, g_q_a, g_kv_a, g_gqa_q, g_gqa_k, prep_rope, t)

    mla_scale = (MLA_NOPE + MLA_ROPE) ** -0.5 * LOG2E
    if rope is None:
        q_mla = _mm(qa, lw['w_q_b'], BF16, scale=mla_scale)
    else:
        q_mla = _mm(qa, lw['w_q_b'], BF16, scale=mla_scale, rope=rope[4:], rows_per_batch=t)

    ckvx = ckvx.reshape(b, t, -1)
    gk_all = gk.reshape(b, t, nk)
    gv_all = gv.reshape(b, t, nk)
    if ctx is not None:
        c_ckv, c_kr, c_k, c_v = ctx
        past = c_ckv.shape[1]
        c_x = jnp.concatenate([c_ckv, c_kr, jnp.zeros((b, past, LANE - MLA_ROPE), F32)], axis=-1).astype(BF16)
        ckvx = jnp.concatenate([ckvx, c_x], axis=1)
        gk_all = jnp.concatenate([gk_all, c_k.reshape(b, past, nk).astype(BF16)], axis=1)
        gv_all = jnp.concatenate([gv_all, c_v.reshape(b, past, nk).astype(BF16)], axis=1)
    s_len = ckvx.shape[1]
    kv = _mm(ckvx.reshape(b * s_len, -1), lw['w_kv'], BF16).reshape(b, s_len, -1)

    o_mla = _attention(q_mla.reshape(b, t, -1), kv, kv, MLA_HEADS, 1, MLA_QPAD, MLA_V,
                       0, MLA_HEADS * MLA_QPAD, tq=512)
    o_gqa = _attention(gq.reshape(b, t, -1), gk_all, gv_all, GQA_KV_HEADS, GQA_HEADS // GQA_KV_HEADS,
                       HEAD_DIM, HEAD_DIM, 0, 0, tq=256)
    merged = _merge(o_mla.reshape(m, -1), o_gqa.reshape(m, -1), lw['w_mla_o'], lw['w_gqa_o'],
                    gates)
    x1 = _mm_residual(merged, lw['w_out'], x2, mods, rpb, GATE1)

    h2 = _norm_mod(x1, g_norm2, mods, rpb, SHIFT2, SCALE2)
    qp = _mm(h2, lw['w_peer_q'], BF16)
    idx_i, idx_j, gates = _route(qp, lw['sub_keys'])
    wmap = _wbuild(idx_i, idx_j, gates)
    po = _peer_mix(h2, lw['peer_ut'], lw['peer_v'], wmap)
    own = (ckv32.reshape(b, t, kv_rank), kr[:, :MLA_ROPE].reshape(b, t, MLA_ROPE),
           gk32.reshape(b, t, GQA_KV_HEADS, HEAD_DIM),
           proj[:, col_gate_mla - nk:col_gate_mla].reshape(b, t, GQA_KV_HEADS, HEAD_DIM))
    return x1, po, own


def kernel(x_prompt, x_sample, c, cache_mla_ckv, cache_mla_krope, cache_gqa_k, cache_gqa_v, c_ctx, w_mod, b_mod, g_norm1, g_norm2, w_in, g_q_a, w_q_b, g_kv_a, w_kv_b, g_gqa_q, g_gqa_k, w_mla_o, w_gqa_o, w_out, w_peer_q, peer_sub_keys, peer_u, peer_v, g_final):
    depth = w_in.shape[0]
    d = x_prompt.shape[-1]
    bc, tc = x_prompt.shape[:2]
    bl, tl = x_sample.shape[:2]

    cg, sg = _rope_tables(tl, HEAD_DIM)
    cm, sm = _rope_tables(tl, MLA_ROPE)
    one = jnp.ones((tl, LANE - MLA_ROPE), F32)
    zero = jnp.zeros((tl, LANE - MLA_ROPE), F32)
    cm_k, sm_k = jnp.concatenate([cm, one], 1), jnp.concatenate([sm, zero], 1)
    ones_n, zeros_n = jnp.ones((tl, MLA_NOPE), F32), jnp.zeros((tl, MLA_NOPE), F32)
    cq = jnp.tile(jnp.concatenate([ones_n, cm, one], 1), (1, 2))
    sq = jnp.tile(jnp.concatenate([zeros_n, sm, zero], 1), (1, 2))
    rope = (cg, sg, cm_k, sm_k, cq, sq)

    n_rows = bl + 1
    pad_rows = -n_rows % 8
    cvecs = jnp.concatenate([c, c_ctx[None, :], jnp.zeros((pad_rows, d), F32)], axis=0)

    xc, xl = x_prompt, x_sample
    new = [[], [], [], []]
    for l in range(depth):
        lw = _layer_weights(w_in[l], w_q_b[l], w_kv_b[l], w_mla_o[l], w_gqa_o[l], w_out[l],
                            w_peer_q[l], peer_sub_keys[l], peer_u[l], peer_v[l])
        gains = (g_norm1[l], g_norm2[l], g_q_a[l], g_kv_a[l], g_gqa_q[l], g_gqa_k[l])
        mod = _modulation(cvecs, w_mod[l], b_mod[l]).reshape(n_rows + pad_rows, N_MOD, d)
        mod = jnp.pad(mod, ((0, 0), (0, 8 - N_MOD), (0, 0)))
        mods_lat, mods_ctx = mod[:bl], mod[bl:bl + 1]
        last = l == depth - 1

        x1, po, own = _layer(xc, mods_ctx, lw, gains, None, None)
        for acc, o in zip(new, own):
            acc.append(o)
        xc = _finish(x1, po, mods_ctx, g_final, bc * tc, last).reshape(bc, tc, d)

        ctx = (cache_mla_ckv[:, l], cache_mla_krope[:, l], cache_gqa_k[:, l], cache_gqa_v[:, l])
        x1, po, _ = _layer(xl, mods_lat, lw, gains, rope, ctx)
        xl = _finish(x1, po, mods_lat, g_final, tl, last).reshape(bl, tl, d)

    return (xc, xl, jnp.stack(new[0], axis=1), jnp.stack(new[1], axis=1),
            jnp.stack(new[2], axis=1), jnp.stack(new[3], axis=1))


def _finish(x1, po, mods, g_final, rows_per_batch, last):
    d = x1.shape[1]
    g = g_final if last else None
    if g is None:
        raise NotImplementedError("only the final layer's residual is fused with the output norm")
    return _final(x1, po, mods, g, rows_per_batch, GATE2)
```

```python
import functools
import math

import jax
import jax.numpy as jnp
from jax import lax
from jax.experimental import pallas as pl
from jax.experimental.pallas import tpu as pltpu

F32 = jnp.float32
BF16 = jnp.bfloat16

GRID_W = 64
EPS = 1e-6
ROPE_BASE = 10000.0
N_MOD = 6
MLA_HEADS = 16
MLA_NOPE = 128
MLA_ROPE = 64
MLA_V = 128
MLA_QPAD = 256
GQA_HEADS = 16
GQA_KV_HEADS = 4
HEAD_DIM = 128
PEER_HEADS = 8
N_KEYS = 128
PEER_TOPK = 16
PEER_CHUNK_ROWS = 2
LANE = 128
SUBLANE = 8
VMEM_LIMIT = 56 * 1024 * 1024
INV_SQRT2 = 1.0 / math.sqrt(2.0)
LOG2E = 1.0 / math.log(2.0)

SHIFT1, SCALE1, GATE1, SHIFT2, SCALE2, GATE2 = range(6)


def _tile(dim, pref):
    t = min(dim, pref)
    assert dim % t == 0, (dim, pref)
    return t


def _params(*sem):
    return pltpu.CompilerParams(dimension_semantics=sem, vmem_limit_bytes=VMEM_LIMIT)


def _mod_kernel(c_ref, w_ref, b_ref, o_ref):
    c = c_ref[...]
    a = (c * jax.nn.sigmoid(c)).astype(BF16)
    o_ref[...] = jnp.dot(a, w_ref[...].astype(BF16), preferred_element_type=F32) + b_ref[...]


def _modulation(cvecs, w_mod, b_mod):
    r, d = cvecs.shape
    n = w_mod.shape[1]
    tn = _tile(n, 512)
    return pl.pallas_call(
        _mod_kernel,
        grid=(n // tn,),
        in_specs=[pl.BlockSpec((r, d), lambda j: (0, 0)),
                  pl.BlockSpec((d, tn), lambda j: (0, j)),
                  pl.BlockSpec((1, tn), lambda j: (0, j))],
        out_specs=pl.BlockSpec((r, tn), lambda j: (0, j)),
        out_shape=jax.ShapeDtypeStruct((r, n), F32),
        compiler_params=_params("parallel"),
        name="modulation",
    )(cvecs, w_mod, b_mod.reshape(1, n))


def _norm_mod_kernel(x_ref, g_ref, m_ref, o_ref, *, shift_row, scale_row):
    x = x_ref[...]
    y = x * lax.rsqrt(jnp.mean(x * x, axis=-1, keepdims=True) + EPS) * g_ref[...]
    m = m_ref[0]
    o_ref[...] = (y * (1.0 + m[scale_row:scale_row + 1]) + m[shift_row:shift_row + 1]).astype(BF16)


def _norm_mod(x, g, mods, rows_per_batch, shift_row, scale_row):
    m, d = x.shape
    tm = _tile(rows_per_batch, 256)
    per = rows_per_batch // tm
    return pl.pallas_call(
        functools.partial(_norm_mod_kernel, shift_row=shift_row, scale_row=scale_row),
        grid=(m // tm,),
        in_specs=[pl.BlockSpec((tm, d), lambda i: (i, 0)),
                  pl.BlockSpec((1, d), lambda i: (0, 0)),
                  pl.BlockSpec((1, 8, d), lambda i: (i // per, 0, 0))],
        out_specs=pl.BlockSpec((tm, d), lambda i: (i, 0)),
        out_shape=jax.ShapeDtypeStruct((m, d), BF16),
        compiler_params=_params("parallel"),
        name="norm_mod",
    )(x, g.reshape(1, d), mods)


def _swap_pairs(x):
    ax = x.ndim - 1
    n = x.shape[ax]
    lane = lax.broadcasted_iota(jnp.int32, x.shape, ax)
    nxt = pltpu.roll(x, n - 1, axis=ax)
    prv = pltpu.roll(x, 1, axis=ax)
    return jnp.where((lane & 1) == 0, nxt, prv)


def _mm_kernel(a_ref, b_ref, o_ref, *, scale, sigmoid):
    acc = jnp.dot(a_ref[...], b_ref[...], preferred_element_type=F32)
    if scale is not None:
        acc = acc * scale
    if sigmoid:
        acc = jax.nn.sigmoid(acc)
    o_ref[...] = acc.astype(o_ref.dtype)


def _mm_rope_kernel(a_ref, b_ref, cos_ref, sin_ref, o_ref, *, scale):
    acc = jnp.dot(a_ref[...], b_ref[...], preferred_element_type=F32)
    acc = acc * cos_ref[...] + _swap_pairs(acc) * sin_ref[...]
    o_ref[...] = (acc * scale).astype(o_ref.dtype)


def _mm(a, b, out_dtype, scale=None, sigmoid=False, rope=None, rows_per_batch=None, tm=1024, tn=512):
    m, k = a.shape
    n = b.shape[1]
    tm = _tile(m if rows_per_batch is None else rows_per_batch, tm)
    tn = _tile(n, tn)
    in_specs = [pl.BlockSpec((tm, k), lambda i, j: (i, 0)),
                pl.BlockSpec((k, tn), lambda i, j: (0, j))]
    args = [a, b]
    if rope is None:
        body = functools.partial(_mm_kernel, scale=scale, sigmoid=sigmoid)
    else:
        per = rows_per_batch // tm
        body = functools.partial(_mm_rope_kernel, scale=scale)
        in_specs += [pl.BlockSpec((tm, tn), lambda i, j: (i % per, 0))] * 2
        args += list(rope)
    return pl.pallas_call(
        body,
        grid=(m // tm, n // tn),
        in_specs=in_specs,
        out_specs=pl.BlockSpec((tm, tn), lambda i, j: (i, j)),
        out_shape=jax.ShapeDtypeStruct((m, n), out_dtype),
        compiler_params=_params("parallel", "parallel"),
        name="matmul",
    )(*args)


def _mm_res_kernel(a_ref, b_ref, x_ref, m_ref, o_ref, *, gate_row):
    acc = jnp.dot(a_ref[...], b_ref[...], preferred_element_type=F32)
    o_ref[...] = x_ref[...] + m_ref[0][gate_row:gate_row + 1] * acc


def _mm_residual(a, b, x, mods, rows_per_batch, gate_row, tm=1024, tn=512):
    m, k = a.shape
    n = b.shape[1]
    tm = _tile(rows_per_batch, tm)
    tn = _tile(n, tn)
    per = rows_per_batch // tm
    return pl.pallas_call(
        functools.partial(_mm_res_kernel, gate_row=gate_row),
        grid=(m // tm, n // tn),
        in_specs=[pl.BlockSpec((tm, k), lambda i, j: (i, 0)),
                  pl.BlockSpec((k, tn), lambda i, j: (0, j)),
                  pl.BlockSpec((tm, tn), lambda i, j: (i, j)),
                  pl.BlockSpec((1, 8, tn), lambda i, j: (i // per, 0, j))],
        out_specs=pl.BlockSpec((tm, tn), lambda i, j: (i, j)),
        out_shape=jax.ShapeDtypeStruct((m, n), F32),
        compiler_params=_params("parallel", "parallel"),
        name="matmul_residual",
    )(a, b, x, mods)


def _merge_kernel(oa_ref, ob_ref, wa_ref, wb_ref, ga_ref, gb_ref, o_ref):
    ya = jnp.dot(oa_ref[...], wa_ref[...], preferred_element_type=F32)
    yb = jnp.dot(ob_ref[...], wb_ref[...], preferred_element_type=F32)
    o_ref[...] = (ga_ref[...] * ya + gb_ref[...] * yb).astype(o_ref.dtype)


def _merge(o_mla, o_gqa, w_mla_o, w_gqa_o, gates, tm=1024, tn=512):
    m, k = o_mla.shape
    n = w_mla_o.shape[1]
    tm = _tile(m, tm)
    tn = _tile(n, tn)
    ca, cb = 0, n // tn
    return pl.pallas_call(
        _merge_kernel,
        grid=(m // tm, n // tn),
        in_specs=[pl.BlockSpec((tm, k), lambda i, j: (i, 0)),
                  pl.BlockSpec((tm, k), lambda i, j: (i, 0)),
                  pl.BlockSpec((k, tn), lambda i, j: (0, j)),
                  pl.BlockSpec((k, tn), lambda i, j: (0, j)),
                  pl.BlockSpec((tm, tn), lambda i, j: (i, ca + j)),
                  pl.BlockSpec((tm, tn), lambda i, j: (i, cb + j))],
        out_specs=pl.BlockSpec((tm, tn), lambda i, j: (i, j)),
        out_shape=jax.ShapeDtypeStruct((m, n), BF16),
        compiler_params=_params("parallel", "parallel"),
        name="merge",
    )(o_mla, o_gqa, w_mla_o, w_gqa_o, gates, gates)


def _rms(x, g):
    return x * lax.rsqrt(jnp.mean(x * x, axis=-1, keepdims=True) + EPS) * g


def _prep_kernel(*refs, q_rank, kv_rank, use_rope):
    if use_rope:
        (p_ref, kr_ref, gqa_ref, gkv_ref, ggq_ref, ggk_ref, cg_ref, sg_ref, cm_ref, sm_ref,
         qa_o, ckv_o, ckvx_o, gq_o, gk32_o, gk_o, gv_o) = refs
    else:
        (p_ref, kr_ref, gqa_ref, gkv_ref, ggq_ref, ggk_ref,
         qa_o, ckv_o, ckvx_o, gq_o, gk32_o, gk_o, gv_o) = refs
    c0 = q_rank
    c1 = c0 + kv_rank
    c2 = c1 + GQA_HEADS * HEAD_DIM
    c3 = c2 + GQA_KV_HEADS * HEAD_DIM
    qa_o[...] = _rms(p_ref[:, 0:c0], gqa_ref[...]).astype(BF16)
    ckv = _rms(p_ref[:, c0:c1], gkv_ref[...])
    ckv_o[...] = ckv
    ckvx_o[:, 0:kv_rank] = ckv.astype(BF16)
    kr = kr_ref[...]
    if use_rope:
        kr = kr * cm_ref[...] + _swap_pairs(kr) * sm_ref[...]
    ckvx_o[:, kv_rank:kv_rank + LANE] = kr.astype(BF16)
    q_scale = HEAD_DIM ** -0.5 * LOG2E
    for h in range(GQA_HEADS):
        x = _rms(p_ref[:, c1 + h * HEAD_DIM:c1 + (h + 1) * HEAD_DIM], ggq_ref[...])
        if use_rope:
            x = x * cg_ref[...] + _swap_pairs(x) * sg_ref[...]
        gq_o[:, h * HEAD_DIM:(h + 1) * HEAD_DIM] = (x * q_scale).astype(BF16)
    for h in range(GQA_KV_HEADS):
        x = _rms(p_ref[:, c2 + h * HEAD_DIM:c2 + (h + 1) * HEAD_DIM], ggk_ref[...])
        if use_rope:
            x = x * cg_ref[...] + _swap_pairs(x) * sg_ref[...]
        gk32_o[:, h * HEAD_DIM:(h + 1) * HEAD_DIM] = x
        gk_o[:, h * HEAD_DIM:(h + 1) * HEAD_DIM] = x.astype(BF16)
    gv_o[...] = p_ref[:, c3:c3 + GQA_KV_HEADS * HEAD_DIM].astype(BF16)


def _prep(proj, kr, g_q_a, g_kv_a, g_gqa_q, g_gqa_k, rope, rows_per_batch):
    m = proj.shape[0]
    q_rank, kv_rank = g_q_a.shape[0], g_kv_a.shape[0]
    width = q_rank + kv_rank + (GQA_HEADS + 2 * GQA_KV_HEADS) * HEAD_DIM
    tm = _tile(rows_per_batch, 256)
    per = rows_per_batch // tm
    nq, nk = GQA_HEADS * HEAD_DIM, GQA_KV_HEADS * HEAD_DIM
    row = lambda i: (i, 0)
    const = lambda i: (0, 0)
    in_specs = [pl.BlockSpec((tm, width), row), pl.BlockSpec((tm, LANE), row),
                pl.BlockSpec((1, q_rank), const), pl.BlockSpec((1, kv_rank), const),
                pl.BlockSpec((1, HEAD_DIM), const), pl.BlockSpec((1, HEAD_DIM), const)]
    args = [proj, kr, g_q_a.reshape(1, -1), g_kv_a.reshape(1, -1), g_gqa_q.reshape(1, -1), g_gqa_k.reshape(1, -1)]
    if rope is not None:
        in_specs += [pl.BlockSpec((tm, LANE), lambda i: (i % per, 0))] * 4
        args += list(rope)
    outs = [((m, q_rank), BF16), ((m, kv_rank), F32), ((m, kv_rank + LANE), BF16), ((m, nq), BF16),
            ((m, nk), F32), ((m, nk), BF16), ((m, nk), BF16)]
    return pl.pallas_call(
        functools.partial(_prep_kernel, q_rank=q_rank, kv_rank=kv_rank, use_rope=rope is not None),
        grid=(m // tm,),
        in_specs=in_specs,
        out_specs=[pl.BlockSpec((tm, s[1]), row) for s, _ in outs],
        out_shape=[jax.ShapeDtypeStruct(s, dt) for s, dt in outs],
        compiler_params=_params("parallel"),
        name="prep",
    )(*args)


def _attn_kernel(q_ref, k_ref, v_ref, o_ref, *, rep, dq, dv, sub):
    k = k_ref[...]
    v = v_ref[...]
    for r in range(rep):
        for i in range(q_ref.shape[0] // sub):
            rows = slice(i * sub, (i + 1) * sub)
            q = q_ref[rows, r * dq:(r + 1) * dq]
            s = lax.dot_general(q, k, (((1,), (1,)), ((), ())), preferred_element_type=F32)
            p = jnp.exp2(s - jnp.max(s, axis=-1, keepdims=True))
            l = jnp.sum(p, axis=-1, keepdims=True)
            o = jnp.dot(p.astype(BF16), v, preferred_element_type=F32)
            o_ref[rows, r * dv:(r + 1) * dv] = (o / l).astype(o_ref.dtype)


def _attention(q, k, v, n_groups, rep, dq, dv, k_col0, v_col0, tq):
    b, t, _ = q.shape
    s_len = k.shape[1]
    tq = _tile(t, tq)
    sub = _tile(tq, 256)
    kb, vb = k_col0 // dq, v_col0 // dv
    assert kb * dq == k_col0 and vb * dv == v_col0
    return pl.pallas_call(
        functools.partial(_attn_kernel, rep=rep, dq=dq, dv=dv, sub=sub),
        grid=(b, n_groups, t // tq),
        in_specs=[pl.BlockSpec((None, tq, rep * dq), lambda bi, g, qi: (bi, qi, g)),
                  pl.BlockSpec((None, s_len, dq), lambda bi, g, qi: (bi, 0, kb + g)),
                  pl.BlockSpec((None, s_len, dv), lambda bi, g, qi: (bi, 0, vb + g))],
        out_specs=pl.BlockSpec((None, tq, rep * dv), lambda bi, g, qi: (bi, qi, g)),
        out_shape=jax.ShapeDtypeStruct((b, t, n_groups * rep * dv), BF16),
        compiler_params=_params("parallel", "parallel", "parallel"),
        name="attention",
    )(q, k, v)


def _top16(s, idx):
    n, c = s.shape
    row = lax.broadcasted_iota(jnp.int32, (PEER_TOPK, c), 0)
    big = float(PEER_TOPK * PEER_TOPK)

    def body(r, carry):
        s, vals, idxs = carry
        mx = jnp.max(s, axis=0, keepdims=True)
        am = jnp.min(jnp.where(s == mx, idx, big), axis=0, keepdims=True)
        vals = jnp.where(row == r, mx, vals)
        idxs = jnp.where(row == r, am, idxs)
        s = jnp.where(idx == am, -jnp.inf, s)
        return s, vals, idxs

    init = (s, jnp.zeros((PEER_TOPK, c), F32), jnp.zeros((PEER_TOPK, c), F32))
    _, vals, idxs = lax.fori_loop(0, PEER_TOPK, body, init)
    return vals, idxs


def _pick(table, pos):
    out = jnp.zeros(pos.shape, F32)
    for a in range(PEER_TOPK):
        out = jnp.where(pos == a, table[a:a + 1], out)
    return out


_CAND_ROWS = [(0, 0), (0, 8), (1, 0), (2, 0), (3, 0), (4, 0), (5, 0), (6, 0), (7, 0)]


def _route_kernel(q_ref, k_ref, oi_ref, oj_ref, og_ref):
    q = q_ref[...]
    tn = q.shape[0]
    half = q.shape[1] // 2
    nt = (((1,), (1,)), ((), ()))
    s0 = lax.dot_general(k_ref[0], q[:, :half], nt, preferred_element_type=F32)
    s1 = lax.dot_general(k_ref[1], q[:, half:], nt, preferred_element_type=F32)
    s01 = jnp.concatenate([s0, s1], axis=1)
    v01, i01 = _top16(s01, lax.broadcasted_iota(jnp.int32, s01.shape, 0).astype(F32))
    v0, v1, i0, i1 = v01[:, :tn], v01[:, tn:], i01[:, :tn], i01[:, tn:]
    sub = lax.broadcasted_iota(jnp.int32, (SUBLANE, tn), 0).astype(F32)
    cand = [v0[a:a + 1] + v1[b:b + SUBLANE] for a, b in _CAND_ROWS]
    flat = [sub + float(a * PEER_TOPK + b) for a, b in _CAND_ROWS]
    cand.append(v0[SUBLANE:] + v1[0:1])
    flat.append((sub + float(SUBLANE)) * float(PEER_TOPK))
    best, pos = _top16(jnp.concatenate(cand, axis=0), jnp.concatenate(flat, axis=0))
    pos = pos.astype(jnp.int32)
    e = jnp.exp(best - best[0:1])
    oi_ref[...] = _pick(i0, pos >> 4)
    oj_ref[...] = _pick(i1, pos & (PEER_TOPK - 1))
    og_ref[...] = e / jnp.sum(e, axis=0, keepdims=True)


def _route(qp, sub_keys, tn=512):
    m = qp.shape[0]
    tn = _tile(m, tn)
    qd = qp.shape[1] // PEER_HEADS
    shp = jax.ShapeDtypeStruct((PEER_HEADS * PEER_TOPK, m), F32)
    spec = pl.BlockSpec((PEER_TOPK, tn), lambda t, h: (h, t))
    return pl.pallas_call(
        _route_kernel,
        grid=(m // tn, PEER_HEADS),
        in_specs=[pl.BlockSpec((tn, qd), lambda t, h: (t, h)),
                  pl.BlockSpec(sub_keys.shape, lambda t, h: (0, 0, 0))],
        out_specs=[spec, spec, spec],
        out_shape=[shp, shp, shp],
        compiler_params=_params("parallel", "parallel"),
        name="peer_route",
    )(qp, sub_keys)


def _wbuild_kernel(i_ref, j_ref, g_ref, o_ref, it_s, jt_s, gt_s):
    it_s[...] = i_ref[...].T
    jt_s[...] = j_ref[...].T
    gt_s[...] = g_ref[...].T
    ne = i_ref.shape[0]
    iota_a = lax.broadcasted_iota(jnp.int32, (2 * N_KEYS, 2 * ne), 0).astype(F32).astype(BF16)
    iota_g = lax.broadcasted_iota(jnp.int32, (N_KEYS, 2 * ne), 0).astype(F32).astype(BF16)
    one, zero = jnp.ones((), BF16), jnp.zeros((), BF16)

    def body(m, carry):
        ri = it_s[pl.ds(2 * m, 2), :]
        rj = jt_s[pl.ds(2 * m, 2), :]
        rg = gt_s[pl.ds(2 * m, 2), :]
        row_i = jnp.concatenate([ri[0:1], ri[1:2] + float(N_KEYS)], axis=1).astype(BF16)
        row_j = jnp.concatenate([rj[0:1], rj[1:2]], axis=1).astype(BF16)
        row_g = jnp.concatenate([rg[0:1], rg[1:2]], axis=1).astype(BF16)
        a2 = jnp.where(iota_a == row_i, one, zero)
        g2 = jnp.where(iota_g == row_j, row_g, zero)
        w = lax.dot_general(a2, g2, (((1,), (1,)), ((), ())), preferred_element_type=F32)
        for t in range(2):
            o_ref[:, 2 * m + t] = w[t * N_KEYS:(t + 1) * N_KEYS].reshape(N_KEYS // SUBLANE, SUBLANE, N_KEYS)
        return carry

    lax.fori_loop(0, o_ref.shape[1] // 2, body, 0, unroll=8)


def _wbuild(idx_i, idx_j, gates):
    ne, m = idx_i.shape
    tw = LANE
    spec = pl.BlockSpec((ne, tw), lambda t: (0, t))
    return pl.pallas_call(
        _wbuild_kernel,
        grid=(m // tw,),
        in_specs=[spec, spec, spec],
        out_specs=pl.BlockSpec((N_KEYS // SUBLANE, tw, SUBLANE, N_KEYS), lambda t: (0, t, 0, 0)),
        out_shape=jax.ShapeDtypeStruct((N_KEYS // SUBLANE, m, SUBLANE, N_KEYS), F32),
        scratch_shapes=[pltpu.VMEM((tw, ne), F32)] * 3,
        compiler_params=_params("parallel"),
        name="peer_gate_map",
    )(idx_i, idx_j, gates)


def _peer_kernel(x_ref, ut_ref, vp_ref, vc_ref, w_ref, o_ref, ga_s, gb_s, *, rows, n_pairs):
    k = pl.program_id(1)
    ec = rows * N_KEYS
    base = (k % (SUBLANE // (2 * rows))) * (2 * rows)

    def gates(half, dst):
        s = jnp.dot(x_ref[...], ut_ref[:, half * ec:(half + 1) * ec], preferred_element_type=F32)
        for il in range(rows):
            wg = w_ref[:, base + half * rows + il, :]
            sl = s[:, il * N_KEYS:(il + 1) * N_KEYS]
            g = 0.5 * sl * (1.0 + lax.erf(sl * INV_SQRT2)) * wg
            dst[:, il * N_KEYS:(il + 1) * N_KEYS] = g.astype(BF16)

    def mix(src, v_ref):
        return jnp.dot(src[...], v_ref[...], preferred_element_type=F32)

    @pl.when(k == 0)
    def _():
        gates(0, ga_s)
        gates(1, gb_s)
        o_ref[...] = mix(ga_s, vc_ref)

    @pl.when(jnp.logical_and(k > 0, k < n_pairs))
    def _():
        gates(0, ga_s)
        o_ref[...] += mix(gb_s, vp_ref)
        gates(1, gb_s)
        o_ref[...] += mix(ga_s, vc_ref)

    @pl.when(k == n_pairs)
    def _():
        o_ref[...] += mix(gb_s, vp_ref)


def _peer_mix(h2, u_t, v, wmap, tn=1024):
    m, d = h2.shape
    n_exp = v.shape[0]
    tn = _tile(m, tn)
    rows = PEER_CHUNK_ROWS
    ec = rows * N_KEYS
    n_pairs = n_exp // (2 * ec)
    assert u_t.shape == (n_pairs, d, 2 * ec)
    per = SUBLANE // (2 * rows)
    last = n_pairs - 1
    return pl.pallas_call(
        functools.partial(_peer_kernel, rows=rows, n_pairs=n_pairs),
        grid=(m // tn, n_pairs + 1),
        in_specs=[pl.BlockSpec((tn, d), lambda t, k: (t, 0), pipeline_mode=pl.Buffered(1)),
                  pl.BlockSpec((None, d, 2 * ec), lambda t, k: (jnp.minimum(k, last), 0, 0)),
                  pl.BlockSpec((ec, d), lambda t, k: (jnp.maximum(2 * k - 1, 0), 0)),
                  pl.BlockSpec((ec, d), lambda t, k: (jnp.minimum(2 * k, 2 * last + 1), 0)),
                  pl.BlockSpec((None, tn, SUBLANE, N_KEYS), lambda t, k: (jnp.minimum(k, last) // per, t, 0, 0))],
        out_specs=pl.BlockSpec((tn, d), lambda t, k: (t, 0), pipeline_mode=pl.Buffered(1)),
        out_shape=jax.ShapeDtypeStruct((m, d), F32),
        scratch_shapes=[pltpu.VMEM((tn, ec), BF16), pltpu.VMEM((tn, ec), BF16)],
        compiler_params=_params("parallel", "arbitrary"),
        name="peer_mix",
    )(h2, u_t, v, v, wmap)


def _final_kernel(x_ref, p_ref, m_ref, g_ref, o_ref, *, gate_row):
    x = x_ref[...] + m_ref[0][gate_row:gate_row + 1] * p_ref[...]
    o_ref[...] = _rms(x, g_ref[...])


def _final(x, po, mods, g_final, rows_per_batch, gate_row):
    m, d = x.shape
    tm = _tile(rows_per_batch, 256)
    per = rows_per_batch // tm
    row = lambda i: (i, 0)
    return pl.pallas_call(
        functools.partial(_final_kernel, gate_row=gate_row),
        grid=(m // tm,),
        in_specs=[pl.BlockSpec((tm, d), row), pl.BlockSpec((tm, d), row),
                  pl.BlockSpec((1, 8, d), lambda i: (i // per, 0, 0)),
                  pl.BlockSpec((1, d), lambda i: (0, 0))],
        out_specs=pl.BlockSpec((tm, d), row),
        out_shape=jax.ShapeDtypeStruct((m, d), F32),
        compiler_params=_params("parallel"),
        name="final_norm",
    )(x, po, mods, g_final.reshape(1, d))


def _rope_tables(n_tokens, dim):
    n_rows = n_tokens // GRID_W
    rows = jnp.broadcast_to(jnp.arange(n_rows)[:, None], (n_rows, GRID_W)).reshape(-1).astype(F32)
    cols = jnp.broadcast_to(jnp.arange(GRID_W)[None, :], (n_rows, GRID_W)).reshape(-1).astype(F32)
    n_freq = dim // 4
    freqs = ROPE_BASE ** (-jnp.arange(n_freq, dtype=F32) / n_freq)
    ang = jnp.concatenate([rows[:, None] * freqs, cols[:, None] * freqs], axis=-1)
    cos, sin = jnp.cos(ang), jnp.sin(ang)
    cos2 = jnp.repeat(cos, 2, axis=-1)
    sin2 = jnp.stack([-sin, sin], axis=-1).reshape(n_tokens, dim)
    return cos2, sin2


def _layer_weights(w_in, w_q_b, w_kv_b, w_mla_o, w_gqa_o, w_out, w_peer_q, sub_keys, peer_u, peer_v):
    q_rank, kv_rank = w_q_b.shape[0], w_kv_b.shape[0]
    k0 = q_rank + kv_rank
    lw = {}
    g0 = k0 + MLA_ROPE + (GQA_HEADS + 2 * GQA_KV_HEADS) * HEAD_DIM
    lw['w_in'] = jnp.concatenate([w_in[:, :k0], w_in[:, k0 + MLA_ROPE:g0]], axis=1).astype(BF16)
    lw['w_gates'] = w_in[:, g0:].astype(BF16)
    lw['w_kr'] = jnp.pad(w_in[:, k0:k0 + MLA_ROPE], ((0, 0), (0, LANE - MLA_ROPE))).astype(BF16)
    wq = w_q_b.reshape(q_rank, MLA_HEADS, MLA_NOPE + MLA_ROPE)
    wq = jnp.pad(wq, ((0, 0), (0, 0), (0, MLA_QPAD - MLA_NOPE - MLA_ROPE)))
    lw['w_q_b'] = wq.reshape(q_rank, MLA_HEADS * MLA_QPAD).astype(BF16)
    wkv = w_kv_b.reshape(kv_rank, MLA_HEADS, MLA_NOPE + MLA_V)
    wk = jnp.zeros((kv_rank + LANE, MLA_HEADS, MLA_QPAD), F32)
    wk = wk.at[:kv_rank, :, :MLA_NOPE].set(wkv[:, :, :MLA_NOPE])
    eye = jnp.broadcast_to(jnp.eye(MLA_ROPE, dtype=F32)[:, None, :], (MLA_ROPE, MLA_HEADS, MLA_ROPE))
    wk = wk.at[kv_rank:kv_rank + MLA_ROPE, :, MLA_NOPE:MLA_NOPE + MLA_ROPE].set(eye)
    wv = jnp.pad(wkv[:, :, MLA_NOPE:], ((0, LANE), (0, 0), (0, 0)))
    lw['w_kv'] = jnp.concatenate([wk.reshape(kv_rank + LANE, -1), wv.reshape(kv_rank + LANE, -1)],
                                 axis=1).astype(BF16)
    lw['w_mla_o'] = w_mla_o.astype(BF16)
    lw['w_gqa_o'] = w_gqa_o.astype(BF16)
    lw['w_out'] = w_out.astype(BF16)
    lw['w_peer_q'] = w_peer_q.astype(BF16)
    lw['sub_keys'] = sub_keys.astype(BF16)
    pair = 2 * PEER_CHUNK_ROWS * N_KEYS
    lw['peer_ut'] = peer_u.reshape(-1, pair, peer_u.shape[1]).transpose(0, 2, 1).astype(BF16)
    lw['peer_v'] = peer_v.astype(BF16)
    return lw


def _layer(x, mods, lw, gains, rope, ctx):
    b, t, d = x.shape
    m = b * t
    x2 = x.reshape(m, d)
    g_norm1, g_norm2, g_q_a, g_kv_a, g_gqa_q, g_gqa_k = gains
    q_rank, kv_rank = g_q_a.shape[0], g_kv_a.shape[0]
    nq, nk = GQA_HEADS * HEAD_DIM, GQA_KV_HEADS * HEAD_DIM
    col_gate_mla = q_rank + kv_rank + nq + 2 * nk

    rpb = t if mods.shape[0] == b else m
    h = _norm_mod(x2, g_norm1, mods, rpb, SHIFT1, SCALE1)
    proj = _mm(h, lw['w_in'], F32)
    gates = _mm(h, lw['w_gates'], BF16, sigmoid=True)
    kr = _mm(h, lw['w_kr'], F32)
    prep_rope = None if rope is None else rope[:4]
    qa, ckv32, ckvx, gq, gk32, gk, gv = _prep(proj, kr, g_q_a, g_kv_a, g_gqa_q, g_gqa_k, prep_rope, t)

    mla_scale = (MLA_NOPE + MLA_ROPE) ** -0.5 * LOG2E
    if rope is None:
        q_mla = _mm(qa, lw['w_q_b'], BF16, scale=mla_scale)
    else:
        q_mla = _mm(qa, lw['w_q_b'], BF16, scale=mla_scale, rope=rope[4:], rows_per_batch=t)

    ckvx = ckvx.reshape(b, t, -1)
    gk_all = gk.reshape(b, t, nk)
    gv_all = gv.reshape(b, t, nk)
    if ctx is not None:
        c_ckv, c_kr, c_k, c_v = ctx
        past = c_ckv.shape[1]
        c_x = jnp.concatenate([c_ckv, c_kr, jnp.zeros((b, past, LANE - MLA_ROPE), F32)], axis=-1).astype(BF16)
        ckvx = jnp.concatenate([ckvx, c_x], axis=1)
        gk_all = jnp.concatenate([gk_all, c_k.reshape(b, past, nk).astype(BF16)], axis=1)
        gv_all = jnp.concatenate([gv_all, c_v.reshape(b, past, nk).astype(BF16)], axis=1)
    s_len = ckvx.shape[1]
    kv = _mm(ckvx.reshape(b * s_len, -1), lw['w_kv'], BF16).reshape(b, s_len, -1)

    o_mla = _attention(q_mla.reshape(b, t, -1), kv, kv, MLA_HEADS, 1, MLA_QPAD, MLA_V,
                       0, MLA_HEADS * MLA_QPAD, tq=2048)
    o_gqa = _attention(gq.reshape(b, t, -1), gk_all, gv_all, GQA_KV_HEADS, GQA_HEADS // GQA_KV_HEADS,
                       HEAD_DIM, HEAD_DIM, 0, 0, tq=512)
    merged = _merge(o_mla.reshape(m, -1), o_gqa.reshape(m, -1), lw['w_mla_o'], lw['w_gqa_o'],
                    gates)
    x1 = _mm_residual(merged, lw['w_out'], x2, mods, rpb, GATE1)

    h2 = _norm_mod(x1, g_norm2, mods, rpb, SHIFT2, SCALE2)
    qp = _mm(h2, lw['w_peer_q'], BF16)
    idx_i, idx_j, gates = _route(qp, lw['sub_keys'])
    wmap = _wbuild(idx_i, idx_j, gates)
    po = _peer_mix(h2, lw['peer_ut'], lw['peer_v'], wmap)
    own = (ckv32.reshape(b, t, kv_rank), kr[:, :MLA_ROPE].reshape(b, t, MLA_ROPE),
           gk32.reshape(b, t, GQA_KV_HEADS, HEAD_DIM),
           proj[:, col_gate_mla - nk:col_gate_mla].reshape(b, t, GQA_KV_HEADS, HEAD_DIM))
    return x1, po, own


def kernel(x_prompt, x_sample, c, cache_mla_ckv, cache_mla_krope, cache_gqa_k, cache_gqa_v, c_ctx, w_mod, b_mod, g_norm1, g_norm2, w_in, g_q_a, w_q_b, g_kv_a, w_kv_b, g_gqa_q, g_gqa_k, w_mla_o, w_gqa_o, w_out, w_peer_q, peer_sub_keys, peer_u, peer_v, g_final):
    depth = w_in.shape[0]
    d = x_prompt.shape[-1]
    bc, tc = x_prompt.shape[:2]
    bl, tl = x_sample.shape[:2]

    cg, sg = _rope_tables(tl, HEAD_DIM)
    cm, sm = _rope_tables(tl, MLA_ROPE)
    one = jnp.ones((tl, LANE - MLA_ROPE), F32)
    zero = jnp.zeros((tl, LANE - MLA_ROPE), F32)
    cm_k, sm_k = jnp.concatenate([cm, one], 1), jnp.concatenate([sm, zero], 1)
    ones_n, zeros_n = jnp.ones((tl, MLA_NOPE), F32), jnp.zeros((tl, MLA_NOPE), F32)
    cq = jnp.tile(jnp.concatenate([ones_n, cm, one], 1), (1, 2))
    sq = jnp.tile(jnp.concatenate([zeros_n, sm, zero], 1), (1, 2))
    rope = (cg, sg, cm_k, sm_k, cq, sq)

    n_rows = bl + 1
    pad_rows = -n_rows % 8
    cvecs = jnp.concatenate([c, c_ctx[None, :], jnp.zeros((pad_rows, d), F32)], axis=0)

    xc, xl = x_prompt, x_sample
    new = [[], [], [], []]
    for l in range(depth):
        lw = _layer_weights(w_in[l], w_q_b[l], w_kv_b[l], w_mla_o[l], w_gqa_o[l], w_out[l],
                            w_peer_q[l], peer_sub_keys[l], peer_u[l], peer_v[l])
        gains = (g_norm1[l], g_norm2[l], g_q_a[l], g_kv_a[l], g_gqa_q[l], g_gqa_k[l])
        mod = _modulation(cvecs, w_mod[l], b_mod[l]).reshape(n_rows + pad_rows, N_MOD, d)
        mod = jnp.pad(mod, ((0, 0), (0, 8 - N_MOD), (0, 0)))
        mods_lat, mods_ctx = mod[:bl], mod[bl:bl + 1]
        last = l == depth - 1

        x1, po, own = _layer(xc, mods_ctx, lw, gains, None, None)
        for acc, o in zip(new, own):
            acc.append(o)
        xc = _finish(x1, po, mods_ctx, g_final, bc * tc, last).reshape(bc, tc, d)

        ctx = (cache_mla_ckv[:, l], cache_mla_krope[:, l], cache_gqa_k[:, l], cache_gqa_v[:, l])
        x1, po, _ = _layer(xl, mods_lat, lw, gains, rope, ctx)
        xl = _finish(x1, po, mods_lat, g_final, tl, last).reshape(bl, tl, d)

    return (xc, xl, jnp.stack(new[0], axis=1), jnp.stack(new[1], axis=1),
            jnp.stack(new[2], axis=1), jnp.stack(new[3], axis=1))


def _finish(x1, po, mods, g_final, rows_per_batch, last):
    d = x1.shape[1]
    g = g_final if last else None
    if g is None:
        raise NotImplementedError("only the final layer's residual is fused with the output norm")
    return _final(x1, po, mods, g, rows_per_batch, GATE2)
```

```python
import functools
import math

import jax
import jax.numpy as jnp
from jax import lax
from jax.experimental import pallas as pl
from jax.experimental.pallas import tpu as pltpu

F32 = jnp.float32
BF16 = jnp.bfloat16

GRID_W = 64
EPS = 1e-6
ROPE_BASE = 10000.0
N_MOD = 6
MLA_HEADS = 16
MLA_NOPE = 128
MLA_ROPE = 64
MLA_V = 128
MLA_QPAD = 256
GQA_HEADS = 16
GQA_KV_HEADS = 4
HEAD_DIM = 128
PEER_HEADS = 8
N_KEYS = 128
PEER_TOPK = 16
PEER_CHUNK_ROWS = 4
LANE = 128
SUBLANE = 8
VMEM_LIMIT = 56 * 1024 * 1024
INV_SQRT2 = 1.0 / math.sqrt(2.0)
LOG2E = 1.0 / math.log(2.0)

SHIFT1, SCALE1, GATE1, SHIFT2, SCALE2, GATE2 = range(6)


def _tile(dim, pref):
    t = min(dim, pref)
    assert dim % t == 0, (dim, pref)
    return t


def _params(*sem):
    return pltpu.CompilerParams(dimension_semantics=sem, vmem_limit_bytes=VMEM_LIMIT)


def _mod_kernel(c_ref, w_ref, b_ref, o_ref):
    c = c_ref[...]
    a = (c * jax.nn.sigmoid(c)).astype(BF16)
    o_ref[...] = jnp.dot(a, w_ref[...].astype(BF16), preferred_element_type=F32) + b_ref[...]


def _modulation(cvecs, w_mod, b_mod):
    r, d = cvecs.shape
    n = w_mod.shape[1]
    tn = _tile(n, 512)
    return pl.pallas_call(
        _mod_kernel,
        grid=(n // tn,),
        in_specs=[pl.BlockSpec((r, d), lambda j: (0, 0)),
                  pl.BlockSpec((d, tn), lambda j: (0, j)),
                  pl.BlockSpec((1, tn), lambda j: (0, j))],
        out_specs=pl.BlockSpec((r, tn), lambda j: (0, j)),
        out_shape=jax.ShapeDtypeStruct((r, n), F32),
        compiler_params=_params("parallel"),
        name="modulation",
    )(cvecs, w_mod, b_mod.reshape(1, n))


def _norm_mod_kernel(x_ref, g_ref, m_ref, o_ref, *, shift_row, scale_row):
    x = x_ref[...]
    y = x * lax.rsqrt(jnp.mean(x * x, axis=-1, keepdims=True) + EPS) * g_ref[...]
    m = m_ref[0]
    o_ref[...] = (y * (1.0 + m[scale_row:scale_row + 1]) + m[shift_row:shift_row + 1]).astype(BF16)


def _norm_mod(x, g, mods, rows_per_batch, shift_row, scale_row):
    m, d = x.shape
    tm = _tile(rows_per_batch, 256)
    per = rows_per_batch // tm
    return pl.pallas_call(
        functools.partial(_norm_mod_kernel, shift_row=shift_row, scale_row=scale_row),
        grid=(m // tm,),
        in_specs=[pl.BlockSpec((tm, d), lambda i: (i, 0)),
                  pl.BlockSpec((1, d), lambda i: (0, 0)),
                  pl.BlockSpec((1, 8, d), lambda i: (i // per, 0, 0))],
        out_specs=pl.BlockSpec((tm, d), lambda i: (i, 0)),
        out_shape=jax.ShapeDtypeStruct((m, d), BF16),
        compiler_params=_params("parallel"),
        name="norm_mod",
    )(x, g.reshape(1, d), mods)


def _swap_pairs(x):
    ax = x.ndim - 1
    n = x.shape[ax]
    lane = lax.broadcasted_iota(jnp.int32, x.shape, ax)
    nxt = pltpu.roll(x, n - 1, axis=ax)
    prv = pltpu.roll(x, 1, axis=ax)
    return jnp.where((lane & 1) == 0, nxt, prv)


def _mm_kernel(a_ref, b_ref, o_ref, *, scale, sigmoid):
    acc = jnp.dot(a_ref[...], b_ref[...], preferred_element_type=F32)
    if scale is not None:
        acc = acc * scale
    if sigmoid:
        acc = jax.nn.sigmoid(acc)
    o_ref[...] = acc.astype(o_ref.dtype)


def _mm_rope_kernel(a_ref, b_ref, cos_ref, sin_ref, o_ref, *, scale):
    acc = jnp.dot(a_ref[...], b_ref[...], preferred_element_type=F32)
    acc = acc * cos_ref[...] + _swap_pairs(acc) * sin_ref[...]
    o_ref[...] = (acc * scale).astype(o_ref.dtype)


def _mm(a, b, out_dtype, scale=None, sigmoid=False, rope=None, rows_per_batch=None, tm=1024, tn=512):
    m, k = a.shape
    n = b.shape[1]
    tm = _tile(m if rows_per_batch is None else rows_per_batch, tm)
    tn = _tile(n, tn)
    in_specs = [pl.BlockSpec((tm, k), lambda i, j: (i, 0)),
                pl.BlockSpec((k, tn), lambda i, j: (0, j))]
    args = [a, b]
    if rope is None:
        body = functools.partial(_mm_kernel, scale=scale, sigmoid=sigmoid)
    else:
        per = rows_per_batch // tm
        body = functools.partial(_mm_rope_kernel, scale=scale)
        in_specs += [pl.BlockSpec((tm, tn), lambda i, j: (i % per, 0))] * 2
        args += list(rope)
    return pl.pallas_call(
        body,
        grid=(m // tm, n // tn),
        in_specs=in_specs,
        out_specs=pl.BlockSpec((tm, tn), lambda i, j: (i, j)),
        out_shape=jax.ShapeDtypeStruct((m, n), out_dtype),
        compiler_params=_params("parallel", "parallel"),
        name="matmul",
    )(*args)


def _mm_res_kernel(a_ref, b_ref, x_ref, m_ref, o_ref, *, gate_row):
    acc = jnp.dot(a_ref[...], b_ref[...], preferred_element_type=F32)
    o_ref[...] = x_ref[...] + m_ref[0][gate_row:gate_row + 1] * acc


def _mm_residual(a, b, x, mods, rows_per_batch, gate_row, tm=1024, tn=512):
    m, k = a.shape
    n = b.shape[1]
    tm = _tile(rows_per_batch, tm)
    tn = _tile(n, tn)
    per = rows_per_batch // tm
    return pl.pallas_call(
        functools.partial(_mm_res_kernel, gate_row=gate_row),
        grid=(m // tm, n // tn),
        in_specs=[pl.BlockSpec((tm, k), lambda i, j: (i, 0)),
                  pl.BlockSpec((k, tn), lambda i, j: (0, j)),
                  pl.BlockSpec((tm, tn), lambda i, j: (i, j)),
                  pl.BlockSpec((1, 8, tn), lambda i, j: (i // per, 0, j))],
        out_specs=pl.BlockSpec((tm, tn), lambda i, j: (i, j)),
        out_shape=jax.ShapeDtypeStruct((m, n), F32),
        compiler_params=_params("parallel", "parallel"),
        name="matmul_residual",
    )(a, b, x, mods)


def _merge_kernel(oa_ref, ob_ref, wa_ref, wb_ref, ga_ref, gb_ref, o_ref):
    ya = jnp.dot(oa_ref[...], wa_ref[...], preferred_element_type=F32)
    yb = jnp.dot(ob_ref[...], wb_ref[...], preferred_element_type=F32)
    o_ref[...] = (ga_ref[...] * ya + gb_ref[...] * yb).astype(o_ref.dtype)


def _merge(o_mla, o_gqa, w_mla_o, w_gqa_o, gates, tm=1024, tn=512):
    m, k = o_mla.shape
    n = w_mla_o.shape[1]
    tm = _tile(m, tm)
    tn = _tile(n, tn)
    ca, cb = 0, n // tn
    return pl.pallas_call(
        _merge_kernel,
        grid=(m // tm, n // tn),
        in_specs=[pl.BlockSpec((tm, k), lambda i, j: (i, 0)),
                  pl.BlockSpec((tm, k), lambda i, j: (i, 0)),
                  pl.BlockSpec((k, tn), lambda i, j: (0, j)),
                  pl.BlockSpec((k, tn), lambda i, j: (0, j)),
                  pl.BlockSpec((tm, tn), lambda i, j: (i, ca + j)),
                  pl.BlockSpec((tm, tn), lambda i, j: (i, cb + j))],
        out_specs=pl.BlockSpec((tm, tn), lambda i, j: (i, j)),
        out_shape=jax.ShapeDtypeStruct((m, n), BF16),
        compiler_params=_params("parallel", "parallel"),
        name="merge",
    )(o_mla, o_gqa, w_mla_o, w_gqa_o, gates, gates)


def _rms(x, g):
    return x * lax.rsqrt(jnp.mean(x * x, axis=-1, keepdims=True) + EPS) * g


def _prep_kernel(*refs, q_rank, kv_rank, use_rope):
    if use_rope:
        (p_ref, kr_ref, gqa_ref, gkv_ref, ggq_ref, ggk_ref, cg_ref, sg_ref, cm_ref, sm_ref,
         qa_o, ckv_o, ckvx_o, gq_o, gk32_o, gk_o, gv_o) = refs
    else:
        (p_ref, kr_ref, gqa_ref, gkv_ref, ggq_ref, ggk_ref,
         qa_o, ckv_o, ckvx_o, gq_o, gk32_o, gk_o, gv_o) = refs
    c0 = q_rank
    c1 = c0 + kv_rank
    c2 = c1 + GQA_HEADS * HEAD_DIM
    c3 = c2 + GQA_KV_HEADS * HEAD_DIM
    qa_o[...] = _rms(p_ref[:, 0:c0], gqa_ref[...]).astype(BF16)
    ckv = _rms(p_ref[:, c0:c1], gkv_ref[...])
    ckv_o[...] = ckv
    ckvx_o[:, 0:kv_rank] = ckv.astype(BF16)
    kr = kr_ref[...]
    if use_rope:
        kr = kr * cm_ref[...] + _swap_pairs(kr) * sm_ref[...]
    ckvx_o[:, kv_rank:kv_rank + LANE] = kr.astype(BF16)
    q_scale = HEAD_DIM ** -0.5 * LOG2E
    for h in range(GQA_HEADS):
        x = _rms(p_ref[:, c1 + h * HEAD_DIM:c1 + (h + 1) * HEAD_DIM], ggq_ref[...])
        if use_rope:
            x = x * cg_ref[...] + _swap_pairs(x) * sg_ref[...]
        gq_o[:, h * HEAD_DIM:(h + 1) * HEAD_DIM] = (x * q_scale).astype(BF16)
    for h in range(GQA_KV_HEADS):
        x = _rms(p_ref[:, c2 + h * HEAD_DIM:c2 + (h + 1) * HEAD_DIM], ggk_ref[...])
        if use_rope:
            x = x * cg_ref[...] + _swap_pairs(x) * sg_ref[...]
        gk32_o[:, h * HEAD_DIM:(h + 1) * HEAD_DIM] = x
        gk_o[:, h * HEAD_DIM:(h + 1) * HEAD_DIM] = x.astype(BF16)
    gv_o[...] = p_ref[:, c3:c3 + GQA_KV_HEADS * HEAD_DIM].astype(BF16)


def _prep(proj, kr, g_q_a, g_kv_a, g_gqa_q, g_gqa_k, rope, rows_per_batch):
    m = proj.shape[0]
    q_rank, kv_rank = g_q_a.shape[0], g_kv_a.shape[0]
    width = q_rank + kv_rank + (GQA_HEADS + 2 * GQA_KV_HEADS) * HEAD_DIM
    tm = _tile(rows_per_batch, 256)
    per = rows_per_batch // tm
    nq, nk = GQA_HEADS * HEAD_DIM, GQA_KV_HEADS * HEAD_DIM
    row = lambda i: (i, 0)
    const = lambda i: (0, 0)
    in_specs = [pl.BlockSpec((tm, width), row), pl.BlockSpec((tm, LANE), row),
                pl.BlockSpec((1, q_rank), const), pl.BlockSpec((1, kv_rank), const),
                pl.BlockSpec((1, HEAD_DIM), const), pl.BlockSpec((1, HEAD_DIM), const)]
    args = [proj, kr, g_q_a.reshape(1, -1), g_kv_a.reshape(1, -1), g_gqa_q.reshape(1, -1), g_gqa_k.reshape(1, -1)]
    if rope is not None:
        in_specs += [pl.BlockSpec((tm, LANE), lambda i: (i % per, 0))] * 4
        args += list(rope)
    outs = [((m, q_rank), BF16), ((m, kv_rank), F32), ((m, kv_rank + LANE), BF16), ((m, nq), BF16),
            ((m, nk), F32), ((m, nk), BF16), ((m, nk), BF16)]
    return pl.pallas_call(
        functools.partial(_prep_kernel, q_rank=q_rank, kv_rank=kv_rank, use_rope=rope is not None),
        grid=(m // tm,),
        in_specs=in_specs,
        out_specs=[pl.BlockSpec((tm, s[1]), row) for s, _ in outs],
        out_shape=[jax.ShapeDtypeStruct(s, dt) for s, dt in outs],
        compiler_params=_params("parallel"),
        name="prep",
    )(*args)


def _attn_kernel(q_ref, k_ref, v_ref, o_ref, *, rep, dq, dv, sub):
    k = k_ref[...]
    v = v_ref[...]
    for r in range(rep):
        for i in range(q_ref.shape[0] // sub):
            rows = slice(i * sub, (i + 1) * sub)
            q = q_ref[rows, r * dq:(r + 1) * dq]
            s = lax.dot_general(q, k, (((1,), (1,)), ((), ())), preferred_element_type=F32)
            p = jnp.exp2(s - jnp.max(s, axis=-1, keepdims=True))
            l = jnp.sum(p, axis=-1, keepdims=True)
            o = jnp.dot(p.astype(BF16), v, preferred_element_type=F32)
            o_ref[rows, r * dv:(r + 1) * dv] = (o / l).astype(o_ref.dtype)


def _attention(q, k, v, n_groups, rep, dq, dv, k_col0, v_col0, tq):
    b, t, _ = q.shape
    s_len = k.shape[1]
    tq = _tile(t, tq)
    sub = _tile(tq, 256)
    kb, vb = k_col0 // dq, v_col0 // dv
    assert kb * dq == k_col0 and vb * dv == v_col0
    return pl.pallas_call(
        functools.partial(_attn_kernel, rep=rep, dq=dq, dv=dv, sub=sub),
        grid=(b, n_groups, t // tq),
        in_specs=[pl.BlockSpec((None, tq, rep * dq), lambda bi, g, qi: (bi, qi, g)),
                  pl.BlockSpec((None, s_len, dq), lambda bi, g, qi: (bi, 0, kb + g)),
                  pl.BlockSpec((None, s_len, dv), lambda bi, g, qi: (bi, 0, vb + g))],
        out_specs=pl.BlockSpec((None, tq, rep * dv), lambda bi, g, qi: (bi, qi, g)),
        out_shape=jax.ShapeDtypeStruct((b, t, n_groups * rep * dv), BF16),
        compiler_params=_params("parallel", "parallel", "parallel"),
        name="attention",
    )(q, k, v)


def _top16(s, idx):
    n, c = s.shape
    row = lax.broadcasted_iota(jnp.int32, (PEER_TOPK, c), 0)
    big = float(PEER_TOPK * PEER_TOPK)

    def body(r, carry):
        s, vals, idxs = carry
        mx = jnp.max(s, axis=0, keepdims=True)
        am = jnp.min(jnp.where(s == mx, idx, big), axis=0, keepdims=True)
        vals = jnp.where(row == r, mx, vals)
        idxs = jnp.where(row == r, am, idxs)
        s = jnp.where(idx == am, -jnp.inf, s)
        return s, vals, idxs

    init = (s, jnp.zeros((PEER_TOPK, c), F32), jnp.zeros((PEER_TOPK, c), F32))
    _, vals, idxs = lax.fori_loop(0, PEER_TOPK, body, init)
    return vals, idxs


def _pick(table, pos):
    out = jnp.zeros(pos.shape, F32)
    for a in range(PEER_TOPK):
        out = jnp.where(pos == a, table[a:a + 1], out)
    return out


_CAND_ROWS = [(0, 0), (0, 8), (1, 0), (2, 0), (3, 0), (4, 0), (5, 0), (6, 0), (7, 0)]


def _route_kernel(q_ref, k_ref, oi_ref, oj_ref, og_ref):
    q = q_ref[...]
    tn = q.shape[0]
    half = q.shape[1] // 2
    nt = (((1,), (1,)), ((), ()))
    s0 = lax.dot_general(k_ref[0], q[:, :half], nt, preferred_element_type=F32)
    s1 = lax.dot_general(k_ref[1], q[:, half:], nt, preferred_element_type=F32)
    s01 = jnp.concatenate([s0, s1], axis=1)
    v01, i01 = _top16(s01, lax.broadcasted_iota(jnp.int32, s01.shape, 0).astype(F32))
    v0, v1, i0, i1 = v01[:, :tn], v01[:, tn:], i01[:, :tn], i01[:, tn:]
    sub = lax.broadcasted_iota(jnp.int32, (SUBLANE, tn), 0).astype(F32)
    cand = [v0[a:a + 1] + v1[b:b + SUBLANE] for a, b in _CAND_ROWS]
    flat = [sub + float(a * PEER_TOPK + b) for a, b in _CAND_ROWS]
    cand.append(v0[SUBLANE:] + v1[0:1])
    flat.append((sub + float(SUBLANE)) * float(PEER_TOPK))
    best, pos = _top16(jnp.concatenate(cand, axis=0), jnp.concatenate(flat, axis=0))
    pos = pos.astype(jnp.int32)
    e = jnp.exp(best - best[0:1])
    oi_ref[...] = _pick(i0, pos >> 4)
    oj_ref[...] = _pick(i1, pos & (PEER_TOPK - 1))
    og_ref[...] = e / jnp.sum(e, axis=0, keepdims=True)


def _route(qp, sub_keys, tn=512):
    m = qp.shape[0]
    tn = _tile(m, tn)
    qd = qp.shape[1] // PEER_HEADS
    shp = jax.ShapeDtypeStruct((PEER_HEADS * PEER_TOPK, m), F32)
    spec = pl.BlockSpec((PEER_TOPK, tn), lambda t, h: (h, t))
    return pl.pallas_call(
        _route_kernel,
        grid=(m // tn, PEER_HEADS),
        in_specs=[pl.BlockSpec((tn, qd), lambda t, h: (t, h)),
                  pl.BlockSpec(sub_keys.shape, lambda t, h: (0, 0, 0))],
        out_specs=[spec, spec, spec],
        out_shape=[shp, shp, shp],
        compiler_params=_params("parallel", "parallel"),
        name="peer_route",
    )(qp, sub_keys)


def _wbuild_kernel(i_ref, j_ref, g_ref, o_ref, it_s, jt_s, gt_s):
    it_s[...] = i_ref[...].T
    jt_s[...] = j_ref[...].T
    gt_s[...] = g_ref[...].T
    ne = i_ref.shape[0]
    iota_a = lax.broadcasted_iota(jnp.int32, (2 * N_KEYS, 2 * ne), 0).astype(F32).astype(BF16)
    iota_g = lax.broadcasted_iota(jnp.int32, (N_KEYS, 2 * ne), 0).astype(F32).astype(BF16)
    one, zero = jnp.ones((), BF16), jnp.zeros((), BF16)

    def body(m, carry):
        ri = it_s[pl.ds(2 * m, 2), :]
        rj = jt_s[pl.ds(2 * m, 2), :]
        rg = gt_s[pl.ds(2 * m, 2), :]
        row_i = jnp.concatenate([ri[0:1], ri[1:2] + float(N_KEYS)], axis=1).astype(BF16)
        row_j = jnp.concatenate([rj[0:1], rj[1:2]], axis=1).astype(BF16)
        row_g = jnp.concatenate([rg[0:1], rg[1:2]], axis=1).astype(BF16)
        a2 = jnp.where(iota_a == row_i, one, zero)
        g2 = jnp.where(iota_g == row_j, row_g, zero)
        w = lax.dot_general(a2, g2, (((1,), (1,)), ((), ())), preferred_element_type=F32)
        for t in range(2):
            o_ref[:, 2 * m + t] = w[t * N_KEYS:(t + 1) * N_KEYS].reshape(N_KEYS // SUBLANE, SUBLANE, N_KEYS)
        return carry

    lax.fori_loop(0, o_ref.shape[1] // 2, body, 0, unroll=8)


def _wbuild(idx_i, idx_j, gates):
    ne, m = idx_i.shape
    tw = LANE
    spec = pl.BlockSpec((ne, tw), lambda t: (0, t))
    return pl.pallas_call(
        _wbuild_kernel,
        grid=(m // tw,),
        in_specs=[spec, spec, spec],
        out_specs=pl.BlockSpec((N_KEYS // SUBLANE, tw, SUBLANE, N_KEYS), lambda t: (0, t, 0, 0)),
        out_shape=jax.ShapeDtypeStruct((N_KEYS // SUBLANE, m, SUBLANE, N_KEYS), F32),
        scratch_shapes=[pltpu.VMEM((tw, ne), F32)] * 3,
        compiler_params=_params("parallel"),
        name="peer_gate_map",
    )(idx_i, idx_j, gates)


def _peer_kernel(x_ref, ut_ref, v_ref, w_ref, o_ref, ga_s, gb_s, *, rows, n_chunks):
    k = pl.program_id(1)
    base = (k % (SUBLANE // rows)) * rows

    def gates(dst):
        s = jnp.dot(x_ref[...], ut_ref[...], preferred_element_type=F32)
        for il in range(rows):
            wg = w_ref[:, base + il, :]
            sl = s[:, il * N_KEYS:(il + 1) * N_KEYS]
            g = 0.5 * sl * (1.0 + lax.erf(sl * INV_SQRT2)) * wg
            dst[:, il * N_KEYS:(il + 1) * N_KEYS] = g.astype(BF16)

    def mix(src):
        return jnp.dot(src[...], v_ref[...], preferred_element_type=F32)

    inner = jnp.logical_and(k > 0, k < n_chunks)

    @pl.when(k == 0)
    def _():
        o_ref[...] = jnp.zeros_like(o_ref)
        gates(ga_s)

    @pl.when(jnp.logical_and(inner, k % 2 == 1))
    def _():
        gates(gb_s)
        o_ref[...] += mix(ga_s)

    @pl.when(jnp.logical_and(inner, k % 2 == 0))
    def _():
        gates(ga_s)
        o_ref[...] += mix(gb_s)

    @pl.when(k == n_chunks)
    def _():
        o_ref[...] += mix(gb_s if n_chunks % 2 == 0 else ga_s)


def _peer_mix(h2, u_t, v, wmap, tn=1024):
    m, d = h2.shape
    n_exp = v.shape[0]
    tn = _tile(m, tn)
    rows = PEER_CHUNK_ROWS
    ec = rows * N_KEYS
    n_chunks = n_exp // ec
    assert u_t.shape == (n_chunks, d, ec)
    per = SUBLANE // rows
    last = n_chunks - 1
    return pl.pallas_call(
        functools.partial(_peer_kernel, rows=rows, n_chunks=n_chunks),
        grid=(m // tn, n_chunks + 1),
        in_specs=[pl.BlockSpec((tn, d), lambda t, k: (t, 0), pipeline_mode=pl.Buffered(1)),
                  pl.BlockSpec((None, d, ec), lambda t, k: (jnp.minimum(k, last), 0, 0)),
                  pl.BlockSpec((ec, d), lambda t, k: (jnp.maximum(k - 1, 0), 0)),
                  pl.BlockSpec((None, tn, SUBLANE, N_KEYS), lambda t, k: (jnp.minimum(k, last) // per, t, 0, 0))],
        out_specs=pl.BlockSpec((tn, d), lambda t, k: (t, 0), pipeline_mode=pl.Buffered(1)),
        out_shape=jax.ShapeDtypeStruct((m, d), F32),
        scratch_shapes=[pltpu.VMEM((tn, ec), BF16), pltpu.VMEM((tn, ec), BF16)],
        compiler_params=_params("parallel", "arbitrary"),
        name="peer_mix",
    )(h2, u_t, v, wmap)


def _final_kernel(x_ref, p_ref, m_ref, g_ref, o_ref, *, gate_row):
    x = x_ref[...] + m_ref[0][gate_row:gate_row + 1] * p_ref[...]
    o_ref[...] = _rms(x, g_ref[...])


def _final(x, po, mods, g_final, rows_per_batch, gate_row):
    m, d = x.shape
    tm = _tile(rows_per_batch, 256)
    per = rows_per_batch // tm
    row = lambda i: (i, 0)
    return pl.pallas_call(
        functools.partial(_final_kernel, gate_row=gate_row),
        grid=(m // tm,),
        in_specs=[pl.BlockSpec((tm, d), row), pl.BlockSpec((tm, d), row),
                  pl.BlockSpec((1, 8, d), lambda i: (i // per, 0, 0)),
                  pl.BlockSpec((1, d), lambda i: (0, 0))],
        out_specs=pl.BlockSpec((tm, d), row),
        out_shape=jax.ShapeDtypeStruct((m, d), F32),
        compiler_params=_params("parallel"),
        name="final_norm",
    )(x, po, mods, g_final.reshape(1, d))


def _rope_tables(n_tokens, dim):
    n_rows = n_tokens // GRID_W
    rows = jnp.broadcast_to(jnp.arange(n_rows)[:, None], (n_rows, GRID_W)).reshape(-1).astype(F32)
    cols = jnp.broadcast_to(jnp.arange(GRID_W)[None, :], (n_rows, GRID_W)).reshape(-1).astype(F32)
    n_freq = dim // 4
    freqs = ROPE_BASE ** (-jnp.arange(n_freq, dtype=F32) / n_freq)
    ang = jnp.concatenate([rows[:, None] * freqs, cols[:, None] * freqs], axis=-1)
    cos, sin = jnp.cos(ang), jnp.sin(ang)
    cos2 = jnp.repeat(cos, 2, axis=-1)
    sin2 = jnp.stack([-sin, sin], axis=-1).reshape(n_tokens, dim)
    return cos2, sin2


def _layer_weights(w_in, w_q_b, w_kv_b, w_mla_o, w_gqa_o, w_out, w_peer_q, sub_keys, peer_u, peer_v):
    q_rank, kv_rank = w_q_b.shape[0], w_kv_b.shape[0]
    k0 = q_rank + kv_rank
    lw = {}
    g0 = k0 + MLA_ROPE + (GQA_HEADS + 2 * GQA_KV_HEADS) * HEAD_DIM
    lw['w_in'] = jnp.concatenate([w_in[:, :k0], w_in[:, k0 + MLA_ROPE:g0]], axis=1).astype(BF16)
    lw['w_gates'] = w_in[:, g0:].astype(BF16)
    lw['w_kr'] = jnp.pad(w_in[:, k0:k0 + MLA_ROPE], ((0, 0), (0, LANE - MLA_ROPE))).astype(BF16)
    wq = w_q_b.reshape(q_rank, MLA_HEADS, MLA_NOPE + MLA_ROPE)
    wq = jnp.pad(wq, ((0, 0), (0, 0), (0, MLA_QPAD - MLA_NOPE - MLA_ROPE)))
    lw['w_q_b'] = wq.reshape(q_rank, MLA_HEADS * MLA_QPAD).astype(BF16)
    wkv = w_kv_b.reshape(kv_rank, MLA_HEADS, MLA_NOPE + MLA_V)
    wk = jnp.zeros((kv_rank + LANE, MLA_HEADS, MLA_QPAD), F32)
    wk = wk.at[:kv_rank, :, :MLA_NOPE].set(wkv[:, :, :MLA_NOPE])
    eye = jnp.broadcast_to(jnp.eye(MLA_ROPE, dtype=F32)[:, None, :], (MLA_ROPE, MLA_HEADS, MLA_ROPE))
    wk = wk.at[kv_rank:kv_rank + MLA_ROPE, :, MLA_NOPE:MLA_NOPE + MLA_ROPE].set(eye)
    wv = jnp.pad(wkv[:, :, MLA_NOPE:], ((0, LANE), (0, 0), (0, 0)))
    lw['w_kv'] = jnp.concatenate([wk.reshape(kv_rank + LANE, -1), wv.reshape(kv_rank + LANE, -1)],
                                 axis=1).astype(BF16)
    lw['w_mla_o'] = w_mla_o.astype(BF16)
    lw['w_gqa_o'] = w_gqa_o.astype(BF16)
    lw['w_out'] = w_out.astype(BF16)
    lw['w_peer_q'] = w_peer_q.astype(BF16)
    lw['sub_keys'] = sub_keys.astype(BF16)
    chunk = PEER_CHUNK_ROWS * N_KEYS
    lw['peer_ut'] = peer_u.reshape(-1, chunk, peer_u.shape[1]).transpose(0, 2, 1).astype(BF16)
    lw['peer_v'] = peer_v.astype(BF16)
    return lw


def _layer(x, mods, lw, gains, rope, ctx):
    b, t, d = x.shape
    m = b * t
    x2 = x.reshape(m, d)
    g_norm1, g_norm2, g_q_a, g_kv_a, g_gqa_q, g_gqa_k = gains
    q_rank, kv_rank = g_q_a.shape[0], g_kv_a.shape[0]
    nq, nk = GQA_HEADS * HEAD_DIM, GQA_KV_HEADS * HEAD_DIM
    col_gate_mla = q_rank + kv_rank + nq + 2 * nk

    rpb = t if mods.shape[0] == b else m
    h = _norm_mod(x2, g_norm1, mods, rpb, SHIFT1, SCALE1)
    proj = _mm(h, lw['w_in'], F32)
    gates = _mm(h, lw['w_gates'], BF16, sigmoid=True)
    kr = _mm(h, lw['w_kr'], F32)
    prep_rope = None if rope is None else rope[:4]
    qa, ckv32, ckvx, gq, gk32, gk, gv = _prep(proj, kr, g_q_a, g_kv_a, g_gqa_q, g_gqa_k, prep_rope, t)

    mla_scale = (MLA_NOPE + MLA_ROPE) ** -0.5 * LOG2E
    if rope is None:
        q_mla = _mm(qa, lw['w_q_b'], BF16, scale=mla_scale)
    else:
        q_mla = _mm(qa, lw['w_q_b'], BF16, scale=mla_scale, rope=rope[4:], rows_per_batch=t)

    ckvx = ckvx.reshape(b, t, -1)
    gk_all = gk.reshape(b, t, nk)
    gv_all = gv.reshape(b, t, nk)
    if ctx is not None:
        c_ckv, c_kr, c_k, c_v = ctx
        past = c_ckv.shape[1]
        c_x = jnp.concatenate([c_ckv, c_kr, jnp.zeros((b, past, LANE - MLA_ROPE), F32)], axis=-1).astype(BF16)
        ckvx = jnp.concatenate([ckvx, c_x], axis=1)
        gk_all = jnp.concatenate([gk_all, c_k.reshape(b, past, nk).astype(BF16)], axis=1)
        gv_all = jnp.concatenate([gv_all, c_v.reshape(b, past, nk).astype(BF16)], axis=1)
    s_len = ckvx.shape[1]
    kv = _mm(ckvx.reshape(b * s_len, -1), lw['w_kv'], BF16).reshape(b, s_len, -1)

    o_mla = _attention(q_mla.reshape(b, t, -1), kv, kv, MLA_HEADS, 1, MLA_QPAD, MLA_V,
                       0, MLA_HEADS * MLA_QPAD, tq=2048)
    o_gqa = _attention(gq.reshape(b, t, -1), gk_all, gv_all, GQA_KV_HEADS, GQA_HEADS // GQA_KV_HEADS,
                       HEAD_DIM, HEAD_DIM, 0, 0, tq=512)
    merged = _merge(o_mla.reshape(m, -1), o_gqa.reshape(m, -1), lw['w_mla_o'], lw['w_gqa_o'],
                    gates)
    x1 = _mm_residual(merged, lw['w_out'], x2, mods, rpb, GATE1)

    h2 = _norm_mod(x1, g_norm2, mods, rpb, SHIFT2, SCALE2)
    qp = _mm(h2, lw['w_peer_q'], BF16)
    idx_i, idx_j, gates = _route(qp, lw['sub_keys'])
    wmap = _wbuild(idx_i, idx_j, gates)
    po = _peer_mix(h2, lw['peer_ut'], lw['peer_v'], wmap)
    own = (ckv32.reshape(b, t, kv_rank), kr[:, :MLA_ROPE].reshape(b, t, MLA_ROPE),
           gk32.reshape(b, t, GQA_KV_HEADS, HEAD_DIM),
           proj[:, col_gate_mla - nk:col_gate_mla].reshape(b, t, GQA_KV_HEADS, HEAD_DIM))
    return x1, po, own


def kernel(x_prompt, x_sample, c, cache_mla_ckv, cache_mla_krope, cache_gqa_k, cache_gqa_v, c_ctx, w_mod, b_mod, g_norm1, g_norm2, w_in, g_q_a, w_q_b, g_kv_a, w_kv_b, g_gqa_q, g_gqa_k, w_mla_o, w_gqa_o, w_out, w_peer_q, peer_sub_keys, peer_u, peer_v, g_final):
    depth = w_in.shape[0]
    d = x_prompt.shape[-1]
    bc, tc = x_prompt.shape[:2]
    bl, tl = x_sample.shape[:2]

    cg, sg = _rope_tables(tl, HEAD_DIM)
    cm, sm = _rope_tables(tl, MLA_ROPE)
    one = jnp.ones((tl, LANE - MLA_ROPE), F32)
    zero = jnp.zeros((tl, LANE - MLA_ROPE), F32)
    cm_k, sm_k = jnp.concatenate([cm, one], 1), jnp.concatenate([sm, zero], 1)
    ones_n, zeros_n = jnp.ones((tl, MLA_NOPE), F32), jnp.zeros((tl, MLA_NOPE), F32)
    cq = jnp.tile(jnp.concatenate([ones_n, cm, one], 1), (1, 2))
    sq = jnp.tile(jnp.concatenate([zeros_n, sm, zero], 1), (1, 2))
    rope = (cg, sg, cm_k, sm_k, cq, sq)

    n_rows = bl + 1
    pad_rows = -n_rows % 8
    cvecs = jnp.concatenate([c, c_ctx[None, :], jnp.zeros((pad_rows, d), F32)], axis=0)

    xc, xl = x_prompt, x_sample
    new = [[], [], [], []]
    for l in range(depth):
        lw = _layer_weights(w_in[l], w_q_b[l], w_kv_b[l], w_mla_o[l], w_gqa_o[l], w_out[l],
                            w_peer_q[l], peer_sub_keys[l], peer_u[l], peer_v[l])
        gains = (g_norm1[l], g_norm2[l], g_q_a[l], g_kv_a[l], g_gqa_q[l], g_gqa_k[l])
        mod = _modulation(cvecs, w_mod[l], b_mod[l]).reshape(n_rows + pad_rows, N_MOD, d)
        mod = jnp.pad(mod, ((0, 0), (0, 8 - N_MOD), (0, 0)))
        mods_lat, mods_ctx = mod[:bl], mod[bl:bl + 1]
        last = l == depth - 1

        x1, po, own = _layer(xc, mods_ctx, lw, gains, None, None)
        for acc, o in zip(new, own):
            acc.append(o)
        xc = _finish(x1, po, mods_ctx, g_final, bc * tc, last).reshape(bc, tc, d)

        ctx = (cache_mla_ckv[:, l], cache_mla_krope[:, l], cache_gqa_k[:, l], cache_gqa_v[:, l])
        x1, po, _ = _layer(xl, mods_lat, lw, gains, rope, ctx)
        xl = _finish(x1, po, mods_lat, g_final, tl, last).reshape(bl, tl, d)

    return (xc, xl, jnp.stack(new[0], axis=1), jnp.stack(new[1], axis=1),
            jnp.stack(new[2], axis=1), jnp.stack(new[3], axis=1))


def _finish(x1, po, mods, g_final, rows_per_batch, last):
    d = x1.shape[1]
    g = g_final if last else None
    if g is None:
        raise NotImplementedError("only the final layer's residual is fused with the output norm")
    return _final(x1, po, mods, g, rows_per_batch, GATE2)
```

```python
import functools
import math

import jax
import jax.numpy as jnp
from jax import lax
from jax.experimental import pallas as pl
from jax.experimental.pallas import tpu as pltpu

F32 = jnp.float32
BF16 = jnp.bfloat16

GRID_W = 64
EPS = 1e-6
ROPE_BASE = 10000.0
N_MOD = 6
MLA_HEADS = 16
MLA_NOPE = 128
MLA_ROPE = 64
MLA_V = 128
MLA_QPAD = 256
GQA_HEADS = 16
GQA_KV_HEADS = 4
HEAD_DIM = 128
PEER_HEADS = 8
N_KEYS = 128
PEER_TOPK = 16
PEER_CHUNK_ROWS = 4
LANE = 128
SUBLANE = 8
VMEM_LIMIT = 56 * 1024 * 1024
INV_SQRT2 = 1.0 / math.sqrt(2.0)
LOG2E = 1.0 / math.log(2.0)

SHIFT1, SCALE1, GATE1, SHIFT2, SCALE2, GATE2 = range(6)


def _tile(dim, pref):
    t = min(dim, pref)
    assert dim % t == 0, (dim, pref)
    return t


def _params(*sem):
    return pltpu.CompilerParams(dimension_semantics=sem, vmem_limit_bytes=VMEM_LIMIT)


def _mod_kernel(c_ref, w_ref, b_ref, o_ref):
    c = c_ref[...]
    a = (c * jax.nn.sigmoid(c)).astype(BF16)
    o_ref[...] = jnp.dot(a, w_ref[...].astype(BF16), preferred_element_type=F32) + b_ref[...]


def _modulation(cvecs, w_mod, b_mod):
    r, d = cvecs.shape
    n = w_mod.shape[1]
    tn = _tile(n, 512)
    return pl.pallas_call(
        _mod_kernel,
        grid=(n // tn,),
        in_specs=[pl.BlockSpec((r, d), lambda j: (0, 0)),
                  pl.BlockSpec((d, tn), lambda j: (0, j)),
                  pl.BlockSpec((1, tn), lambda j: (0, j))],
        out_specs=pl.BlockSpec((r, tn), lambda j: (0, j)),
        out_shape=jax.ShapeDtypeStruct((r, n), F32),
        compiler_params=_params("parallel"),
        name="modulation",
    )(cvecs, w_mod, b_mod.reshape(1, n))


def _norm_mod_kernel(x_ref, g_ref, m_ref, o_ref, *, shift_row, scale_row):
    x = x_ref[...]
    y = x * lax.rsqrt(jnp.mean(x * x, axis=-1, keepdims=True) + EPS) * g_ref[...]
    m = m_ref[0]
    o_ref[...] = (y * (1.0 + m[scale_row:scale_row + 1]) + m[shift_row:shift_row + 1]).astype(BF16)


def _norm_mod(x, g, mods, rows_per_batch, shift_row, scale_row):
    m, d = x.shape
    tm = _tile(rows_per_batch, 256)
    per = rows_per_batch // tm
    return pl.pallas_call(
        functools.partial(_norm_mod_kernel, shift_row=shift_row, scale_row=scale_row),
        grid=(m // tm,),
        in_specs=[pl.BlockSpec((tm, d), lambda i: (i, 0)),
                  pl.BlockSpec((1, d), lambda i: (0, 0)),
                  pl.BlockSpec((1, 8, d), lambda i: (i // per, 0, 0))],
        out_specs=pl.BlockSpec((tm, d), lambda i: (i, 0)),
        out_shape=jax.ShapeDtypeStruct((m, d), BF16),
        compiler_params=_params("parallel"),
        name="norm_mod",
    )(x, g.reshape(1, d), mods)


def _swap_pairs(x):
    ax = x.ndim - 1
    n = x.shape[ax]
    lane = lax.broadcasted_iota(jnp.int32, x.shape, ax)
    nxt = pltpu.roll(x, n - 1, axis=ax)
    prv = pltpu.roll(x, 1, axis=ax)
    return jnp.where((lane & 1) == 0, nxt, prv)


def _mm_kernel(a_ref, b_ref, o_ref, *, scale, sigmoid):
    acc = jnp.dot(a_ref[...], b_ref[...], preferred_element_type=F32)
    if scale is not None:
        acc = acc * scale
    if sigmoid:
        acc = jax.nn.sigmoid(acc)
    o_ref[...] = acc.astype(o_ref.dtype)


def _mm_rope_kernel(a_ref, b_ref, cos_ref, sin_ref, o_ref, *, scale):
    acc = jnp.dot(a_ref[...], b_ref[...], preferred_element_type=F32)
    for lo in range(0, acc.shape[1], MLA_QPAD):
        mid, hi = lo + MLA_NOPE, lo + MLA_QPAD
        o_ref[:, lo:mid] = (acc[:, lo:mid] * scale).astype(o_ref.dtype)
        r = acc[:, mid:hi]
        r = r * cos_ref[:, mid:hi] + _swap_pairs(r) * sin_ref[:, mid:hi]
        o_ref[:, mid:hi] = (r * scale).astype(o_ref.dtype)


def _mm(a, b, out_dtype, scale=None, sigmoid=False, rope=None, rows_per_batch=None, tm=1024, tn=512):
    m, k = a.shape
    n = b.shape[1]
    tm = _tile(m if rows_per_batch is None else rows_per_batch, tm)
    tn = _tile(n, tn)
    in_specs = [pl.BlockSpec((tm, k), lambda i, j: (i, 0)),
                pl.BlockSpec((k, tn), lambda i, j: (0, j))]
    args = [a, b]
    if rope is None:
        body = functools.partial(_mm_kernel, scale=scale, sigmoid=sigmoid)
    else:
        per = rows_per_batch // tm
        body = functools.partial(_mm_rope_kernel, scale=scale)
        in_specs += [pl.BlockSpec((tm, tn), lambda i, j: (i % per, 0))] * 2
        args += list(rope)
    return pl.pallas_call(
        body,
        grid=(m // tm, n // tn),
        in_specs=in_specs,
        out_specs=pl.BlockSpec((tm, tn), lambda i, j: (i, j)),
        out_shape=jax.ShapeDtypeStruct((m, n), out_dtype),
        compiler_params=_params("parallel", "parallel"),
        name="matmul",
    )(*args)


def _mm_res_kernel(a_ref, b_ref, x_ref, m_ref, o_ref, *, gate_row):
    acc = jnp.dot(a_ref[...], b_ref[...], preferred_element_type=F32)
    o_ref[...] = x_ref[...] + m_ref[0][gate_row:gate_row + 1] * acc


def _mm_residual(a, b, x, mods, rows_per_batch, gate_row, tm=1024, tn=512):
    m, k = a.shape
    n = b.shape[1]
    tm = _tile(rows_per_batch, tm)
    tn = _tile(n, tn)
    per = rows_per_batch // tm
    return pl.pallas_call(
        functools.partial(_mm_res_kernel, gate_row=gate_row),
        grid=(m // tm, n // tn),
        in_specs=[pl.BlockSpec((tm, k), lambda i, j: (i, 0)),
                  pl.BlockSpec((k, tn), lambda i, j: (0, j)),
                  pl.BlockSpec((tm, tn), lambda i, j: (i, j)),
                  pl.BlockSpec((1, 8, tn), lambda i, j: (i // per, 0, j))],
        out_specs=pl.BlockSpec((tm, tn), lambda i, j: (i, j)),
        out_shape=jax.ShapeDtypeStruct((m, n), F32),
        compiler_params=_params("parallel", "parallel"),
        name="matmul_residual",
    )(a, b, x, mods)


def _merge_kernel(oa_ref, ob_ref, wa_ref, wb_ref, ga_ref, gb_ref, o_ref):
    ya = jnp.dot(oa_ref[...], wa_ref[...], preferred_element_type=F32)
    yb = jnp.dot(ob_ref[...], wb_ref[...], preferred_element_type=F32)
    o_ref[...] = (ga_ref[...] * ya + gb_ref[...] * yb).astype(o_ref.dtype)


def _merge(o_mla, o_gqa, w_mla_o, w_gqa_o, gates, tm=1024, tn=512):
    m, k = o_mla.shape
    n = w_mla_o.shape[1]
    tm = _tile(m, tm)
    tn = _tile(n, tn)
    ca, cb = 0, n // tn
    return pl.pallas_call(
        _merge_kernel,
        grid=(m // tm, n // tn),
        in_specs=[pl.BlockSpec((tm, k), lambda i, j: (i, 0)),
                  pl.BlockSpec((tm, k), lambda i, j: (i, 0)),
                  pl.BlockSpec((k, tn), lambda i, j: (0, j)),
                  pl.BlockSpec((k, tn), lambda i, j: (0, j)),
                  pl.BlockSpec((tm, tn), lambda i, j: (i, ca + j)),
                  pl.BlockSpec((tm, tn), lambda i, j: (i, cb + j))],
        out_specs=pl.BlockSpec((tm, tn), lambda i, j: (i, j)),
        out_shape=jax.ShapeDtypeStruct((m, n), BF16),
        compiler_params=_params("parallel", "parallel"),
        name="merge",
    )(o_mla, o_gqa, w_mla_o, w_gqa_o, gates, gates)


def _rms(x, g):
    return x * lax.rsqrt(jnp.mean(x * x, axis=-1, keepdims=True) + EPS) * g


def _prep_kernel(*refs, q_rank, kv_rank, use_rope):
    if use_rope:
        (p_ref, kr_ref, gqa_ref, gkv_ref, ggq_ref, ggk_ref, cg_ref, sg_ref, cm_ref, sm_ref,
         qa_o, ckv_o, ckvx_o, gq_o, gk32_o, gk_o, gv_o) = refs
    else:
        (p_ref, kr_ref, gqa_ref, gkv_ref, ggq_ref, ggk_ref,
         qa_o, ckv_o, ckvx_o, gq_o, gk32_o, gk_o, gv_o) = refs
    c0 = q_rank
    c1 = c0 + kv_rank
    c2 = c1 + GQA_HEADS * HEAD_DIM
    c3 = c2 + GQA_KV_HEADS * HEAD_DIM
    qa_o[...] = _rms(p_ref[:, 0:c0], gqa_ref[...]).astype(BF16)
    ckv = _rms(p_ref[:, c0:c1], gkv_ref[...])
    ckv_o[...] = ckv
    ckvx_o[:, 0:kv_rank] = ckv.astype(BF16)
    kr = kr_ref[...]
    if use_rope:
        kr = kr * cm_ref[...] + _swap_pairs(kr) * sm_ref[...]
    ckvx_o[:, kv_rank:kv_rank + LANE] = kr.astype(BF16)
    q_scale = HEAD_DIM ** -0.5 * LOG2E
    for h in range(GQA_HEADS):
        x = _rms(p_ref[:, c1 + h * HEAD_DIM:c1 + (h + 1) * HEAD_DIM], ggq_ref[...])
        if use_rope:
            x = x * cg_ref[...] + _swap_pairs(x) * sg_ref[...]
        gq_o[:, h * HEAD_DIM:(h + 1) * HEAD_DIM] = (x * q_scale).astype(BF16)
    for h in range(GQA_KV_HEADS):
        x = _rms(p_ref[:, c2 + h * HEAD_DIM:c2 + (h + 1) * HEAD_DIM], ggk_ref[...])
        if use_rope:
            x = x * cg_ref[...] + _swap_pairs(x) * sg_ref[...]
        gk32_o[:, h * HEAD_DIM:(h + 1) * HEAD_DIM] = x
        gk_o[:, h * HEAD_DIM:(h + 1) * HEAD_DIM] = x.astype(BF16)
    gv_o[...] = p_ref[:, c3:c3 + GQA_KV_HEADS * HEAD_DIM].astype(BF16)


def _prep(proj, kr, g_q_a, g_kv_a, g_gqa_q, g_gqa_k, rope, rows_per_batch):
    m = proj.shape[0]
    q_rank, kv_rank = g_q_a.shape[0], g_kv_a.shape[0]
    width = q_rank + kv_rank + (GQA_HEADS + 2 * GQA_KV_HEADS) * HEAD_DIM
    tm = _tile(rows_per_batch, 256)
    per = rows_per_batch // tm
    nq, nk = GQA_HEADS * HEAD_DIM, GQA_KV_HEADS * HEAD_DIM
    row = lambda i: (i, 0)
    const = lambda i: (0, 0)
    in_specs = [pl.BlockSpec((tm, width), row), pl.BlockSpec((tm, LANE), row),
                pl.BlockSpec((1, q_rank), const), pl.BlockSpec((1, kv_rank), const),
                pl.BlockSpec((1, HEAD_DIM), const), pl.BlockSpec((1, HEAD_DIM), const)]
    args = [proj, kr, g_q_a.reshape(1, -1), g_kv_a.reshape(1, -1), g_gqa_q.reshape(1, -1), g_gqa_k.reshape(1, -1)]
    if rope is not None:
        in_specs += [pl.BlockSpec((tm, LANE), lambda i: (i % per, 0))] * 4
        args += list(rope)
    outs = [((m, q_rank), BF16), ((m, kv_rank), F32), ((m, kv_rank + LANE), BF16), ((m, nq), BF16),
            ((m, nk), F32), ((m, nk), BF16), ((m, nk), BF16)]
    return pl.pallas_call(
        functools.partial(_prep_kernel, q_rank=q_rank, kv_rank=kv_rank, use_rope=rope is not None),
        grid=(m // tm,),
        in_specs=in_specs,
        out_specs=[pl.BlockSpec((tm, s[1]), row) for s, _ in outs],
        out_shape=[jax.ShapeDtypeStruct(s, dt) for s, dt in outs],
        compiler_params=_params("parallel"),
        name="prep",
    )(*args)


def _attn_kernel(q_ref, *refs, rep, dq, dv, sub):
    o_ref = refs[-1]
    ks = [r[...] for r in refs[0:-1:2]]
    vs = [r[...] for r in refs[1:-1:2]]
    for r in range(rep):
        for i in range(q_ref.shape[0] // sub):
            rows = slice(i * sub, (i + 1) * sub)
            q = q_ref[rows, r * dq:(r + 1) * dq]
            ss = [lax.dot_general(q, k, (((1,), (1,)), ((), ())), preferred_element_type=F32) for k in ks]
            s = ss[0] if len(ss) == 1 else jnp.concatenate(ss, axis=1)
            p = jnp.exp2(s - jnp.max(s, axis=-1, keepdims=True))
            l = jnp.sum(p, axis=-1, keepdims=True)
            p = p.astype(BF16)
            o, off = None, 0
            for v in vs:
                c = jnp.dot(p[:, off:off + v.shape[0]], v, preferred_element_type=F32)
                o = c if o is None else o + c
                off += v.shape[0]
            o_ref[rows, r * dv:(r + 1) * dv] = (o / l).astype(o_ref.dtype)


def _attention(q, kvs, n_groups, rep, dq, dv, k_col0, v_col0, tq):
    b, t, _ = q.shape
    tq = _tile(t, tq)
    sub = _tile(tq, 256)
    kb, vb = k_col0 // dq, v_col0 // dv
    assert kb * dq == k_col0 and vb * dv == v_col0
    in_specs = [pl.BlockSpec((None, tq, rep * dq), lambda bi, g, qi: (bi, qi, g))]
    args = [q]
    for k, v in kvs:
        in_specs += [pl.BlockSpec((None, k.shape[1], dq), lambda bi, g, qi: (bi, 0, kb + g)),
                     pl.BlockSpec((None, v.shape[1], dv), lambda bi, g, qi: (bi, 0, vb + g))]
        args += [k, v]
    return pl.pallas_call(
        functools.partial(_attn_kernel, rep=rep, dq=dq, dv=dv, sub=sub),
        grid=(b, n_groups, t // tq),
        in_specs=in_specs,
        out_specs=pl.BlockSpec((None, tq, rep * dv), lambda bi, g, qi: (bi, qi, g)),
        out_shape=jax.ShapeDtypeStruct((b, t, n_groups * rep * dv), BF16),
        compiler_params=_params("parallel", "parallel", "parallel"),
        name="attention",
    )(*args)


def _top16(s, idx):
    n, c = s.shape
    row = lax.broadcasted_iota(jnp.int32, (PEER_TOPK, c), 0)
    big = float(PEER_TOPK * PEER_TOPK)

    def body(r, carry):
        s, vals, idxs = carry
        mx = jnp.max(s, axis=0, keepdims=True)
        am = jnp.min(jnp.where(s == mx, idx, big), axis=0, keepdims=True)
        vals = jnp.where(row == r, mx, vals)
        idxs = jnp.where(row == r, am, idxs)
        s = jnp.where(idx == am, -jnp.inf, s)
        return s, vals, idxs

    init = (s, jnp.zeros((PEER_TOPK, c), F32), jnp.zeros((PEER_TOPK, c), F32))
    _, vals, idxs = lax.fori_loop(0, PEER_TOPK, body, init)
    return vals, idxs


def _pick(table, pos):
    out = jnp.zeros(pos.shape, F32)
    for a in range(PEER_TOPK):
        out = jnp.where(pos == a, table[a:a + 1], out)
    return out


_CAND_ROWS = [(0, 0), (0, 8), (1, 0), (2, 0), (3, 0), (4, 0), (5, 0), (6, 0), (7, 0)]


def _route_kernel(q_ref, k_ref, oi_ref, oj_ref, og_ref):
    q = q_ref[...]
    tn = q.shape[0]
    half = q.shape[1] // 2
    nt = (((1,), (1,)), ((), ()))
    s0 = lax.dot_general(k_ref[0], q[:, :half], nt, preferred_element_type=F32)
    s1 = lax.dot_general(k_ref[1], q[:, half:], nt, preferred_element_type=F32)
    s01 = jnp.concatenate([s0, s1], axis=1)
    v01, i01 = _top16(s01, lax.broadcasted_iota(jnp.int32, s01.shape, 0).astype(F32))
    v0, v1, i0, i1 = v01[:, :tn], v01[:, tn:], i01[:, :tn], i01[:, tn:]
    sub = lax.broadcasted_iota(jnp.int32, (SUBLANE, tn), 0).astype(F32)
    cand = [v0[a:a + 1] + v1[b:b + SUBLANE] for a, b in _CAND_ROWS]
    flat = [sub + float(a * PEER_TOPK + b) for a, b in _CAND_ROWS]
    cand.append(v0[SUBLANE:] + v1[0:1])
    flat.append((sub + float(SUBLANE)) * float(PEER_TOPK))
    best, pos = _top16(jnp.concatenate(cand, axis=0), jnp.concatenate(flat, axis=0))
    pos = pos.astype(jnp.int32)
    e = jnp.exp(best - best[0:1])
    oi_ref[...] = _pick(i0, pos >> 4)
    oj_ref[...] = _pick(i1, pos & (PEER_TOPK - 1))
    og_ref[...] = e / jnp.sum(e, axis=0, keepdims=True)


def _route(qp, sub_keys, tn=512):
    m = qp.shape[0]
    tn = _tile(m, tn)
    qd = qp.shape[1] // PEER_HEADS
    shp = jax.ShapeDtypeStruct((PEER_HEADS * PEER_TOPK, m), F32)
    spec = pl.BlockSpec((PEER_TOPK, tn), lambda t, h: (h, t))
    return pl.pallas_call(
        _route_kernel,
        grid=(m // tn, PEER_HEADS),
        in_specs=[pl.BlockSpec((tn, qd), lambda t, h: (t, h)),
                  pl.BlockSpec(sub_keys.shape, lambda t, h: (0, 0, 0))],
        out_specs=[spec, spec, spec],
        out_shape=[shp, shp, shp],
        compiler_params=_params("parallel", "parallel"),
        name="peer_route",
    )(qp, sub_keys)


def _wbuild_kernel(i_ref, j_ref, g_ref, o_ref, it_s, jt_s, gt_s):
    it_s[...] = i_ref[...].T
    jt_s[...] = j_ref[...].T
    gt_s[...] = g_ref[...].T
    ne = i_ref.shape[0]
    iota_a = lax.broadcasted_iota(jnp.int32, (2 * N_KEYS, 2 * ne), 0).astype(F32).astype(BF16)
    iota_g = lax.broadcasted_iota(jnp.int32, (N_KEYS, 2 * ne), 0).astype(F32).astype(BF16)
    one, zero = jnp.ones((), BF16), jnp.zeros((), BF16)

    def body(m, carry):
        ri = it_s[pl.ds(2 * m, 2), :]
        rj = jt_s[pl.ds(2 * m, 2), :]
        rg = gt_s[pl.ds(2 * m, 2), :]
        row_i = jnp.concatenate([ri[0:1], ri[1:2] + float(N_KEYS)], axis=1).astype(BF16)
        row_j = jnp.concatenate([rj[0:1], rj[1:2]], axis=1).astype(BF16)
        row_g = jnp.concatenate([rg[0:1], rg[1:2]], axis=1).astype(BF16)
        a2 = jnp.where(iota_a == row_i, one, zero)
        g2 = jnp.where(iota_g == row_j, row_g, zero)
        w = lax.dot_general(a2, g2, (((1,), (1,)), ((), ())), preferred_element_type=F32)
        for t in range(2):
            o_ref[:, 2 * m + t] = w[t * N_KEYS:(t + 1) * N_KEYS].reshape(N_KEYS // SUBLANE, SUBLANE, N_KEYS)
        return carry

    lax.fori_loop(0, o_ref.shape[1] // 2, body, 0, unroll=16)


def _wbuild(idx_i, idx_j, gates):
    ne, m = idx_i.shape
    tw = LANE
    spec = pl.BlockSpec((ne, tw), lambda t: (0, t))
    return pl.pallas_call(
        _wbuild_kernel,
        grid=(m // tw,),
        in_specs=[spec, spec, spec],
        out_specs=pl.BlockSpec((N_KEYS // SUBLANE, tw, SUBLANE, N_KEYS), lambda t: (0, t, 0, 0)),
        out_shape=jax.ShapeDtypeStruct((N_KEYS // SUBLANE, m, SUBLANE, N_KEYS), F32),
        scratch_shapes=[pltpu.VMEM((tw, ne), F32)] * 3,
        compiler_params=_params("parallel"),
        name="peer_gate_map",
    )(idx_i, idx_j, gates)


def _peer_kernel(x_ref, ut_ref, v_ref, w_ref, o_ref, ga_s, gb_s, *, rows, n_chunks):
    k = pl.program_id(1)
    base = (k % (SUBLANE // rows)) * rows

    def gates(dst):
        s = jnp.dot(x_ref[...], ut_ref[...], preferred_element_type=F32)
        for il in range(rows):
            wg = w_ref[:, base + il, :]
            sl = s[:, il * N_KEYS:(il + 1) * N_KEYS]
            g = 0.5 * sl * (1.0 + lax.erf(sl * INV_SQRT2)) * wg
            dst[:, il * N_KEYS:(il + 1) * N_KEYS] = g.astype(BF16)

    def mix(src):
        return jnp.dot(src[...], v_ref[...], preferred_element_type=F32)

    inner = jnp.logical_and(k > 0, k < n_chunks)

    @pl.when(k == 0)
    def _():
        o_ref[...] = jnp.zeros_like(o_ref)
        gates(ga_s)

    @pl.when(jnp.logical_and(inner, k % 2 == 1))
    def _():
        gates(gb_s)
        o_ref[...] += mix(ga_s)

    @pl.when(jnp.logical_and(inner, k % 2 == 0))
    def _():
        gates(ga_s)
        o_ref[...] += mix(gb_s)

    @pl.when(k == n_chunks)
    def _():
        o_ref[...] += mix(gb_s if n_chunks % 2 == 0 else ga_s)


def _peer_mix(h2, u_t, v, wmap, tn=1024):
    m, d = h2.shape
    n_exp = v.shape[0]
    tn = _tile(m, tn)
    rows = PEER_CHUNK_ROWS
    ec = rows * N_KEYS
    n_chunks = n_exp // ec
    assert u_t.shape == (n_chunks, d, ec)
    per = SUBLANE // rows
    last = n_chunks - 1
    return pl.pallas_call(
        functools.partial(_peer_kernel, rows=rows, n_chunks=n_chunks),
        grid=(m // tn, n_chunks + 1),
        in_specs=[pl.BlockSpec((tn, d), lambda t, k: (t, 0), pipeline_mode=pl.Buffered(1)),
                  pl.BlockSpec((None, d, ec), lambda t, k: (jnp.minimum(k, last), 0, 0)),
                  pl.BlockSpec((ec, d), lambda t, k: (jnp.maximum(k - 1, 0), 0)),
                  pl.BlockSpec((None, tn, SUBLANE, N_KEYS), lambda t, k: (jnp.minimum(k, last) // per, t, 0, 0))],
        out_specs=pl.BlockSpec((tn, d), lambda t, k: (t, 0), pipeline_mode=pl.Buffered(1)),
        out_shape=jax.ShapeDtypeStruct((m, d), F32),
        scratch_shapes=[pltpu.VMEM((tn, ec), BF16), pltpu.VMEM((tn, ec), BF16)],
        compiler_params=_params("parallel", "arbitrary"),
        name="peer_mix",
    )(h2, u_t, v, wmap)


def _final_kernel(x_ref, p_ref, m_ref, g_ref, o_ref, *, gate_row):
    x = x_ref[...] + m_ref[0][gate_row:gate_row + 1] * p_ref[...]
    o_ref[...] = _rms(x, g_ref[...])


def _final(x, po, mods, g_final, rows_per_batch, gate_row):
    m, d = x.shape
    tm = _tile(rows_per_batch, 256)
    per = rows_per_batch // tm
    row = lambda i: (i, 0)
    return pl.pallas_call(
        functools.partial(_final_kernel, gate_row=gate_row),
        grid=(m // tm,),
        in_specs=[pl.BlockSpec((tm, d), row), pl.BlockSpec((tm, d), row),
                  pl.BlockSpec((1, 8, d), lambda i: (i // per, 0, 0)),
                  pl.BlockSpec((1, d), lambda i: (0, 0))],
        out_specs=pl.BlockSpec((tm, d), row),
        out_shape=jax.ShapeDtypeStruct((m, d), F32),
        compiler_params=_params("parallel"),
        name="final_norm",
    )(x, po, mods, g_final.reshape(1, d))


def _rope_tables(n_tokens, dim):
    n_rows = n_tokens // GRID_W
    rows = jnp.broadcast_to(jnp.arange(n_rows)[:, None], (n_rows, GRID_W)).reshape(-1).astype(F32)
    cols = jnp.broadcast_to(jnp.arange(GRID_W)[None, :], (n_rows, GRID_W)).reshape(-1).astype(F32)
    n_freq = dim // 4
    freqs = ROPE_BASE ** (-jnp.arange(n_freq, dtype=F32) / n_freq)
    ang = jnp.concatenate([rows[:, None] * freqs, cols[:, None] * freqs], axis=-1)
    cos, sin = jnp.cos(ang), jnp.sin(ang)
    cos2 = jnp.repeat(cos, 2, axis=-1)
    sin2 = jnp.stack([-sin, sin], axis=-1).reshape(n_tokens, dim)
    return cos2, sin2


def _layer_weights(w_in, w_q_b, w_kv_b, w_mla_o, w_gqa_o, w_out, w_peer_q, sub_keys, peer_u, peer_v):
    q_rank, kv_rank = w_q_b.shape[0], w_kv_b.shape[0]
    k0 = q_rank + kv_rank
    lw = {}
    g0 = k0 + MLA_ROPE + (GQA_HEADS + 2 * GQA_KV_HEADS) * HEAD_DIM
    lw['w_in'] = jnp.concatenate([w_in[:, :k0], w_in[:, k0 + MLA_ROPE:g0]], axis=1).astype(BF16)
    lw['w_gates'] = w_in[:, g0:].astype(BF16)
    lw['w_kr'] = jnp.pad(w_in[:, k0:k0 + MLA_ROPE], ((0, 0), (0, LANE - MLA_ROPE))).astype(BF16)
    wq = w_q_b.reshape(q_rank, MLA_HEADS, MLA_NOPE + MLA_ROPE)
    wq = jnp.pad(wq, ((0, 0), (0, 0), (0, MLA_QPAD - MLA_NOPE - MLA_ROPE)))
    lw['w_q_b'] = wq.reshape(q_rank, MLA_HEADS * MLA_QPAD).astype(BF16)
    wkv = w_kv_b.reshape(kv_rank, MLA_HEADS, MLA_NOPE + MLA_V)
    wk = jnp.zeros((kv_rank + LANE, MLA_HEADS, MLA_QPAD), F32)
    wk = wk.at[:kv_rank, :, :MLA_NOPE].set(wkv[:, :, :MLA_NOPE])
    eye = jnp.broadcast_to(jnp.eye(MLA_ROPE, dtype=F32)[:, None, :], (MLA_ROPE, MLA_HEADS, MLA_ROPE))
    wk = wk.at[kv_rank:kv_rank + MLA_ROPE, :, MLA_NOPE:MLA_NOPE + MLA_ROPE].set(eye)
    wv = jnp.pad(wkv[:, :, MLA_NOPE:], ((0, LANE), (0, 0), (0, 0)))
    lw['w_kv'] = jnp.concatenate([wk.reshape(kv_rank + LANE, -1), wv.reshape(kv_rank + LANE, -1)],
                                 axis=1).astype(BF16)
    lw['w_mla_o'] = w_mla_o.astype(BF16)
    lw['w_gqa_o'] = w_gqa_o.astype(BF16)
    lw['w_out'] = w_out.astype(BF16)
    lw['w_peer_q'] = w_peer_q.astype(BF16)
    lw['sub_keys'] = sub_keys.astype(BF16)
    chunk = PEER_CHUNK_ROWS * N_KEYS
    lw['peer_ut'] = peer_u.reshape(-1, chunk, peer_u.shape[1]).transpose(0, 2, 1).astype(BF16)
    lw['peer_v'] = peer_v.astype(BF16)
    return lw


def _layer(x, mods, lw, gains, rope, ctx):
    b, t, d = x.shape
    m = b * t
    x2 = x.reshape(m, d)
    g_norm1, g_norm2, g_q_a, g_kv_a, g_gqa_q, g_gqa_k = gains
    q_rank, kv_rank = g_q_a.shape[0], g_kv_a.shape[0]
    nq, nk = GQA_HEADS * HEAD_DIM, GQA_KV_HEADS * HEAD_DIM
    col_gate_mla = q_rank + kv_rank + nq + 2 * nk

    rpb = t if mods.shape[0] == b else m
    h = _norm_mod(x2, g_norm1, mods, rpb, SHIFT1, SCALE1)
    proj = _mm(h, lw['w_in'], F32)
    gates = _mm(h, lw['w_gates'], BF16, sigmoid=True)
    kr = _mm(h, lw['w_kr'], F32)
    prep_rope = None if rope is None else rope[:4]
    qa, ckv32, ckvx, gq, gk32, gk, gv = _prep(proj, kr, g_q_a, g_kv_a, g_gqa_q, g_gqa_k, prep_rope, t)

    mla_scale = (MLA_NOPE + MLA_ROPE) ** -0.5 * LOG2E
    if rope is None:
        q_mla = _mm(qa, lw['w_q_b'], BF16, scale=mla_scale)
    else:
        q_mla = _mm(qa, lw['w_q_b'], BF16, scale=mla_scale, rope=rope[4:], rows_per_batch=t)

    kv = _mm(ckvx, lw['w_kv'], BF16, tn=2048).reshape(b, t, -1)
    mla_kv = [(kv, kv)]
    gqa_kv = [(gk.reshape(b, t, nk), gv.reshape(b, t, nk))]
    if ctx is not None:
        c_ckv, c_kr, c_k, c_v = ctx
        past = c_ckv.shape[1]
        c_x = jnp.concatenate([c_ckv, c_kr, jnp.zeros((b, past, LANE - MLA_ROPE), F32)], axis=-1).astype(BF16)
        c_kv = _mm(c_x.reshape(b * past, -1), lw['w_kv'], BF16, tn=2048).reshape(b, past, -1)
        mla_kv.append((c_kv, c_kv))
        gqa_kv.append((c_k.reshape(b, past, nk).astype(BF16), c_v.reshape(b, past, nk).astype(BF16)))

    o_mla = _attention(q_mla.reshape(b, t, -1), mla_kv, MLA_HEADS, 1, MLA_QPAD, MLA_V,
                       0, MLA_HEADS * MLA_QPAD, tq=2048)
    o_gqa = _attention(gq.reshape(b, t, -1), gqa_kv, GQA_KV_HEADS, GQA_HEADS // GQA_KV_HEADS,
                       HEAD_DIM, HEAD_DIM, 0, 0, tq=1024)
    merged = _merge(o_mla.reshape(m, -1), o_gqa.reshape(m, -1), lw['w_mla_o'], lw['w_gqa_o'],
                    gates)
    x1 = _mm_residual(merged, lw['w_out'], x2, mods, rpb, GATE1)

    h2 = _norm_mod(x1, g_norm2, mods, rpb, SHIFT2, SCALE2)
    qp = _mm(h2, lw['w_peer_q'], BF16)
    idx_i, idx_j, gates = _route(qp, lw['sub_keys'])
    wmap = _wbuild(idx_i, idx_j, gates)
    po = _peer_mix(h2, lw['peer_ut'], lw['peer_v'], wmap)
    own = (ckv32.reshape(b, t, kv_rank), kr[:, :MLA_ROPE].reshape(b, t, MLA_ROPE),
           gk32.reshape(b, t, GQA_KV_HEADS, HEAD_DIM),
           proj[:, col_gate_mla - nk:col_gate_mla].reshape(b, t, GQA_KV_HEADS, HEAD_DIM))
    return x1, po, own


def kernel(x_prompt, x_sample, c, cache_mla_ckv, cache_mla_krope, cache_gqa_k, cache_gqa_v, c_ctx, w_mod, b_mod, g_norm1, g_norm2, w_in, g_q_a, w_q_b, g_kv_a, w_kv_b, g_gqa_q, g_gqa_k, w_mla_o, w_gqa_o, w_out, w_peer_q, peer_sub_keys, peer_u, peer_v, g_final):
    depth = w_in.shape[0]
    d = x_prompt.shape[-1]
    bc, tc = x_prompt.shape[:2]
    bl, tl = x_sample.shape[:2]

    cg, sg = _rope_tables(tl, HEAD_DIM)
    cm, sm = _rope_tables(tl, MLA_ROPE)
    one = jnp.ones((tl, LANE - MLA_ROPE), F32)
    zero = jnp.zeros((tl, LANE - MLA_ROPE), F32)
    cm_k, sm_k = jnp.concatenate([cm, one], 1), jnp.concatenate([sm, zero], 1)
    ones_n, zeros_n = jnp.ones((tl, MLA_NOPE), F32), jnp.zeros((tl, MLA_NOPE), F32)
    cq = jnp.tile(jnp.concatenate([ones_n, cm, one], 1), (1, 2))
    sq = jnp.tile(jnp.concatenate([zeros_n, sm, zero], 1), (1, 2))
    rope = (cg, sg, cm_k, sm_k, cq, sq)

    n_rows = bl + 1
    pad_rows = -n_rows % 8
    cvecs = jnp.concatenate([c, c_ctx[None, :], jnp.zeros((pad_rows, d), F32)], axis=0)

    xc, xl = x_prompt, x_sample
    new = [[], [], [], []]
    for l in range(depth):
        lw = _layer_weights(w_in[l], w_q_b[l], w_kv_b[l], w_mla_o[l], w_gqa_o[l], w_out[l],
                            w_peer_q[l], peer_sub_keys[l], peer_u[l], peer_v[l])
        gains = (g_norm1[l], g_norm2[l], g_q_a[l], g_kv_a[l], g_gqa_q[l], g_gqa_k[l])
        mod = _modulation(cvecs, w_mod[l], b_mod[l]).reshape(n_rows + pad_rows, N_MOD, d)
        mod = jnp.pad(mod, ((0, 0), (0, 8 - N_MOD), (0, 0)))
        mods_lat, mods_ctx = mod[:bl], mod[bl:bl + 1]
        last = l == depth - 1

        x1, po, own = _layer(xc, mods_ctx, lw, gains, None, None)
        for acc, o in zip(new, own):
            acc.append(o)
        xc = _finish(x1, po, mods_ctx, g_final, bc * tc, last).reshape(bc, tc, d)

        ctx = (cache_mla_ckv[:, l], cache_mla_krope[:, l], cache_gqa_k[:, l], cache_gqa_v[:, l])
        x1, po, _ = _layer(xl, mods_lat, lw, gains, rope, ctx)
        xl = _finish(x1, po, mods_lat, g_final, tl, last).reshape(bl, tl, d)

    return (xc, xl, jnp.stack(new[0], axis=1), jnp.stack(new[1], axis=1),
            jnp.stack(new[2], axis=1), jnp.stack(new[3], axis=1))


def _finish(x1, po, mods, g_final, rows_per_batch, last):
    d = x1.shape[1]
    g = g_final if last else None
    if g is None:
        raise NotImplementedError("only the final layer's residual is fused with the output norm")
    return _final(x1, po, mods, g, rows_per_batch, GATE2)
```

```python
import functools
import math

import jax
import jax.numpy as jnp
from jax import lax
from jax.experimental import pallas as pl
from jax.experimental.pallas import tpu as pltpu

F32 = jnp.float32
BF16 = jnp.bfloat16

GRID_W = 64
EPS = 1e-6
ROPE_BASE = 10000.0
N_MOD = 6
MLA_HEADS = 16
MLA_NOPE = 128
MLA_ROPE = 64
MLA_V = 128
MLA_QPAD = 256
GQA_HEADS = 16
GQA_KV_HEADS = 4
HEAD_DIM = 128
PEER_HEADS = 8
N_KEYS = 128
PEER_TOPK = 16
PEER_CHUNK_ROWS = 4
LANE = 128
SUBLANE = 8
VMEM_LIMIT = 56 * 1024 * 1024
INV_SQRT2 = 1.0 / math.sqrt(2.0)
LOG2E = 1.0 / math.log(2.0)

SHIFT1, SCALE1, GATE1, SHIFT2, SCALE2, GATE2 = range(6)


def _tile(dim, pref):
    t = min(dim, pref)
    assert dim % t == 0, (dim, pref)
    return t


def _params(*sem):
    return pltpu.CompilerParams(dimension_semantics=sem, vmem_limit_bytes=VMEM_LIMIT)


def _mod_kernel(c_ref, w_ref, b_ref, o_ref):
    c = c_ref[...]
    a = (c * jax.nn.sigmoid(c)).astype(BF16)
    o_ref[...] = jnp.dot(a, w_ref[...].astype(BF16), preferred_element_type=F32) + b_ref[...]


def _modulation(cvecs, w_mod, b_mod):
    r, d = cvecs.shape
    n = w_mod.shape[1]
    tn = _tile(n, 512)
    return pl.pallas_call(
        _mod_kernel,
        grid=(n // tn,),
        in_specs=[pl.BlockSpec((r, d), lambda j: (0, 0)),
                  pl.BlockSpec((d, tn), lambda j: (0, j)),
                  pl.BlockSpec((1, tn), lambda j: (0, j))],
        out_specs=pl.BlockSpec((r, tn), lambda j: (0, j)),
        out_shape=jax.ShapeDtypeStruct((r, n), F32),
        compiler_params=_params("parallel"),
        name="modulation",
    )(cvecs, w_mod, b_mod.reshape(1, n))


def _norm_mod_kernel(x_ref, g_ref, m_ref, o_ref, *, shift_row, scale_row):
    x = x_ref[...]
    y = x * lax.rsqrt(jnp.mean(x * x, axis=-1, keepdims=True) + EPS) * g_ref[...]
    m = m_ref[0]
    o_ref[...] = (y * (1.0 + m[scale_row:scale_row + 1]) + m[shift_row:shift_row + 1]).astype(BF16)


def _norm_mod(x, g, mods, rows_per_batch, shift_row, scale_row):
    m, d = x.shape
    tm = _tile(rows_per_batch, 256)
    per = rows_per_batch // tm
    return pl.pallas_call(
        functools.partial(_norm_mod_kernel, shift_row=shift_row, scale_row=scale_row),
        grid=(m // tm,),
        in_specs=[pl.BlockSpec((tm, d), lambda i: (i, 0)),
                  pl.BlockSpec((1, d), lambda i: (0, 0)),
                  pl.BlockSpec((1, 8, d), lambda i: (i // per, 0, 0))],
        out_specs=pl.BlockSpec((tm, d), lambda i: (i, 0)),
        out_shape=jax.ShapeDtypeStruct((m, d), BF16),
        compiler_params=_params("parallel"),
        name="norm_mod",
    )(x, g.reshape(1, d), mods)


def _swap_pairs(x):
    ax = x.ndim - 1
    n = x.shape[ax]
    lane = lax.broadcasted_iota(jnp.int32, x.shape, ax)
    nxt = pltpu.roll(x, n - 1, axis=ax)
    prv = pltpu.roll(x, 1, axis=ax)
    return jnp.where((lane & 1) == 0, nxt, prv)


def _mm_kernel(a_ref, b_ref, o_ref, *, scale, sigmoid):
    acc = jnp.dot(a_ref[...], b_ref[...], preferred_element_type=F32)
    if scale is not None:
        acc = acc * scale
    if sigmoid:
        acc = jax.nn.sigmoid(acc)
    o_ref[...] = acc.astype(o_ref.dtype)


def _mm_rope_kernel(a_ref, b_ref, cos_ref, sin_ref, o_ref, *, scale):
    acc = jnp.dot(a_ref[...], b_ref[...], preferred_element_type=F32)
    for lo in range(0, acc.shape[1], MLA_QPAD):
        mid, hi = lo + MLA_NOPE, lo + MLA_QPAD
        o_ref[:, lo:mid] = (acc[:, lo:mid] * scale).astype(o_ref.dtype)
        r = acc[:, mid:hi]
        r = r * cos_ref[:, mid:hi] + _swap_pairs(r) * sin_ref[:, mid:hi]
        o_ref[:, mid:hi] = (r * scale).astype(o_ref.dtype)


def _mm(a, b, out_dtype, scale=None, sigmoid=False, rope=None, rows_per_batch=None, tm=1024, tn=512):
    m, k = a.shape
    n = b.shape[1]
    tm = _tile(m if rows_per_batch is None else rows_per_batch, tm)
    tn = _tile(n, tn)
    in_specs = [pl.BlockSpec((tm, k), lambda i, j: (i, 0)),
                pl.BlockSpec((k, tn), lambda i, j: (0, j))]
    args = [a, b]
    if rope is None:
        body = functools.partial(_mm_kernel, scale=scale, sigmoid=sigmoid)
    else:
        per = rows_per_batch // tm
        body = functools.partial(_mm_rope_kernel, scale=scale)
        in_specs += [pl.BlockSpec((tm, tn), lambda i, j: (i % per, 0))] * 2
        args += list(rope)
    return pl.pallas_call(
        body,
        grid=(m // tm, n // tn),
        in_specs=in_specs,
        out_specs=pl.BlockSpec((tm, tn), lambda i, j: (i, j)),
        out_shape=jax.ShapeDtypeStruct((m, n), out_dtype),
        compiler_params=_params("parallel", "parallel"),
        name="matmul",
    )(*args)


def _mm_res_kernel(a_ref, b_ref, x_ref, m_ref, o_ref, *, gate_row):
    acc = jnp.dot(a_ref[...], b_ref[...], preferred_element_type=F32)
    o_ref[...] = x_ref[...] + m_ref[0][gate_row:gate_row + 1] * acc


def _mm_residual(a, b, x, mods, rows_per_batch, gate_row, tm=1024, tn=512):
    m, k = a.shape
    n = b.shape[1]
    tm = _tile(rows_per_batch, tm)
    tn = _tile(n, tn)
    per = rows_per_batch // tm
    return pl.pallas_call(
        functools.partial(_mm_res_kernel, gate_row=gate_row),
        grid=(m // tm, n // tn),
        in_specs=[pl.BlockSpec((tm, k), lambda i, j: (i, 0)),
                  pl.BlockSpec((k, tn), lambda i, j: (0, j)),
                  pl.BlockSpec((tm, tn), lambda i, j: (i, j)),
                  pl.BlockSpec((1, 8, tn), lambda i, j: (i // per, 0, j))],
        out_specs=pl.BlockSpec((tm, tn), lambda i, j: (i, j)),
        out_shape=jax.ShapeDtypeStruct((m, n), F32),
        compiler_params=_params("parallel", "parallel"),
        name="matmul_residual",
    )(a, b, x, mods)


def _merge_kernel(oa_ref, ob_ref, wa_ref, wb_ref, ga_ref, gb_ref, o_ref):
    ya = jnp.dot(oa_ref[...], wa_ref[...], preferred_element_type=F32)
    yb = jnp.dot(ob_ref[...], wb_ref[...], preferred_element_type=F32)
    o_ref[...] = (ga_ref[...] * ya + gb_ref[...] * yb).astype(o_ref.dtype)


def _merge(o_mla, o_gqa, w_mla_o, w_gqa_o, gates, tm=1024, tn=512):
    m, k = o_mla.shape
    n = w_mla_o.shape[1]
    tm = _tile(m, tm)
    tn = _tile(n, tn)
    ca, cb = 0, n // tn
    return pl.pallas_call(
        _merge_kernel,
        grid=(m // tm, n // tn),
        in_specs=[pl.BlockSpec((tm, k), lambda i, j: (i, 0)),
                  pl.BlockSpec((tm, k), lambda i, j: (i, 0)),
                  pl.BlockSpec((k, tn), lambda i, j: (0, j)),
                  pl.BlockSpec((k, tn), lambda i, j: (0, j)),
                  pl.BlockSpec((tm, tn), lambda i, j: (i, ca + j)),
                  pl.BlockSpec((tm, tn), lambda i, j: (i, cb + j))],
        out_specs=pl.BlockSpec((tm, tn), lambda i, j: (i, j)),
        out_shape=jax.ShapeDtypeStruct((m, n), BF16),
        compiler_params=_params("parallel", "parallel"),
        name="merge",
    )(o_mla, o_gqa, w_mla_o, w_gqa_o, gates, gates)


def _rms(x, g):
    return x * lax.rsqrt(jnp.mean(x * x, axis=-1, keepdims=True) + EPS) * g


def _prep_kernel(*refs, q_rank, kv_rank, use_rope):
    if use_rope:
        (p_ref, kr_ref, gqa_ref, gkv_ref, ggq_ref, ggk_ref, cg_ref, sg_ref, cm_ref, sm_ref,
         qa_o, ckv_o, ckvx_o, gq_o, gk32_o, gk_o, gv_o) = refs
    else:
        (p_ref, kr_ref, gqa_ref, gkv_ref, ggq_ref, ggk_ref,
         qa_o, ckv_o, ckvx_o, gq_o, gk32_o, gk_o, gv_o) = refs
    c0 = q_rank
    c1 = c0 + kv_rank
    c2 = c1 + GQA_HEADS * HEAD_DIM
    c3 = c2 + GQA_KV_HEADS * HEAD_DIM
    qa_o[...] = _rms(p_ref[:, 0:c0], gqa_ref[...]).astype(BF16)
    ckv = _rms(p_ref[:, c0:c1], gkv_ref[...])
    ckv_o[...] = ckv
    ckvx_o[:, 0:kv_rank] = ckv.astype(BF16)
    kr = kr_ref[...]
    if use_rope:
        kr = kr * cm_ref[...] + _swap_pairs(kr) * sm_ref[...]
    ckvx_o[:, kv_rank:kv_rank + LANE] = kr.astype(BF16)
    q_scale = HEAD_DIM ** -0.5 * LOG2E
    for h in range(GQA_HEADS):
        x = _rms(p_ref[:, c1 + h * HEAD_DIM:c1 + (h + 1) * HEAD_DIM], ggq_ref[...])
        if use_rope:
            x = x * cg_ref[...] + _swap_pairs(x) * sg_ref[...]
        gq_o[:, h * HEAD_DIM:(h + 1) * HEAD_DIM] = (x * q_scale).astype(BF16)
    for h in range(GQA_KV_HEADS):
        x = _rms(p_ref[:, c2 + h * HEAD_DIM:c2 + (h + 1) * HEAD_DIM], ggk_ref[...])
        if use_rope:
            x = x * cg_ref[...] + _swap_pairs(x) * sg_ref[...]
        gk32_o[:, h * HEAD_DIM:(h + 1) * HEAD_DIM] = x
        gk_o[:, h * HEAD_DIM:(h + 1) * HEAD_DIM] = x.astype(BF16)
    gv_o[...] = p_ref[:, c3:c3 + GQA_KV_HEADS * HEAD_DIM].astype(BF16)


def _prep(proj, kr, g_q_a, g_kv_a, g_gqa_q, g_gqa_k, rope, rows_per_batch):
    m = proj.shape[0]
    q_rank, kv_rank = g_q_a.shape[0], g_kv_a.shape[0]
    width = q_rank + kv_rank + (GQA_HEADS + 2 * GQA_KV_HEADS) * HEAD_DIM
    tm = _tile(rows_per_batch, 256)
    per = rows_per_batch // tm
    nq, nk = GQA_HEADS * HEAD_DIM, GQA_KV_HEADS * HEAD_DIM
    row = lambda i: (i, 0)
    const = lambda i: (0, 0)
    in_specs = [pl.BlockSpec((tm, width), row), pl.BlockSpec((tm, LANE), row),
                pl.BlockSpec((1, q_rank), const), pl.BlockSpec((1, kv_rank), const),
                pl.BlockSpec((1, HEAD_DIM), const), pl.BlockSpec((1, HEAD_DIM), const)]
    args = [proj, kr, g_q_a.reshape(1, -1), g_kv_a.reshape(1, -1), g_gqa_q.reshape(1, -1), g_gqa_k.reshape(1, -1)]
    if rope is not None:
        in_specs += [pl.BlockSpec((tm, LANE), lambda i: (i % per, 0))] * 4
        args += list(rope)
    outs = [((m, q_rank), BF16), ((m, kv_rank), F32), ((m, kv_rank + LANE), BF16), ((m, nq), BF16),
            ((m, nk), F32), ((m, nk), BF16), ((m, nk), BF16)]
    return pl.pallas_call(
        functools.partial(_prep_kernel, q_rank=q_rank, kv_rank=kv_rank, use_rope=rope is not None),
        grid=(m // tm,),
        in_specs=in_specs,
        out_specs=[pl.BlockSpec((tm, s[1]), row) for s, _ in outs],
        out_shape=[jax.ShapeDtypeStruct(s, dt) for s, dt in outs],
        compiler_params=_params("parallel"),
        name="prep",
    )(*args)


def _attn_kernel(q_ref, *refs, rep, dq, dv, sub):
    o_ref = refs[-1]
    ks = [r[...] for r in refs[0:-1:2]]
    vs = [r[...] for r in refs[1:-1:2]]
    for r in range(rep):
        for i in range(q_ref.shape[0] // sub):
            rows = slice(i * sub, (i + 1) * sub)
            q = q_ref[rows, r * dq:(r + 1) * dq]
            ss = [lax.dot_general(q, k, (((1,), (1,)), ((), ())), preferred_element_type=F32) for k in ks]
            s = ss[0] if len(ss) == 1 else jnp.concatenate(ss, axis=1)
            p = jnp.exp2(s - jnp.max(s, axis=-1, keepdims=True))
            l = jnp.sum(p, axis=-1, keepdims=True)
            p = p.astype(BF16)
            o, off = None, 0
            for v in vs:
                c = jnp.dot(p[:, off:off + v.shape[0]], v, preferred_element_type=F32)
                o = c if o is None else o + c
                off += v.shape[0]
            o_ref[rows, r * dv:(r + 1) * dv] = (o / l).astype(o_ref.dtype)


def _attention(q, kvs, n_groups, rep, dq, dv, k_col0, v_col0, tq):
    b, t, _ = q.shape
    tq = _tile(t, tq)
    sub = _tile(tq, 256)
    kb, vb = k_col0 // dq, v_col0 // dv
    assert kb * dq == k_col0 and vb * dv == v_col0
    in_specs = [pl.BlockSpec((None, tq, rep * dq), lambda bi, g, qi: (bi, qi, g))]
    args = [q]
    for k, v in kvs:
        in_specs += [pl.BlockSpec((None, k.shape[1], dq), lambda bi, g, qi: (bi, 0, kb + g)),
                     pl.BlockSpec((None, v.shape[1], dv), lambda bi, g, qi: (bi, 0, vb + g))]
        args += [k, v]
    return pl.pallas_call(
        functools.partial(_attn_kernel, rep=rep, dq=dq, dv=dv, sub=sub),
        grid=(b, n_groups, t // tq),
        in_specs=in_specs,
        out_specs=pl.BlockSpec((None, tq, rep * dv), lambda bi, g, qi: (bi, qi, g)),
        out_shape=jax.ShapeDtypeStruct((b, t, n_groups * rep * dv), BF16),
        compiler_params=_params("parallel", "parallel", "parallel"),
        name="attention",
    )(*args)


def _top16(s, idx):
    n, c = s.shape
    row = lax.broadcasted_iota(jnp.int32, (PEER_TOPK, c), 0)
    big = float(PEER_TOPK * PEER_TOPK)

    def body(r, carry):
        s, vals, idxs = carry
        mx = jnp.max(s, axis=0, keepdims=True)
        am = jnp.min(jnp.where(s == mx, idx, big), axis=0, keepdims=True)
        vals = jnp.where(row == r, mx, vals)
        idxs = jnp.where(row == r, am, idxs)
        s = jnp.where(idx == am, -jnp.inf, s)
        return s, vals, idxs

    init = (s, jnp.zeros((PEER_TOPK, c), F32), jnp.zeros((PEER_TOPK, c), F32))
    _, vals, idxs = lax.fori_loop(0, PEER_TOPK, body, init)
    return vals, idxs


def _pick(table, pos):
    out = jnp.zeros(pos.shape, F32)
    for a in range(PEER_TOPK):
        out = jnp.where(pos == a, table[a:a + 1], out)
    return out


_CAND_ROWS = [(0, 0), (0, 8), (1, 0), (2, 0), (3, 0), (4, 0), (5, 0), (6, 0), (7, 0)]


def _route_kernel(q_ref, k_ref, oi_ref, oj_ref, og_ref):
    q = q_ref[...]
    tn = q.shape[0]
    half = q.shape[1] // 2
    nt = (((1,), (1,)), ((), ()))
    s0 = lax.dot_general(k_ref[0], q[:, :half], nt, preferred_element_type=F32)
    s1 = lax.dot_general(k_ref[1], q[:, half:], nt, preferred_element_type=F32)
    s01 = jnp.concatenate([s0, s1], axis=1)
    v01, i01 = _top16(s01, lax.broadcasted_iota(jnp.int32, s01.shape, 0).astype(F32))
    v0, v1, i0, i1 = v01[:, :tn], v01[:, tn:], i01[:, :tn], i01[:, tn:]
    sub = lax.broadcasted_iota(jnp.int32, (SUBLANE, tn), 0).astype(F32)
    cand = [v0[a:a + 1] + v1[b:b + SUBLANE] for a, b in _CAND_ROWS]
    flat = [sub + float(a * PEER_TOPK + b) for a, b in _CAND_ROWS]
    cand.append(v0[SUBLANE:] + v1[0:1])
    flat.append((sub + float(SUBLANE)) * float(PEER_TOPK))
    best, pos = _top16(jnp.concatenate(cand, axis=0), jnp.concatenate(flat, axis=0))
    pos = pos.astype(jnp.int32)
    e = jnp.exp(best - best[0:1])
    oi_ref[...] = _pick(i0, pos >> 4)
    oj_ref[...] = _pick(i1, pos & (PEER_TOPK - 1))
    og_ref[...] = e / jnp.sum(e, axis=0, keepdims=True)


def _route(qp, sub_keys, tn=512):
    m = qp.shape[0]
    tn = _tile(m, tn)
    qd = qp.shape[1] // PEER_HEADS
    shp = jax.ShapeDtypeStruct((PEER_HEADS * PEER_TOPK, m), F32)
    spec = pl.BlockSpec((PEER_TOPK, tn), lambda t, h: (h, t))
    return pl.pallas_call(
        _route_kernel,
        grid=(m // tn, PEER_HEADS),
        in_specs=[pl.BlockSpec((tn, qd), lambda t, h: (t, h)),
                  pl.BlockSpec(sub_keys.shape, lambda t, h: (0, 0, 0))],
        out_specs=[spec, spec, spec],
        out_shape=[shp, shp, shp],
        compiler_params=_params("parallel", "parallel"),
        name="peer_route",
    )(qp, sub_keys)


def _wbuild_kernel(i_ref, j_ref, g_ref, o_ref, it_s, jt_s, gt_s):
    it_s[...] = i_ref[...].T
    jt_s[...] = j_ref[...].T
    gt_s[...] = g_ref[...].T
    ne = i_ref.shape[0]
    iota_a = lax.broadcasted_iota(jnp.int32, (2 * N_KEYS, 2 * ne), 0).astype(F32).astype(BF16)
    iota_g = lax.broadcasted_iota(jnp.int32, (N_KEYS, 2 * ne), 0).astype(F32).astype(BF16)
    one, zero = jnp.ones((), BF16), jnp.zeros((), BF16)

    def body(m, carry):
        ri = it_s[pl.ds(2 * m, 2), :]
        rj = jt_s[pl.ds(2 * m, 2), :]
        rg = gt_s[pl.ds(2 * m, 2), :]
        row_i = jnp.concatenate([ri[0:1], ri[1:2] + float(N_KEYS)], axis=1).astype(BF16)
        row_j = jnp.concatenate([rj[0:1], rj[1:2]], axis=1).astype(BF16)
        row_g = jnp.concatenate([rg[0:1], rg[1:2]], axis=1).astype(BF16)
        a2 = jnp.where(iota_a == row_i, one, zero)
        g2 = jnp.where(iota_g == row_j, row_g, zero)
        w = lax.dot_general(a2, g2, (((1,), (1,)), ((), ())), preferred_element_type=F32)
        for t in range(2):
            o_ref[:, 2 * m + t] = w[t * N_KEYS:(t + 1) * N_KEYS].reshape(N_KEYS // SUBLANE, SUBLANE, N_KEYS)
        return carry

    lax.fori_loop(0, o_ref.shape[1] // 2, body, 0, unroll=16)


def _wbuild(idx_i, idx_j, gates):
    ne, m = idx_i.shape
    tw = LANE
    spec = pl.BlockSpec((ne, tw), lambda t: (0, t))
    return pl.pallas_call(
        _wbuild_kernel,
        grid=(m // tw,),
        in_specs=[spec, spec, spec],
        out_specs=pl.BlockSpec((N_KEYS // SUBLANE, tw, SUBLANE, N_KEYS), lambda t: (0, t, 0, 0)),
        out_shape=jax.ShapeDtypeStruct((N_KEYS // SUBLANE, m, SUBLANE, N_KEYS), F32),
        scratch_shapes=[pltpu.VMEM((tw, ne), F32)] * 3,
        compiler_params=_params("parallel"),
        name="peer_gate_map",
    )(idx_i, idx_j, gates)


def _peer_kernel(x_ref, ut_ref, v_ref, w_ref, o_ref, ga_s, gb_s, *, rows, n_chunks):
    k = pl.program_id(1)
    base = (k % (SUBLANE // rows)) * rows

    def gates(dst):
        s = jnp.dot(x_ref[...], ut_ref[...], preferred_element_type=F32)
        for il in range(rows):
            wg = w_ref[:, base + il, :]
            sl = s[:, il * N_KEYS:(il + 1) * N_KEYS]
            g = 0.5 * sl * (1.0 + lax.erf(sl * INV_SQRT2)) * wg
            dst[:, il * N_KEYS:(il + 1) * N_KEYS] = g.astype(BF16)

    def mix(src):
        return jnp.dot(src[...], v_ref[...], preferred_element_type=F32)

    inner = jnp.logical_and(k > 0, k < n_chunks)

    @pl.when(k == 0)
    def _():
        o_ref[...] = jnp.zeros_like(o_ref)
        gates(ga_s)

    @pl.when(jnp.logical_and(inner, k % 2 == 1))
    def _():
        gates(gb_s)
        o_ref[...] += mix(ga_s)

    @pl.when(jnp.logical_and(inner, k % 2 == 0))
    def _():
        gates(ga_s)
        o_ref[...] += mix(gb_s)

    @pl.when(k == n_chunks)
    def _():
        o_ref[...] += mix(gb_s if n_chunks % 2 == 0 else ga_s)


def _peer_mix(h2, u_t, v, wmap, tn=1024):
    m, d = h2.shape
    n_exp = v.shape[0]
    tn = _tile(m, tn)
    rows = PEER_CHUNK_ROWS
    ec = rows * N_KEYS
    n_chunks = n_exp // ec
    assert u_t.shape == (n_chunks, d, ec)
    per = SUBLANE // rows
    last = n_chunks - 1
    return pl.pallas_call(
        functools.partial(_peer_kernel, rows=rows, n_chunks=n_chunks),
        grid=(m // tn, n_chunks + 1),
        in_specs=[pl.BlockSpec((tn, d), lambda t, k: (t, 0), pipeline_mode=pl.Buffered(1)),
                  pl.BlockSpec((None, d, ec), lambda t, k: (jnp.minimum(k, last), 0, 0)),
                  pl.BlockSpec((ec, d), lambda t, k: (jnp.maximum(k - 1, 0), 0)),
                  pl.BlockSpec((None, tn, SUBLANE, N_KEYS), lambda t, k: (jnp.minimum(k, last) // per, t, 0, 0))],
        out_specs=pl.BlockSpec((tn, d), lambda t, k: (t, 0), pipeline_mode=pl.Buffered(1)),
        out_shape=jax.ShapeDtypeStruct((m, d), F32),
        scratch_shapes=[pltpu.VMEM((tn, ec), BF16), pltpu.VMEM((tn, ec), BF16)],
        compiler_params=_params("parallel", "arbitrary"),
        name="peer_mix",
    )(h2, u_t, v, wmap)


def _final_kernel(x_ref, p_ref, m_ref, g_ref, o_ref, *, gate_row):
    x = x_ref[...] + m_ref[0][gate_row:gate_row + 1] * p_ref[...]
    o_ref[...] = _rms(x, g_ref[...])


def _final(x, po, mods, g_final, rows_per_batch, gate_row):
    m, d = x.shape
    tm = _tile(rows_per_batch, 256)
    per = rows_per_batch // tm
    row = lambda i: (i, 0)
    return pl.pallas_call(
        functools.partial(_final_kernel, gate_row=gate_row),
        grid=(m // tm,),
        in_specs=[pl.BlockSpec((tm, d), row), pl.BlockSpec((tm, d), row),
                  pl.BlockSpec((1, 8, d), lambda i: (i // per, 0, 0)),
                  pl.BlockSpec((1, d), lambda i: (0, 0))],
        out_specs=pl.BlockSpec((tm, d), row),
        out_shape=jax.ShapeDtypeStruct((m, d), F32),
        compiler_params=_params("parallel"),
        name="final_norm",
    )(x, po, mods, g_final.reshape(1, d))


def _rope_tables(n_tokens, dim):
    n_rows = n_tokens // GRID_W
    rows = jnp.broadcast_to(jnp.arange(n_rows)[:, None], (n_rows, GRID_W)).reshape(-1).astype(F32)
    cols = jnp.broadcast_to(jnp.arange(GRID_W)[None, :], (n_rows, GRID_W)).reshape(-1).astype(F32)
    n_freq = dim // 4
    freqs = ROPE_BASE ** (-jnp.arange(n_freq, dtype=F32) / n_freq)
    ang = jnp.concatenate([rows[:, None] * freqs, cols[:, None] * freqs], axis=-1)
    cos, sin = jnp.cos(ang), jnp.sin(ang)
    cos2 = jnp.repeat(cos, 2, axis=-1)
    sin2 = jnp.stack([-sin, sin], axis=-1).reshape(n_tokens, dim)
    return cos2, sin2


def _layer_weights(w_in, w_q_b, w_kv_b, w_mla_o, w_gqa_o, w_out, w_peer_q, sub_keys, peer_u, peer_v):
    q_rank, kv_rank = w_q_b.shape[0], w_kv_b.shape[0]
    k0 = q_rank + kv_rank
    lw = {}
    g0 = k0 + MLA_ROPE + (GQA_HEADS + 2 * GQA_KV_HEADS) * HEAD_DIM
    lw['w_in'] = jnp.concatenate([w_in[:, :k0], w_in[:, k0 + MLA_ROPE:g0]], axis=1).astype(BF16)
    lw['w_gates'] = w_in[:, g0:].astype(BF16)
    lw['w_kr'] = jnp.pad(w_in[:, k0:k0 + MLA_ROPE], ((0, 0), (0, LANE - MLA_ROPE))).astype(BF16)
    wq = w_q_b.reshape(q_rank, MLA_HEADS, MLA_NOPE + MLA_ROPE)
    wq = jnp.pad(wq, ((0, 0), (0, 0), (0, MLA_QPAD - MLA_NOPE - MLA_ROPE)))
    lw['w_q_b'] = wq.reshape(q_rank, MLA_HEADS * MLA_QPAD).astype(BF16)
    wkv = w_kv_b.reshape(kv_rank, MLA_HEADS, MLA_NOPE + MLA_V)
    wk = jnp.zeros((kv_rank + LANE, MLA_HEADS, MLA_QPAD), F32)
    wk = wk.at[:kv_rank, :, :MLA_NOPE].set(wkv[:, :, :MLA_NOPE])
    eye = jnp.broadcast_to(jnp.eye(MLA_ROPE, dtype=F32)[:, None, :], (MLA_ROPE, MLA_HEADS, MLA_ROPE))
    wk = wk.at[kv_rank:kv_rank + MLA_ROPE, :, MLA_NOPE:MLA_NOPE + MLA_ROPE].set(eye)
    wv = jnp.pad(wkv[:, :, MLA_NOPE:], ((0, LANE), (0, 0), (0, 0)))
    lw['w_kv'] = jnp.concatenate([wk.reshape(kv_rank + LANE, -1), wv.reshape(kv_rank + LANE, -1)],
                                 axis=1).astype(BF16)
    lw['w_mla_o'] = w_mla_o.astype(BF16)
    lw['w_gqa_o'] = w_gqa_o.astype(BF16)
    lw['w_out'] = w_out.astype(BF16)
    lw['w_peer_q'] = w_peer_q.astype(BF16)
    lw['sub_keys'] = sub_keys.astype(BF16)
    chunk = PEER_CHUNK_ROWS * N_KEYS
    lw['peer_ut'] = peer_u.reshape(-1, chunk, peer_u.shape[1]).transpose(0, 2, 1).astype(BF16)
    lw['peer_v'] = peer_v.astype(BF16)
    return lw


def _layer(x, mods, lw, gains, rope, ctx):
    b, t, d = x.shape
    m = b * t
    x2 = x.reshape(m, d)
    g_norm1, g_norm2, g_q_a, g_kv_a, g_gqa_q, g_gqa_k = gains
    q_rank, kv_rank = g_q_a.shape[0], g_kv_a.shape[0]
    nq, nk = GQA_HEADS * HEAD_DIM, GQA_KV_HEADS * HEAD_DIM
    col_gate_mla = q_rank + kv_rank + nq + 2 * nk

    rpb = t if mods.shape[0] == b else m
    h = _norm_mod(x2, g_norm1, mods, rpb, SHIFT1, SCALE1)
    proj = _mm(h, lw['w_in'], F32)
    gates = _mm(h, lw['w_gates'], BF16, sigmoid=True)
    kr = _mm(h, lw['w_kr'], F32)
    prep_rope = None if rope is None else rope[:4]
    qa, ckv32, ckvx, gq, gk32, gk, gv = _prep(proj, kr, g_q_a, g_kv_a, g_gqa_q, g_gqa_k, prep_rope, t)

    mla_scale = (MLA_NOPE + MLA_ROPE) ** -0.5 * LOG2E
    if rope is None:
        q_mla = _mm(qa, lw['w_q_b'], BF16, scale=mla_scale)
    else:
        q_mla = _mm(qa, lw['w_q_b'], BF16, scale=mla_scale, rope=rope[4:], rows_per_batch=t)

    kv = _mm(ckvx, lw['w_kv'], BF16, tn=2048).reshape(b, t, -1)
    mla_kv = [(kv, kv)]
    gqa_kv = [(gk.reshape(b, t, nk), gv.reshape(b, t, nk))]
    if ctx is not None:
        c_ckv, c_kr, c_k, c_v = ctx
        past = c_ckv.shape[1]
        c_x = jnp.concatenate([c_ckv, c_kr, jnp.zeros((b, past, LANE - MLA_ROPE), F32)], axis=-1).astype(BF16)
        c_kv = _mm(c_x.reshape(b * past, -1), lw['w_kv'], BF16, tn=2048).reshape(b, past, -1)
        mla_kv.append((c_kv, c_kv))
        gqa_kv.append((c_k.reshape(b, past, nk).astype(BF16), c_v.reshape(b, past, nk).astype(BF16)))

    o_mla = _attention(q_mla.reshape(b, t, -1), mla_kv, MLA_HEADS, 1, MLA_QPAD, MLA_V,
                       0, MLA_HEADS * MLA_QPAD, tq=4096)
    o_gqa = _attention(gq.reshape(b, t, -1), gqa_kv, GQA_KV_HEADS, GQA_HEADS // GQA_KV_HEADS,
                       HEAD_DIM, HEAD_DIM, 0, 0, tq=1024)
    merged = _merge(o_mla.reshape(m, -1), o_gqa.reshape(m, -1), lw['w_mla_o'], lw['w_gqa_o'],
                    gates)
    x1 = _mm_residual(merged, lw['w_out'], x2, mods, rpb, GATE1)

    h2 = _norm_mod(x1, g_norm2, mods, rpb, SHIFT2, SCALE2)
    qp = _mm(h2, lw['w_peer_q'], BF16)
    idx_i, idx_j, gates = _route(qp, lw['sub_keys'])
    wmap = _wbuild(idx_i, idx_j, gates)
    po = _peer_mix(h2, lw['peer_ut'], lw['peer_v'], wmap)
    own = (ckv32.reshape(b, t, kv_rank), kr[:, :MLA_ROPE].reshape(b, t, MLA_ROPE),
           gk32.reshape(b, t, GQA_KV_HEADS, HEAD_DIM),
           proj[:, col_gate_mla - nk:col_gate_mla].reshape(b, t, GQA_KV_HEADS, HEAD_DIM))
    return x1, po, own


def kernel(x_prompt, x_sample, c, cache_mla_ckv, cache_mla_krope, cache_gqa_k, cache_gqa_v, c_ctx, w_mod, b_mod, g_norm1, g_norm2, w_in, g_q_a, w_q_b, g_kv_a, w_kv_b, g_gqa_q, g_gqa_k, w_mla_o, w_gqa_o, w_out, w_peer_q, peer_sub_keys, peer_u, peer_v, g_final):
    assert w_in.shape[0] == 1, "one trunk layer is supported"
    d = x_prompt.shape[-1]
    bc, tc = x_prompt.shape[:2]
    bl, tl = x_sample.shape[:2]

    cg, sg = _rope_tables(tl, HEAD_DIM)
    cm, sm = _rope_tables(tl, MLA_ROPE)
    one = jnp.ones((tl, LANE - MLA_ROPE), F32)
    zero = jnp.zeros((tl, LANE - MLA_ROPE), F32)
    cm_k, sm_k = jnp.concatenate([cm, one], 1), jnp.concatenate([sm, zero], 1)
    ones_n, zeros_n = jnp.ones((tl, MLA_NOPE), F32), jnp.zeros((tl, MLA_NOPE), F32)
    cq = jnp.tile(jnp.concatenate([ones_n, cm, one], 1), (1, 2))
    sq = jnp.tile(jnp.concatenate([zeros_n, sm, zero], 1), (1, 2))
    rope = (cg, sg, cm_k, sm_k, cq, sq)

    n_rows = bl + 1
    pad_rows = -n_rows % 8
    cvecs = jnp.concatenate([c, c_ctx[None, :], jnp.zeros((pad_rows, d), F32)], axis=0)

    lw = _layer_weights(w_in[0], w_q_b[0], w_kv_b[0], w_mla_o[0], w_gqa_o[0], w_out[0],
                        w_peer_q[0], peer_sub_keys[0], peer_u[0], peer_v[0])
    gains = (g_norm1[0], g_norm2[0], g_q_a[0], g_kv_a[0], g_gqa_q[0], g_gqa_k[0])
    mod = _modulation(cvecs, w_mod[0], b_mod[0]).reshape(n_rows + pad_rows, N_MOD, d)
    mod = jnp.pad(mod, ((0, 0), (0, 8 - N_MOD), (0, 0)))
    mods_lat, mods_ctx = mod[:bl], mod[bl:bl + 1]

    x1, po, own = _layer(x_prompt, mods_ctx, lw, gains, None, None)
    y_prompt = _final(x1, po, mods_ctx, g_final, bc * tc, GATE2).reshape(bc, tc, d)

    ctx = (cache_mla_ckv[:, 0], cache_mla_krope[:, 0], cache_gqa_k[:, 0], cache_gqa_v[:, 0])
    x1, po, _ = _layer(x_sample, mods_lat, lw, gains, rope, ctx)
    y_sample = _final(x1, po, mods_lat, g_final, tl, GATE2).reshape(bl, tl, d)

    return (y_prompt, y_sample) + tuple(o[:, None] for o in own)
```

```python
import functools
import math

import jax
import jax.numpy as jnp
from jax import lax
from jax.experimental import pallas as pl
from jax.experimental.pallas import tpu as pltpu

F32 = jnp.float32
BF16 = jnp.bfloat16

GRID_W = 64
EPS = 1e-6
ROPE_BASE = 10000.0
N_MOD = 6
MLA_HEADS = 16
MLA_NOPE = 128
MLA_ROPE = 64
MLA_V = 128
MLA_QPAD = 256
GQA_HEADS = 16
GQA_KV_HEADS = 4
HEAD_DIM = 128
PEER_HEADS = 8
N_KEYS = 128
PEER_TOPK = 16
PEER_CHUNK_ROWS = 4
LANE = 128
SUBLANE = 8
VMEM_LIMIT = 56 * 1024 * 1024
INV_SQRT2 = 1.0 / math.sqrt(2.0)
LOG2E = 1.0 / math.log(2.0)

SHIFT1, SCALE1, GATE1, SHIFT2, SCALE2, GATE2 = range(6)


def _tile(dim, pref):
    t = min(dim, pref)
    assert dim % t == 0, (dim, pref)
    return t


def _params(*sem):
    return pltpu.CompilerParams(dimension_semantics=sem, vmem_limit_bytes=VMEM_LIMIT)


def _mod_kernel(c_ref, w_ref, b_ref, o_ref):
    c = c_ref[...]
    a = (c * jax.nn.sigmoid(c)).astype(BF16)
    o_ref[...] = jnp.dot(a, w_ref[...].astype(BF16), preferred_element_type=F32) + b_ref[...]


def _modulation(cvecs, w_mod, b_mod):
    r, d = cvecs.shape
    n = w_mod.shape[1]
    tn = _tile(n, 512)
    return pl.pallas_call(
        _mod_kernel,
        grid=(n // tn,),
        in_specs=[pl.BlockSpec((r, d), lambda j: (0, 0)),
                  pl.BlockSpec((d, tn), lambda j: (0, j)),
                  pl.BlockSpec((1, tn), lambda j: (0, j))],
        out_specs=pl.BlockSpec((r, tn), lambda j: (0, j)),
        out_shape=jax.ShapeDtypeStruct((r, n), F32),
        compiler_params=_params("parallel"),
        name="modulation",
    )(cvecs, w_mod, b_mod.reshape(1, n))


def _norm_mod_kernel(x_ref, g_ref, m_ref, o_ref, *, shift_row, scale_row):
    x = x_ref[...]
    y = x * lax.rsqrt(jnp.mean(x * x, axis=-1, keepdims=True) + EPS) * g_ref[...]
    m = m_ref[0]
    o_ref[...] = (y * (1.0 + m[scale_row:scale_row + 1]) + m[shift_row:shift_row + 1]).astype(BF16)


def _norm_mod(x, g, mods, rows_per_batch, shift_row, scale_row):
    m, d = x.shape
    tm = _tile(rows_per_batch, 256)
    per = rows_per_batch // tm
    return pl.pallas_call(
        functools.partial(_norm_mod_kernel, shift_row=shift_row, scale_row=scale_row),
        grid=(m // tm,),
        in_specs=[pl.BlockSpec((tm, d), lambda i: (i, 0)),
                  pl.BlockSpec((1, d), lambda i: (0, 0)),
                  pl.BlockSpec((1, 8, d), lambda i: (i // per, 0, 0))],
        out_specs=pl.BlockSpec((tm, d), lambda i: (i, 0)),
        out_shape=jax.ShapeDtypeStruct((m, d), BF16),
        compiler_params=_params("parallel"),
        name="norm_mod",
    )(x, g.reshape(1, d), mods)


def _swap_pairs(x):
    ax = x.ndim - 1
    n = x.shape[ax]
    lane = lax.broadcasted_iota(jnp.int32, x.shape, ax)
    nxt = pltpu.roll(x, n - 1, axis=ax)
    prv = pltpu.roll(x, 1, axis=ax)
    return jnp.where((lane & 1) == 0, nxt, prv)


def _mm_kernel(a_ref, b_ref, o_ref, *, scale, sigmoid):
    acc = jnp.dot(a_ref[...], b_ref[...], preferred_element_type=F32)
    if scale is not None:
        acc = acc * scale
    if sigmoid:
        acc = jax.nn.sigmoid(acc)
    o_ref[...] = acc.astype(o_ref.dtype)


def _mm_rope_kernel(a_ref, b_ref, cos_ref, sin_ref, o_ref, *, scale):
    acc = jnp.dot(a_ref[...], b_ref[...], preferred_element_type=F32)
    for lo in range(0, acc.shape[1], MLA_QPAD):
        mid, hi = lo + MLA_NOPE, lo + MLA_QPAD
        o_ref[:, lo:mid] = (acc[:, lo:mid] * scale).astype(o_ref.dtype)
        r = acc[:, mid:hi]
        r = r * cos_ref[:, mid:hi] + _swap_pairs(r) * sin_ref[:, mid:hi]
        o_ref[:, mid:hi] = (r * scale).astype(o_ref.dtype)


def _mm(a, b, out_dtype, scale=None, sigmoid=False, rope=None, rows_per_batch=None, tm=1024, tn=512):
    m, k = a.shape
    n = b.shape[1]
    tm = _tile(m if rows_per_batch is None else rows_per_batch, tm)
    tn = _tile(n, tn)
    in_specs = [pl.BlockSpec((tm, k), lambda i, j: (i, 0)),
                pl.BlockSpec((k, tn), lambda i, j: (0, j))]
    args = [a, b]
    if rope is None:
        body = functools.partial(_mm_kernel, scale=scale, sigmoid=sigmoid)
    else:
        per = rows_per_batch // tm
        body = functools.partial(_mm_rope_kernel, scale=scale)
        in_specs += [pl.BlockSpec((tm, tn), lambda i, j: (i % per, 0))] * 2
        args += list(rope)
    return pl.pallas_call(
        body,
        grid=(m // tm, n // tn),
        in_specs=in_specs,
        out_specs=pl.BlockSpec((tm, tn), lambda i, j: (i, j)),
        out_shape=jax.ShapeDtypeStruct((m, n), out_dtype),
        compiler_params=_params("parallel", "parallel"),
        name="matmul",
    )(*args)


def _mm_res_kernel(a_ref, b_ref, x_ref, m_ref, o_ref, *, gate_row):
    acc = jnp.dot(a_ref[...], b_ref[...], preferred_element_type=F32)
    o_ref[...] = x_ref[...] + m_ref[0][gate_row:gate_row + 1] * acc


def _mm_residual(a, b, x, mods, rows_per_batch, gate_row, tm=1024, tn=1024):
    m, k = a.shape
    n = b.shape[1]
    tm = _tile(rows_per_batch, tm)
    tn = _tile(n, tn)
    per = rows_per_batch // tm
    return pl.pallas_call(
        functools.partial(_mm_res_kernel, gate_row=gate_row),
        grid=(m // tm, n // tn),
        in_specs=[pl.BlockSpec((tm, k), lambda i, j: (i, 0)),
                  pl.BlockSpec((k, tn), lambda i, j: (0, j)),
                  pl.BlockSpec((tm, tn), lambda i, j: (i, j)),
                  pl.BlockSpec((1, 8, tn), lambda i, j: (i // per, 0, j))],
        out_specs=pl.BlockSpec((tm, tn), lambda i, j: (i, j)),
        out_shape=jax.ShapeDtypeStruct((m, n), F32),
        compiler_params=_params("parallel", "parallel"),
        name="matmul_residual",
    )(a, b, x, mods)


def _merge_kernel(oa_ref, ob_ref, wa_ref, wb_ref, ga_ref, gb_ref, o_ref):
    ya = jnp.dot(oa_ref[...], wa_ref[...], preferred_element_type=F32)
    yb = jnp.dot(ob_ref[...], wb_ref[...], preferred_element_type=F32)
    o_ref[...] = (ga_ref[...] * ya + gb_ref[...] * yb).astype(o_ref.dtype)


def _merge(o_mla, o_gqa, w_mla_o, w_gqa_o, gates, tm=1024, tn=1024):
    m, k = o_mla.shape
    n = w_mla_o.shape[1]
    tm = _tile(m, tm)
    tn = _tile(n, tn)
    ca, cb = 0, n // tn
    return pl.pallas_call(
        _merge_kernel,
        grid=(m // tm, n // tn),
        in_specs=[pl.BlockSpec((tm, k), lambda i, j: (i, 0)),
                  pl.BlockSpec((tm, k), lambda i, j: (i, 0)),
                  pl.BlockSpec((k, tn), lambda i, j: (0, j)),
                  pl.BlockSpec((k, tn), lambda i, j: (0, j)),
                  pl.BlockSpec((tm, tn), lambda i, j: (i, ca + j)),
                  pl.BlockSpec((tm, tn), lambda i, j: (i, cb + j))],
        out_specs=pl.BlockSpec((tm, tn), lambda i, j: (i, j)),
        out_shape=jax.ShapeDtypeStruct((m, n), BF16),
        compiler_params=_params("parallel", "parallel"),
        name="merge",
    )(o_mla, o_gqa, w_mla_o, w_gqa_o, gates, gates)


def _rms(x, g):
    return x * lax.rsqrt(jnp.mean(x * x, axis=-1, keepdims=True) + EPS) * g


def _prep_kernel(*refs, q_rank, kv_rank, use_rope):
    if use_rope:
        (p_ref, kr_ref, gqa_ref, gkv_ref, ggq_ref, ggk_ref, cg_ref, sg_ref, cm_ref, sm_ref,
         qa_o, ckv_o, ckvx_o, gq_o, gk32_o, gk_o, gv_o) = refs
    else:
        (p_ref, kr_ref, gqa_ref, gkv_ref, ggq_ref, ggk_ref,
         qa_o, ckv_o, ckvx_o, gq_o, gk32_o, gk_o, gv_o) = refs
    c0 = q_rank
    c1 = c0 + kv_rank
    c2 = c1 + GQA_HEADS * HEAD_DIM
    c3 = c2 + GQA_KV_HEADS * HEAD_DIM
    qa_o[...] = _rms(p_ref[:, 0:c0], gqa_ref[...]).astype(BF16)
    ckv = _rms(p_ref[:, c0:c1], gkv_ref[...])
    ckv_o[...] = ckv
    ckvx_o[:, 0:kv_rank] = ckv.astype(BF16)
    kr = kr_ref[...]
    if use_rope:
        kr = kr * cm_ref[...] + _swap_pairs(kr) * sm_ref[...]
    ckvx_o[:, kv_rank:kv_rank + LANE] = kr.astype(BF16)
    q_scale = HEAD_DIM ** -0.5 * LOG2E
    for h in range(GQA_HEADS):
        x = _rms(p_ref[:, c1 + h * HEAD_DIM:c1 + (h + 1) * HEAD_DIM], ggq_ref[...])
        if use_rope:
            x = x * cg_ref[...] + _swap_pairs(x) * sg_ref[...]
        gq_o[:, h * HEAD_DIM:(h + 1) * HEAD_DIM] = (x * q_scale).astype(BF16)
    for h in range(GQA_KV_HEADS):
        x = _rms(p_ref[:, c2 + h * HEAD_DIM:c2 + (h + 1) * HEAD_DIM], ggk_ref[...])
        if use_rope:
            x = x * cg_ref[...] + _swap_pairs(x) * sg_ref[...]
        gk32_o[:, h * HEAD_DIM:(h + 1) * HEAD_DIM] = x
        gk_o[:, h * HEAD_DIM:(h + 1) * HEAD_DIM] = x.astype(BF16)
    gv_o[...] = p_ref[:, c3:c3 + GQA_KV_HEADS * HEAD_DIM].astype(BF16)


def _prep(proj, kr, g_q_a, g_kv_a, g_gqa_q, g_gqa_k, rope, rows_per_batch):
    m = proj.shape[0]
    q_rank, kv_rank = g_q_a.shape[0], g_kv_a.shape[0]
    width = q_rank + kv_rank + (GQA_HEADS + 2 * GQA_KV_HEADS) * HEAD_DIM
    tm = _tile(rows_per_batch, 256)
    per = rows_per_batch // tm
    nq, nk = GQA_HEADS * HEAD_DIM, GQA_KV_HEADS * HEAD_DIM
    row = lambda i: (i, 0)
    const = lambda i: (0, 0)
    in_specs = [pl.BlockSpec((tm, width), row), pl.BlockSpec((tm, LANE), row),
                pl.BlockSpec((1, q_rank), const), pl.BlockSpec((1, kv_rank), const),
                pl.BlockSpec((1, HEAD_DIM), const), pl.BlockSpec((1, HEAD_DIM), const)]
    args = [proj, kr, g_q_a.reshape(1, -1), g_kv_a.reshape(1, -1), g_gqa_q.reshape(1, -1), g_gqa_k.reshape(1, -1)]
    if rope is not None:
        in_specs += [pl.BlockSpec((tm, LANE), lambda i: (i % per, 0))] * 4
        args += list(rope)
    outs = [((m, q_rank), BF16), ((m, kv_rank), F32), ((m, kv_rank + LANE), BF16), ((m, nq), BF16),
            ((m, nk), F32), ((m, nk), BF16), ((m, nk), BF16)]
    return pl.pallas_call(
        functools.partial(_prep_kernel, q_rank=q_rank, kv_rank=kv_rank, use_rope=rope is not None),
        grid=(m // tm,),
        in_specs=in_specs,
        out_specs=[pl.BlockSpec((tm, s[1]), row) for s, _ in outs],
        out_shape=[jax.ShapeDtypeStruct(s, dt) for s, dt in outs],
        compiler_params=_params("parallel"),
        name="prep",
    )(*args)


def _attn_kernel(q_ref, *refs, rep, dq, dv, sub):
    o_ref = refs[-1]
    ks = [r[...] for r in refs[0:-1:2]]
    vs = [r[...] for r in refs[1:-1:2]]
    for r in range(rep):
        for i in range(q_ref.shape[0] // sub):
            rows = slice(i * sub, (i + 1) * sub)
            q = q_ref[rows, r * dq:(r + 1) * dq]
            ss = [lax.dot_general(q, k, (((1,), (1,)), ((), ())), preferred_element_type=F32) for k in ks]
            s = ss[0] if len(ss) == 1 else jnp.concatenate(ss, axis=1)
            p = jnp.exp2(s - jnp.max(s, axis=-1, keepdims=True))
            l = jnp.sum(p, axis=-1, keepdims=True)
            p = p.astype(BF16)
            o, off = None, 0
            for v in vs:
                c = jnp.dot(p[:, off:off + v.shape[0]], v, preferred_element_type=F32)
                o = c if o is None else o + c
                off += v.shape[0]
            o_ref[rows, r * dv:(r + 1) * dv] = (o / l).astype(o_ref.dtype)


def _attention(q, kvs, n_groups, rep, dq, dv, k_col0, v_col0, tq):
    b, t, _ = q.shape
    tq = _tile(t, tq)
    sub = _tile(tq, 256)
    kb, vb = k_col0 // dq, v_col0 // dv
    assert kb * dq == k_col0 and vb * dv == v_col0
    in_specs = [pl.BlockSpec((None, tq, rep * dq), lambda bi, g, qi: (bi, qi, g))]
    args = [q]
    for k, v in kvs:
        in_specs += [pl.BlockSpec((None, k.shape[1], dq), lambda bi, g, qi: (bi, 0, kb + g)),
                     pl.BlockSpec((None, v.shape[1], dv), lambda bi, g, qi: (bi, 0, vb + g))]
        args += [k, v]
    return pl.pallas_call(
        functools.partial(_attn_kernel, rep=rep, dq=dq, dv=dv, sub=sub),
        grid=(b, n_groups, t // tq),
        in_specs=in_specs,
        out_specs=pl.BlockSpec((None, tq, rep * dv), lambda bi, g, qi: (bi, qi, g)),
        out_shape=jax.ShapeDtypeStruct((b, t, n_groups * rep * dv), BF16),
        compiler_params=_params("parallel", "parallel", "parallel"),
        name="attention",
    )(*args)


def _top16(s, idx):
    n, c = s.shape
    row = lax.broadcasted_iota(jnp.int32, (PEER_TOPK, c), 0)
    big = float(PEER_TOPK * PEER_TOPK)

    def body(r, carry):
        s, vals, idxs = carry
        mx = jnp.max(s, axis=0, keepdims=True)
        am = jnp.min(jnp.where(s == mx, idx, big), axis=0, keepdims=True)
        vals = jnp.where(row == r, mx, vals)
        idxs = jnp.where(row == r, am, idxs)
        s = jnp.where(idx == am, -jnp.inf, s)
        return s, vals, idxs

    init = (s, jnp.zeros((PEER_TOPK, c), F32), jnp.zeros((PEER_TOPK, c), F32))
    _, vals, idxs = lax.fori_loop(0, PEER_TOPK, body, init)
    return vals, idxs


def _pick(table, pos):
    out = jnp.zeros(pos.shape, F32)
    for a in range(PEER_TOPK):
        out = jnp.where(pos == a, table[a:a + 1], out)
    return out


_CAND_ROWS = [(0, 0), (0, 8), (1, 0), (2, 0), (3, 0), (4, 0), (5, 0), (6, 0), (7, 0)]


def _route_kernel(q_ref, k_ref, oi_ref, oj_ref, og_ref):
    q = q_ref[...]
    tn = q.shape[0]
    half = q.shape[1] // 2
    nt = (((1,), (1,)), ((), ()))
    s0 = lax.dot_general(k_ref[0], q[:, :half], nt, preferred_element_type=F32)
    s1 = lax.dot_general(k_ref[1], q[:, half:], nt, preferred_element_type=F32)
    s01 = jnp.concatenate([s0, s1], axis=1)
    v01, i01 = _top16(s01, lax.broadcasted_iota(jnp.int32, s01.shape, 0).astype(F32))
    v0, v1, i0, i1 = v01[:, :tn], v01[:, tn:], i01[:, :tn], i01[:, tn:]
    sub = lax.broadcasted_iota(jnp.int32, (SUBLANE, tn), 0).astype(F32)
    cand = [v0[a:a + 1] + v1[b:b + SUBLANE] for a, b in _CAND_ROWS]
    flat = [sub + float(a * PEER_TOPK + b) for a, b in _CAND_ROWS]
    cand.append(v0[SUBLANE:] + v1[0:1])
    flat.append((sub + float(SUBLANE)) * float(PEER_TOPK))
    best, pos = _top16(jnp.concatenate(cand, axis=0), jnp.concatenate(flat, axis=0))
    pos = pos.astype(jnp.int32)
    e = jnp.exp(best - best[0:1])
    oi_ref[...] = _pick(i0, pos >> 4)
    oj_ref[...] = _pick(i1, pos & (PEER_TOPK - 1))
    og_ref[...] = e / jnp.sum(e, axis=0, keepdims=True)


def _route(qp, sub_keys, tn=512):
    m = qp.shape[0]
    tn = _tile(m, tn)
    qd = qp.shape[1] // PEER_HEADS
    shp = jax.ShapeDtypeStruct((PEER_HEADS * PEER_TOPK, m), F32)
    spec = pl.BlockSpec((PEER_TOPK, tn), lambda t, h: (h, t))
    return pl.pallas_call(
        _route_kernel,
        grid=(m // tn, PEER_HEADS),
        in_specs=[pl.BlockSpec((tn, qd), lambda t, h: (t, h)),
                  pl.BlockSpec(sub_keys.shape, lambda t, h: (0, 0, 0))],
        out_specs=[spec, spec, spec],
        out_shape=[shp, shp, shp],
        compiler_params=_params("parallel", "parallel"),
        name="peer_route",
    )(qp, sub_keys)


def _wbuild_kernel(i_ref, j_ref, g_ref, o_ref, it_s, jt_s, gt_s):
    it_s[...] = i_ref[...].T
    jt_s[...] = j_ref[...].T
    gt_s[...] = g_ref[...].T
    ne = i_ref.shape[0]
    iota_a = lax.broadcasted_iota(jnp.int32, (2 * N_KEYS, 2 * ne), 0).astype(F32).astype(BF16)
    iota_g = lax.broadcasted_iota(jnp.int32, (N_KEYS, 2 * ne), 0).astype(F32).astype(BF16)
    one, zero = jnp.ones((), BF16), jnp.zeros((), BF16)

    def body(m, carry):
        ri = it_s[pl.ds(2 * m, 2), :]
        rj = jt_s[pl.ds(2 * m, 2), :]
        rg = gt_s[pl.ds(2 * m, 2), :]
        row_i = jnp.concatenate([ri[0:1], ri[1:2] + float(N_KEYS)], axis=1).astype(BF16)
        row_j = jnp.concatenate([rj[0:1], rj[1:2]], axis=1).astype(BF16)
        row_g = jnp.concatenate([rg[0:1], rg[1:2]], axis=1).astype(BF16)
        a2 = jnp.where(iota_a == row_i, one, zero)
        g2 = jnp.where(iota_g == row_j, row_g, zero)
        w = lax.dot_general(a2, g2, (((1,), (1,)), ((), ())), preferred_element_type=F32)
        for t in range(2):
            o_ref[:, 2 * m + t] = w[t * N_KEYS:(t + 1) * N_KEYS].reshape(N_KEYS // SUBLANE, SUBLANE, N_KEYS)
        return carry

    lax.fori_loop(0, o_ref.shape[1] // 2, body, 0, unroll=16)


def _wbuild(idx_i, idx_j, gates):
    ne, m = idx_i.shape
    tw = LANE
    spec = pl.BlockSpec((ne, tw), lambda t: (0, t))
    return pl.pallas_call(
        _wbuild_kernel,
        grid=(m // tw,),
        in_specs=[spec, spec, spec],
        out_specs=pl.BlockSpec((N_KEYS // SUBLANE, tw, SUBLANE, N_KEYS), lambda t: (0, t, 0, 0)),
        out_shape=jax.ShapeDtypeStruct((N_KEYS // SUBLANE, m, SUBLANE, N_KEYS), F32),
        scratch_shapes=[pltpu.VMEM((tw, ne), F32)] * 3,
        compiler_params=_params("parallel"),
        name="peer_gate_map",
    )(idx_i, idx_j, gates)


def _peer_kernel(x_ref, ut_ref, v_ref, w_ref, o_ref, ga_s, gb_s, *, rows, n_chunks):
    k = pl.program_id(1)
    base = (k % (SUBLANE // rows)) * rows

    def gates(dst):
        s = jnp.dot(x_ref[...], ut_ref[...], preferred_element_type=F32)
        for il in range(rows):
            wg = w_ref[:, base + il, :]
            sl = s[:, il * N_KEYS:(il + 1) * N_KEYS]
            g = 0.5 * sl * (1.0 + lax.erf(sl * INV_SQRT2)) * wg
            dst[:, il * N_KEYS:(il + 1) * N_KEYS] = g.astype(BF16)

    def mix(src):
        return jnp.dot(src[...], v_ref[...], preferred_element_type=F32)

    inner = jnp.logical_and(k > 0, k < n_chunks)

    @pl.when(k == 0)
    def _():
        o_ref[...] = jnp.zeros_like(o_ref)
        gates(ga_s)

    @pl.when(jnp.logical_and(inner, k % 2 == 1))
    def _():
        gates(gb_s)
        o_ref[...] += mix(ga_s)

    @pl.when(jnp.logical_and(inner, k % 2 == 0))
    def _():
        gates(ga_s)
        o_ref[...] += mix(gb_s)

    @pl.when(k == n_chunks)
    def _():
        o_ref[...] += mix(gb_s if n_chunks % 2 == 0 else ga_s)


def _peer_mix(h2, u_t, v, wmap, tn=1024):
    m, d = h2.shape
    n_exp = v.shape[0]
    tn = _tile(m, tn)
    rows = PEER_CHUNK_ROWS
    ec = rows * N_KEYS
    n_chunks = n_exp // ec
    assert u_t.shape == (n_chunks, d, ec)
    per = SUBLANE // rows
    last = n_chunks - 1
    return pl.pallas_call(
        functools.partial(_peer_kernel, rows=rows, n_chunks=n_chunks),
        grid=(m // tn, n_chunks + 1),
        in_specs=[pl.BlockSpec((tn, d), lambda t, k: (t, 0), pipeline_mode=pl.Buffered(1)),
                  pl.BlockSpec((None, d, ec), lambda t, k: (jnp.minimum(k, last), 0, 0)),
                  pl.BlockSpec((ec, d), lambda t, k: (jnp.maximum(k - 1, 0), 0)),
                  pl.BlockSpec((None, tn, SUBLANE, N_KEYS), lambda t, k: (jnp.minimum(k, last) // per, t, 0, 0))],
        out_specs=pl.BlockSpec((tn, d), lambda t, k: (t, 0), pipeline_mode=pl.Buffered(1)),
        out_shape=jax.ShapeDtypeStruct((m, d), F32),
        scratch_shapes=[pltpu.VMEM((tn, ec), BF16), pltpu.VMEM((tn, ec), BF16)],
        compiler_params=_params("parallel", "arbitrary"),
        name="peer_mix",
    )(h2, u_t, v, wmap)


def _final_kernel(x_ref, p_ref, m_ref, g_ref, o_ref, *, gate_row):
    x = x_ref[...] + m_ref[0][gate_row:gate_row + 1] * p_ref[...]
    o_ref[...] = _rms(x, g_ref[...])


def _final(x, po, mods, g_final, rows_per_batch, gate_row):
    m, d = x.shape
    tm = _tile(rows_per_batch, 256)
    per = rows_per_batch // tm
    row = lambda i: (i, 0)
    return pl.pallas_call(
        functools.partial(_final_kernel, gate_row=gate_row),
        grid=(m // tm,),
        in_specs=[pl.BlockSpec((tm, d), row), pl.BlockSpec((tm, d), row),
                  pl.BlockSpec((1, 8, d), lambda i: (i // per, 0, 0)),
                  pl.BlockSpec((1, d), lambda i: (0, 0))],
        out_specs=pl.BlockSpec((tm, d), row),
        out_shape=jax.ShapeDtypeStruct((m, d), F32),
        compiler_params=_params("parallel"),
        name="final_norm",
    )(x, po, mods, g_final.reshape(1, d))


def _rope_tables(n_tokens, dim):
    n_rows = n_tokens // GRID_W
    rows = jnp.broadcast_to(jnp.arange(n_rows)[:, None], (n_rows, GRID_W)).reshape(-1).astype(F32)
    cols = jnp.broadcast_to(jnp.arange(GRID_W)[None, :], (n_rows, GRID_W)).reshape(-1).astype(F32)
    n_freq = dim // 4
    freqs = ROPE_BASE ** (-jnp.arange(n_freq, dtype=F32) / n_freq)
    ang = jnp.concatenate([rows[:, None] * freqs, cols[:, None] * freqs], axis=-1)
    cos, sin = jnp.cos(ang), jnp.sin(ang)
    cos2 = jnp.repeat(cos, 2, axis=-1)
    sin2 = jnp.stack([-sin, sin], axis=-1).reshape(n_tokens, dim)
    return cos2, sin2


def _layer_weights(w_in, w_q_b, w_kv_b, w_mla_o, w_gqa_o, w_out, w_peer_q, sub_keys, peer_u, peer_v):
    q_rank, kv_rank = w_q_b.shape[0], w_kv_b.shape[0]
    k0 = q_rank + kv_rank
    lw = {}
    g0 = k0 + MLA_ROPE + (GQA_HEADS + 2 * GQA_KV_HEADS) * HEAD_DIM
    lw['w_in'] = jnp.concatenate([w_in[:, :k0], w_in[:, k0 + MLA_ROPE:g0]], axis=1).astype(BF16)
    lw['w_gates'] = w_in[:, g0:].astype(BF16)
    lw['w_kr'] = jnp.pad(w_in[:, k0:k0 + MLA_ROPE], ((0, 0), (0, LANE - MLA_ROPE))).astype(BF16)
    wq = w_q_b.reshape(q_rank, MLA_HEADS, MLA_NOPE + MLA_ROPE)
    wq = jnp.pad(wq, ((0, 0), (0, 0), (0, MLA_QPAD - MLA_NOPE - MLA_ROPE)))
    lw['w_q_b'] = wq.reshape(q_rank, MLA_HEADS * MLA_QPAD).astype(BF16)
    wkv = w_kv_b.reshape(kv_rank, MLA_HEADS, MLA_NOPE + MLA_V)
    wk = jnp.zeros((kv_rank + LANE, MLA_HEADS, MLA_QPAD), F32)
    wk = wk.at[:kv_rank, :, :MLA_NOPE].set(wkv[:, :, :MLA_NOPE])
    eye = jnp.broadcast_to(jnp.eye(MLA_ROPE, dtype=F32)[:, None, :], (MLA_ROPE, MLA_HEADS, MLA_ROPE))
    wk = wk.at[kv_rank:kv_rank + MLA_ROPE, :, MLA_NOPE:MLA_NOPE + MLA_ROPE].set(eye)
    wv = jnp.pad(wkv[:, :, MLA_NOPE:], ((0, LANE), (0, 0), (0, 0)))
    lw['w_kv'] = jnp.concatenate([wk.reshape(kv_rank + LANE, -1), wv.reshape(kv_rank + LANE, -1)],
                                 axis=1).astype(BF16)
    lw['w_mla_o'] = w_mla_o.astype(BF16)
    lw['w_gqa_o'] = w_gqa_o.astype(BF16)
    lw['w_out'] = w_out.astype(BF16)
    lw['w_peer_q'] = w_peer_q.astype(BF16)
    lw['sub_keys'] = sub_keys.astype(BF16)
    chunk = PEER_CHUNK_ROWS * N_KEYS
    lw['peer_ut'] = peer_u.reshape(-1, chunk, peer_u.shape[1]).transpose(0, 2, 1).astype(BF16)
    lw['peer_v'] = peer_v.astype(BF16)
    return lw


def _layer(x, mods, lw, gains, rope, ctx):
    b, t, d = x.shape
    m = b * t
    x2 = x.reshape(m, d)
    g_norm1, g_norm2, g_q_a, g_kv_a, g_gqa_q, g_gqa_k = gains
    q_rank, kv_rank = g_q_a.shape[0], g_kv_a.shape[0]
    nq, nk = GQA_HEADS * HEAD_DIM, GQA_KV_HEADS * HEAD_DIM
    col_gate_mla = q_rank + kv_rank + nq + 2 * nk

    rpb = t if mods.shape[0] == b else m
    h = _norm_mod(x2, g_norm1, mods, rpb, SHIFT1, SCALE1)
    proj = _mm(h, lw['w_in'], F32)
    gates = _mm(h, lw['w_gates'], BF16, sigmoid=True, tn=1024)
    kr = _mm(h, lw['w_kr'], F32)
    prep_rope = None if rope is None else rope[:4]
    qa, ckv32, ckvx, gq, gk32, gk, gv = _prep(proj, kr, g_q_a, g_kv_a, g_gqa_q, g_gqa_k, prep_rope, t)

    mla_scale = (MLA_NOPE + MLA_ROPE) ** -0.5 * LOG2E
    if rope is None:
        q_mla = _mm(qa, lw['w_q_b'], BF16, scale=mla_scale)
    else:
        q_mla = _mm(qa, lw['w_q_b'], BF16, scale=mla_scale, rope=rope[4:], rows_per_batch=t)

    kv = _mm(ckvx, lw['w_kv'], BF16, tn=2048).reshape(b, t, -1)
    mla_kv = [(kv, kv)]
    gqa_kv = [(gk.reshape(b, t, nk), gv.reshape(b, t, nk))]
    if ctx is not None:
        c_ckv, c_kr, c_k, c_v = ctx
        past = c_ckv.shape[1]
        c_x = jnp.concatenate([c_ckv, c_kr, jnp.zeros((b, past, LANE - MLA_ROPE), F32)], axis=-1).astype(BF16)
        c_kv = _mm(c_x.reshape(b * past, -1), lw['w_kv'], BF16, tn=2048).reshape(b, past, -1)
        mla_kv.append((c_kv, c_kv))
        gqa_kv.append((c_k.reshape(b, past, nk).astype(BF16), c_v.reshape(b, past, nk).astype(BF16)))

    o_mla = _attention(q_mla.reshape(b, t, -1), mla_kv, MLA_HEADS, 1, MLA_QPAD, MLA_V,
                       0, MLA_HEADS * MLA_QPAD, tq=4096)
    o_gqa = _attention(gq.reshape(b, t, -1), gqa_kv, GQA_KV_HEADS, GQA_HEADS // GQA_KV_HEADS,
                       HEAD_DIM, HEAD_DIM, 0, 0, tq=1024)
    merged = _merge(o_mla.reshape(m, -1), o_gqa.reshape(m, -1), lw['w_mla_o'], lw['w_gqa_o'],
                    gates)
    x1 = _mm_residual(merged, lw['w_out'], x2, mods, rpb, GATE1)

    h2 = _norm_mod(x1, g_norm2, mods, rpb, SHIFT2, SCALE2)
    qp = _mm(h2, lw['w_peer_q'], BF16, tn=1024)
    idx_i, idx_j, gates = _route(qp, lw['sub_keys'])
    wmap = _wbuild(idx_i, idx_j, gates)
    po = _peer_mix(h2, lw['peer_ut'], lw['peer_v'], wmap)
    own = (ckv32.reshape(b, t, kv_rank), kr[:, :MLA_ROPE].reshape(b, t, MLA_ROPE),
           gk32.reshape(b, t, GQA_KV_HEADS, HEAD_DIM),
           proj[:, col_gate_mla - nk:col_gate_mla].reshape(b, t, GQA_KV_HEADS, HEAD_DIM))
    return x1, po, own


def kernel(x_prompt, x_sample, c, cache_mla_ckv, cache_mla_krope, cache_gqa_k, cache_gqa_v, c_ctx, w_mod, b_mod, g_norm1, g_norm2, w_in, g_q_a, w_q_b, g_kv_a, w_kv_b, g_gqa_q, g_gqa_k, w_mla_o, w_gqa_o, w_out, w_peer_q, peer_sub_keys, peer_u, peer_v, g_final):
    assert w_in.shape[0] == 1, "one trunk layer is supported"
    d = x_prompt.shape[-1]
    bc, tc = x_prompt.shape[:2]
    bl, tl = x_sample.shape[:2]

    cg, sg = _rope_tables(tl, HEAD_DIM)
    cm, sm = _rope_tables(tl, MLA_ROPE)
    one = jnp.ones((tl, LANE - MLA_ROPE), F32)
    zero = jnp.zeros((tl, LANE - MLA_ROPE), F32)
    cm_k, sm_k = jnp.concatenate([cm, one], 1), jnp.concatenate([sm, zero], 1)
    ones_n, zeros_n = jnp.ones((tl, MLA_NOPE), F32), jnp.zeros((tl, MLA_NOPE), F32)
    cq = jnp.tile(jnp.concatenate([ones_n, cm, one], 1), (1, 2))
    sq = jnp.tile(jnp.concatenate([zeros_n, sm, zero], 1), (1, 2))
    rope = (cg, sg, cm_k, sm_k, cq, sq)

    n_rows = bl + 1
    pad_rows = -n_rows % 8
    cvecs = jnp.concatenate([c, c_ctx[None, :], jnp.zeros((pad_rows, d), F32)], axis=0)

    lw = _layer_weights(w_in[0], w_q_b[0], w_kv_b[0], w_mla_o[0], w_gqa_o[0], w_out[0],
                        w_peer_q[0], peer_sub_keys[0], peer_u[0], peer_v[0])
    gains = (g_norm1[0], g_norm2[0], g_q_a[0], g_kv_a[0], g_gqa_q[0], g_gqa_k[0])
    mod = _modulation(cvecs, w_mod[0], b_mod[0]).reshape(n_rows + pad_rows, N_MOD, d)
    mod = jnp.pad(mod, ((0, 0), (0, 8 - N_MOD), (0, 0)))
    mods_lat, mods_ctx = mod[:bl], mod[bl:bl + 1]

    x1, po, own = _layer(x_prompt, mods_ctx, lw, gains, None, None)
    y_prompt = _final(x1, po, mods_ctx, g_final, bc * tc, GATE2).reshape(bc, tc, d)

    ctx = (cache_mla_ckv[:, 0], cache_mla_krope[:, 0], cache_gqa_k[:, 0], cache_gqa_v[:, 0])
    x1, po, _ = _layer(x_sample, mods_lat, lw, gains, rope, ctx)
    y_sample = _final(x1, po, mods_lat, g_final, tl, GATE2).reshape(bl, tl, d)

    return (y_prompt, y_sample) + tuple(o[:, None] for o in own)
```

```python
import functools
import math

import jax
import jax.numpy as jnp
from jax import lax
from jax.experimental import pallas as pl
from jax.experimental.pallas import tpu as pltpu

F32 = jnp.float32
BF16 = jnp.bfloat16

GRID_W = 64
EPS = 1e-6
ROPE_BASE = 10000.0
N_MOD = 6
MLA_HEADS = 16
MLA_NOPE = 128
MLA_ROPE = 64
MLA_V = 128
MLA_QPAD = 256
GQA_HEADS = 16
GQA_KV_HEADS = 4
HEAD_DIM = 128
PEER_HEADS = 8
N_KEYS = 128
PEER_TOPK = 16
PEER_CHUNK_ROWS = 4
LANE = 128
SUBLANE = 8
VMEM_LIMIT = 56 * 1024 * 1024
ROW_TILE = 512
INV_SQRT2 = 1.0 / math.sqrt(2.0)
LOG2E = 1.0 / math.log(2.0)

SHIFT1, SCALE1, GATE1, SHIFT2, SCALE2, GATE2 = range(6)


def _tile(dim, pref):
    t = min(dim, pref)
    assert dim % t == 0, (dim, pref)
    return t


def _params(*sem):
    return pltpu.CompilerParams(dimension_semantics=sem, vmem_limit_bytes=VMEM_LIMIT)


def _mod_kernel(c_ref, w_ref, b_ref, o_ref):
    c = c_ref[...]
    a = (c * jax.nn.sigmoid(c)).astype(BF16)
    o_ref[...] = jnp.dot(a, w_ref[...].astype(BF16), preferred_element_type=F32) + b_ref[...]


def _modulation(cvecs, w_mod, b_mod):
    r, d = cvecs.shape
    n = w_mod.shape[1]
    tn = _tile(n, 512)
    return pl.pallas_call(
        _mod_kernel,
        grid=(n // tn,),
        in_specs=[pl.BlockSpec((r, d), lambda j: (0, 0)),
                  pl.BlockSpec((d, tn), lambda j: (0, j)),
                  pl.BlockSpec((1, tn), lambda j: (0, j))],
        out_specs=pl.BlockSpec((r, tn), lambda j: (0, j)),
        out_shape=jax.ShapeDtypeStruct((r, n), F32),
        compiler_params=_params("parallel"),
        name="modulation",
    )(cvecs, w_mod, b_mod.reshape(1, n))


def _norm_mod_kernel(x_ref, g_ref, m_ref, o_ref, *, shift_row, scale_row):
    x = x_ref[...]
    y = x * lax.rsqrt(jnp.mean(x * x, axis=-1, keepdims=True) + EPS) * g_ref[...]
    m = m_ref[0]
    o_ref[...] = (y * (1.0 + m[scale_row:scale_row + 1]) + m[shift_row:shift_row + 1]).astype(BF16)


def _norm_mod(x, g, mods, rows_per_batch, shift_row, scale_row):
    m, d = x.shape
    tm = _tile(rows_per_batch, ROW_TILE)
    per = rows_per_batch // tm
    return pl.pallas_call(
        functools.partial(_norm_mod_kernel, shift_row=shift_row, scale_row=scale_row),
        grid=(m // tm,),
        in_specs=[pl.BlockSpec((tm, d), lambda i: (i, 0)),
                  pl.BlockSpec((1, d), lambda i: (0, 0)),
                  pl.BlockSpec((1, 8, d), lambda i: (i // per, 0, 0))],
        out_specs=pl.BlockSpec((tm, d), lambda i: (i, 0)),
        out_shape=jax.ShapeDtypeStruct((m, d), BF16),
        compiler_params=_params("parallel"),
        name="norm_mod",
    )(x, g.reshape(1, d), mods)


def _swap_pairs(x):
    ax = x.ndim - 1
    n = x.shape[ax]
    lane = lax.broadcasted_iota(jnp.int32, x.shape, ax)
    nxt = pltpu.roll(x, n - 1, axis=ax)
    prv = pltpu.roll(x, 1, axis=ax)
    return jnp.where((lane & 1) == 0, nxt, prv)


def _mm_kernel(a_ref, b_ref, o_ref, *, scale, sigmoid):
    acc = jnp.dot(a_ref[...], b_ref[...], preferred_element_type=F32)
    if scale is not None:
        acc = acc * scale
    if sigmoid:
        acc = jax.nn.sigmoid(acc)
    o_ref[...] = acc.astype(o_ref.dtype)


def _mm_rope_kernel(a_ref, b_ref, cos_ref, sin_ref, o_ref, *, scale):
    acc = jnp.dot(a_ref[...], b_ref[...], preferred_element_type=F32)
    for lo in range(0, acc.shape[1], MLA_QPAD):
        mid, hi = lo + MLA_NOPE, lo + MLA_QPAD
        o_ref[:, lo:mid] = (acc[:, lo:mid] * scale).astype(o_ref.dtype)
        r = acc[:, mid:hi]
        r = r * cos_ref[:, mid:hi] + _swap_pairs(r) * sin_ref[:, mid:hi]
        o_ref[:, mid:hi] = (r * scale).astype(o_ref.dtype)


def _mm(a, b, out_dtype, scale=None, sigmoid=False, rope=None, rows_per_batch=None, tm=1024, tn=512):
    m, k = a.shape
    n = b.shape[1]
    tm = _tile(m if rows_per_batch is None else rows_per_batch, tm)
    tn = _tile(n, tn)
    in_specs = [pl.BlockSpec((tm, k), lambda i, j: (i, 0)),
                pl.BlockSpec((k, tn), lambda i, j: (0, j))]
    args = [a, b]
    if rope is None:
        body = functools.partial(_mm_kernel, scale=scale, sigmoid=sigmoid)
    else:
        per = rows_per_batch // tm
        body = functools.partial(_mm_rope_kernel, scale=scale)
        in_specs += [pl.BlockSpec((tm, tn), lambda i, j: (i % per, 0))] * 2
        args += list(rope)
    return pl.pallas_call(
        body,
        grid=(m // tm, n // tn),
        in_specs=in_specs,
        out_specs=pl.BlockSpec((tm, tn), lambda i, j: (i, j)),
        out_shape=jax.ShapeDtypeStruct((m, n), out_dtype),
        compiler_params=_params("parallel", "parallel"),
        name="matmul",
    )(*args)


def _mm_res_kernel(a_ref, b_ref, x_ref, m_ref, o_ref, *, gate_row):
    acc = jnp.dot(a_ref[...], b_ref[...], preferred_element_type=F32)
    o_ref[...] = x_ref[...] + m_ref[0][gate_row:gate_row + 1] * acc


def _mm_residual(a, b, x, mods, rows_per_batch, gate_row, tm=1024, tn=1024):
    m, k = a.shape
    n = b.shape[1]
    tm = _tile(rows_per_batch, tm)
    tn = _tile(n, tn)
    per = rows_per_batch // tm
    return pl.pallas_call(
        functools.partial(_mm_res_kernel, gate_row=gate_row),
        grid=(m // tm, n // tn),
        in_specs=[pl.BlockSpec((tm, k), lambda i, j: (i, 0)),
                  pl.BlockSpec((k, tn), lambda i, j: (0, j)),
                  pl.BlockSpec((tm, tn), lambda i, j: (i, j)),
                  pl.BlockSpec((1, 8, tn), lambda i, j: (i // per, 0, j))],
        out_specs=pl.BlockSpec((tm, tn), lambda i, j: (i, j)),
        out_shape=jax.ShapeDtypeStruct((m, n), F32),
        compiler_params=_params("parallel", "parallel"),
        name="matmul_residual",
    )(a, b, x, mods)


def _merge_kernel(oa_ref, ob_ref, wa_ref, wb_ref, ga_ref, gb_ref, o_ref):
    ya = jnp.dot(oa_ref[...], wa_ref[...], preferred_element_type=F32)
    yb = jnp.dot(ob_ref[...], wb_ref[...], preferred_element_type=F32)
    o_ref[...] = (ga_ref[...] * ya + gb_ref[...] * yb).astype(o_ref.dtype)


def _merge(o_mla, o_gqa, w_mla_o, w_gqa_o, gates, tm=1024, tn=1024):
    m, k = o_mla.shape
    n = w_mla_o.shape[1]
    tm = _tile(m, tm)
    tn = _tile(n, tn)
    ca, cb = 0, n // tn
    return pl.pallas_call(
        _merge_kernel,
        grid=(m // tm, n // tn),
        in_specs=[pl.BlockSpec((tm, k), lambda i, j: (i, 0)),
                  pl.BlockSpec((tm, k), lambda i, j: (i, 0)),
                  pl.BlockSpec((k, tn), lambda i, j: (0, j)),
                  pl.BlockSpec((k, tn), lambda i, j: (0, j)),
                  pl.BlockSpec((tm, tn), lambda i, j: (i, ca + j)),
                  pl.BlockSpec((tm, tn), lambda i, j: (i, cb + j))],
        out_specs=pl.BlockSpec((tm, tn), lambda i, j: (i, j)),
        out_shape=jax.ShapeDtypeStruct((m, n), BF16),
        compiler_params=_params("parallel", "parallel"),
        name="merge",
    )(o_mla, o_gqa, w_mla_o, w_gqa_o, gates, gates)


def _rms(x, g):
    return x * lax.rsqrt(jnp.mean(x * x, axis=-1, keepdims=True) + EPS) * g


def _prep_kernel(*refs, q_rank, kv_rank, use_rope):
    if use_rope:
        (p_ref, kr_ref, gqa_ref, gkv_ref, ggq_ref, ggk_ref, cg_ref, sg_ref, cm_ref, sm_ref,
         qa_o, ckv_o, ckvx_o, gq_o, gk32_o, gk_o, gv_o) = refs
    else:
        (p_ref, kr_ref, gqa_ref, gkv_ref, ggq_ref, ggk_ref,
         qa_o, ckv_o, ckvx_o, gq_o, gk32_o, gk_o, gv_o) = refs
    c0 = q_rank
    c1 = c0 + kv_rank
    c2 = c1 + GQA_HEADS * HEAD_DIM
    c3 = c2 + GQA_KV_HEADS * HEAD_DIM
    qa_o[...] = _rms(p_ref[:, 0:c0], gqa_ref[...]).astype(BF16)
    ckv = _rms(p_ref[:, c0:c1], gkv_ref[...])
    ckv_o[...] = ckv
    ckvx_o[:, 0:kv_rank] = ckv.astype(BF16)
    kr = kr_ref[...]
    if use_rope:
        kr = kr * cm_ref[...] + _swap_pairs(kr) * sm_ref[...]
    ckvx_o[:, kv_rank:kv_rank + LANE] = kr.astype(BF16)
    q_scale = HEAD_DIM ** -0.5 * LOG2E
    for h in range(GQA_HEADS):
        x = _rms(p_ref[:, c1 + h * HEAD_DIM:c1 + (h + 1) * HEAD_DIM], ggq_ref[...])
        if use_rope:
            x = x * cg_ref[...] + _swap_pairs(x) * sg_ref[...]
        gq_o[:, h * HEAD_DIM:(h + 1) * HEAD_DIM] = (x * q_scale).astype(BF16)
    for h in range(GQA_KV_HEADS):
        x = _rms(p_ref[:, c2 + h * HEAD_DIM:c2 + (h + 1) * HEAD_DIM], ggk_ref[...])
        if use_rope:
            x = x * cg_ref[...] + _swap_pairs(x) * sg_ref[...]
        gk32_o[:, h * HEAD_DIM:(h + 1) * HEAD_DIM] = x
        gk_o[:, h * HEAD_DIM:(h + 1) * HEAD_DIM] = x.astype(BF16)
    gv_o[...] = p_ref[:, c3:c3 + GQA_KV_HEADS * HEAD_DIM].astype(BF16)


def _prep(proj, kr, g_q_a, g_kv_a, g_gqa_q, g_gqa_k, rope, rows_per_batch):
    m = proj.shape[0]
    q_rank, kv_rank = g_q_a.shape[0], g_kv_a.shape[0]
    width = q_rank + kv_rank + (GQA_HEADS + 2 * GQA_KV_HEADS) * HEAD_DIM
    tm = _tile(rows_per_batch, ROW_TILE)
    per = rows_per_batch // tm
    nq, nk = GQA_HEADS * HEAD_DIM, GQA_KV_HEADS * HEAD_DIM
    row = lambda i: (i, 0)
    const = lambda i: (0, 0)
    in_specs = [pl.BlockSpec((tm, width), row), pl.BlockSpec((tm, LANE), row),
                pl.BlockSpec((1, q_rank), const), pl.BlockSpec((1, kv_rank), const),
                pl.BlockSpec((1, HEAD_DIM), const), pl.BlockSpec((1, HEAD_DIM), const)]
    args = [proj, kr, g_q_a.reshape(1, -1), g_kv_a.reshape(1, -1), g_gqa_q.reshape(1, -1), g_gqa_k.reshape(1, -1)]
    if rope is not None:
        in_specs += [pl.BlockSpec((tm, LANE), lambda i: (i % per, 0))] * 4
        args += list(rope)
    outs = [((m, q_rank), BF16), ((m, kv_rank), F32), ((m, kv_rank + LANE), BF16), ((m, nq), BF16),
            ((m, nk), F32), ((m, nk), BF16), ((m, nk), BF16)]
    return pl.pallas_call(
        functools.partial(_prep_kernel, q_rank=q_rank, kv_rank=kv_rank, use_rope=rope is not None),
        grid=(m // tm,),
        in_specs=in_specs,
        out_specs=[pl.BlockSpec((tm, s[1]), row) for s, _ in outs],
        out_shape=[jax.ShapeDtypeStruct(s, dt) for s, dt in outs],
        compiler_params=_params("parallel"),
        name="prep",
    )(*args)


def _attn_kernel(q_ref, *refs, rep, dq, dv, sub):
    o_ref = refs[-1]
    ks = [r[...] for r in refs[0:-1:2]]
    vs = [r[...] for r in refs[1:-1:2]]
    for r in range(rep):
        for i in range(q_ref.shape[0] // sub):
            rows = slice(i * sub, (i + 1) * sub)
            q = q_ref[rows, r * dq:(r + 1) * dq]
            ss = [lax.dot_general(q, k, (((1,), (1,)), ((), ())), preferred_element_type=F32) for k in ks]
            s = ss[0] if len(ss) == 1 else jnp.concatenate(ss, axis=1)
            p = jnp.exp2(s - jnp.max(s, axis=-1, keepdims=True))
            l = jnp.sum(p, axis=-1, keepdims=True)
            p = p.astype(BF16)
            o, off = None, 0
            for v in vs:
                c = jnp.dot(p[:, off:off + v.shape[0]], v, preferred_element_type=F32)
                o = c if o is None else o + c
                off += v.shape[0]
            o_ref[rows, r * dv:(r + 1) * dv] = (o / l).astype(o_ref.dtype)


def _attention(q, kvs, n_groups, rep, dq, dv, k_col0, v_col0, tq):
    b, t, _ = q.shape
    tq = _tile(t, tq)
    sub = _tile(tq, 256)
    kb, vb = k_col0 // dq, v_col0 // dv
    assert kb * dq == k_col0 and vb * dv == v_col0
    in_specs = [pl.BlockSpec((None, tq, rep * dq), lambda bi, g, qi: (bi, qi, g))]
    args = [q]
    for k, v in kvs:
        in_specs += [pl.BlockSpec((None, k.shape[1], dq), lambda bi, g, qi: (bi, 0, kb + g)),
                     pl.BlockSpec((None, v.shape[1], dv), lambda bi, g, qi: (bi, 0, vb + g))]
        args += [k, v]
    return pl.pallas_call(
        functools.partial(_attn_kernel, rep=rep, dq=dq, dv=dv, sub=sub),
        grid=(b, n_groups, t // tq),
        in_specs=in_specs,
        out_specs=pl.BlockSpec((None, tq, rep * dv), lambda bi, g, qi: (bi, qi, g)),
        out_shape=jax.ShapeDtypeStruct((b, t, n_groups * rep * dv), BF16),
        compiler_params=_params("parallel", "parallel", "parallel"),
        name="attention",
    )(*args)


def _top16(s, idx):
    n, c = s.shape
    row = lax.broadcasted_iota(jnp.int32, (PEER_TOPK, c), 0)
    big = float(PEER_TOPK * PEER_TOPK)

    def body(r, carry):
        s, vals, idxs = carry
        mx = jnp.max(s, axis=0, keepdims=True)
        am = jnp.min(jnp.where(s == mx, idx, big), axis=0, keepdims=True)
        vals = jnp.where(row == r, mx, vals)
        idxs = jnp.where(row == r, am, idxs)
        s = jnp.where(idx == am, -jnp.inf, s)
        return s, vals, idxs

    init = (s, jnp.zeros((PEER_TOPK, c), F32), jnp.zeros((PEER_TOPK, c), F32))
    _, vals, idxs = lax.fori_loop(0, PEER_TOPK, body, init)
    return vals, idxs


def _pick(table, pos):
    out = jnp.zeros(pos.shape, F32)
    for a in range(PEER_TOPK):
        out = jnp.where(pos == a, table[a:a + 1], out)
    return out


_CAND_ROWS = [(0, 0), (0, 8), (1, 0), (2, 0), (3, 0), (4, 0), (5, 0), (6, 0), (7, 0)]


def _route_kernel(q_ref, k_ref, oi_ref, oj_ref, og_ref):
    q = q_ref[...]
    tn = q.shape[0]
    half = q.shape[1] // 2
    nt = (((1,), (1,)), ((), ()))
    s0 = lax.dot_general(k_ref[0], q[:, :half], nt, preferred_element_type=F32)
    s1 = lax.dot_general(k_ref[1], q[:, half:], nt, preferred_element_type=F32)
    s01 = jnp.concatenate([s0, s1], axis=1)
    v01, i01 = _top16(s01, lax.broadcasted_iota(jnp.int32, s01.shape, 0).astype(F32))
    v0, v1, i0, i1 = v01[:, :tn], v01[:, tn:], i01[:, :tn], i01[:, tn:]
    sub = lax.broadcasted_iota(jnp.int32, (SUBLANE, tn), 0).astype(F32)
    cand = [v0[a:a + 1] + v1[b:b + SUBLANE] for a, b in _CAND_ROWS]
    flat = [sub + float(a * PEER_TOPK + b) for a, b in _CAND_ROWS]
    cand.append(v0[SUBLANE:] + v1[0:1])
    flat.append((sub + float(SUBLANE)) * float(PEER_TOPK))
    best, pos = _top16(jnp.concatenate(cand, axis=0), jnp.concatenate(flat, axis=0))
    pos = pos.astype(jnp.int32)
    e = jnp.exp(best - best[0:1])
    oi_ref[...] = _pick(i0, pos >> 4)
    oj_ref[...] = _pick(i1, pos & (PEER_TOPK - 1))
    og_ref[...] = e / jnp.sum(e, axis=0, keepdims=True)


def _route(qp, sub_keys, tn=512):
    m = qp.shape[0]
    tn = _tile(m, tn)
    qd = qp.shape[1] // PEER_HEADS
    shp = jax.ShapeDtypeStruct((PEER_HEADS * PEER_TOPK, m), F32)
    spec = pl.BlockSpec((PEER_TOPK, tn), lambda t, h: (h, t))
    return pl.pallas_call(
        _route_kernel,
        grid=(m // tn, PEER_HEADS),
        in_specs=[pl.BlockSpec((tn, qd), lambda t, h: (t, h)),
                  pl.BlockSpec(sub_keys.shape, lambda t, h: (0, 0, 0))],
        out_specs=[spec, spec, spec],
        out_shape=[shp, shp, shp],
        compiler_params=_params("parallel", "parallel"),
        name="peer_route",
    )(qp, sub_keys)


def _wbuild_kernel(i_ref, j_ref, g_ref, o_ref, it_s, jt_s, gt_s):
    it_s[...] = i_ref[...].T
    jt_s[...] = j_ref[...].T
    gt_s[...] = g_ref[...].T
    ne = i_ref.shape[0]
    iota_a = lax.broadcasted_iota(jnp.int32, (2 * N_KEYS, 2 * ne), 0).astype(F32).astype(BF16)
    iota_g = lax.broadcasted_iota(jnp.int32, (N_KEYS, 2 * ne), 0).astype(F32).astype(BF16)
    one, zero = jnp.ones((), BF16), jnp.zeros((), BF16)

    def body(m, carry):
        ri = it_s[pl.ds(2 * m, 2), :]
        rj = jt_s[pl.ds(2 * m, 2), :]
        rg = gt_s[pl.ds(2 * m, 2), :]
        row_i = jnp.concatenate([ri[0:1], ri[1:2] + float(N_KEYS)], axis=1).astype(BF16)
        row_j = jnp.concatenate([rj[0:1], rj[1:2]], axis=1).astype(BF16)
        row_g = jnp.concatenate([rg[0:1], rg[1:2]], axis=1).astype(BF16)
        a2 = jnp.where(iota_a == row_i, one, zero)
        g2 = jnp.where(iota_g == row_j, row_g, zero)
        w = lax.dot_general(a2, g2, (((1,), (1,)), ((), ())), preferred_element_type=F32)
        for t in range(2):
            o_ref[:, 2 * m + t] = w[t * N_KEYS:(t + 1) * N_KEYS].reshape(N_KEYS // SUBLANE, SUBLANE, N_KEYS)
        return carry

    lax.fori_loop(0, o_ref.shape[1] // 2, body, 0, unroll=16)


def _wbuild(idx_i, idx_j, gates):
    ne, m = idx_i.shape
    tw = LANE
    spec = pl.BlockSpec((ne, tw), lambda t: (0, t))
    return pl.pallas_call(
        _wbuild_kernel,
        grid=(m // tw,),
        in_specs=[spec, spec, spec],
        out_specs=pl.BlockSpec((N_KEYS // SUBLANE, tw, SUBLANE, N_KEYS), lambda t: (0, t, 0, 0)),
        out_shape=jax.ShapeDtypeStruct((N_KEYS // SUBLANE, m, SUBLANE, N_KEYS), F32),
        scratch_shapes=[pltpu.VMEM((tw, ne), F32)] * 3,
        compiler_params=_params("parallel"),
        name="peer_gate_map",
    )(idx_i, idx_j, gates)


def _peer_kernel(x_ref, ut_ref, v_ref, w_ref, o_ref, ga_s, gb_s, *, rows, n_chunks):
    k = pl.program_id(1)
    base = (k % (SUBLANE // rows)) * rows

    def gates(dst):
        s = jnp.dot(x_ref[...], ut_ref[...], preferred_element_type=F32)
        for il in range(rows):
            wg = w_ref[:, base + il, :]
            sl = s[:, il * N_KEYS:(il + 1) * N_KEYS]
            g = 0.5 * sl * (1.0 + lax.erf(sl * INV_SQRT2)) * wg
            dst[:, il * N_KEYS:(il + 1) * N_KEYS] = g.astype(BF16)

    def mix(src):
        return jnp.dot(src[...], v_ref[...], preferred_element_type=F32)

    inner = jnp.logical_and(k > 0, k < n_chunks)

    @pl.when(k == 0)
    def _():
        o_ref[...] = jnp.zeros_like(o_ref)
        gates(ga_s)

    @pl.when(jnp.logical_and(inner, k % 2 == 1))
    def _():
        gates(gb_s)
        o_ref[...] += mix(ga_s)

    @pl.when(jnp.logical_and(inner, k % 2 == 0))
    def _():
        gates(ga_s)
        o_ref[...] += mix(gb_s)

    @pl.when(k == n_chunks)
    def _():
        o_ref[...] += mix(gb_s if n_chunks % 2 == 0 else ga_s)


def _peer_mix(h2, u_t, v, wmap, tn=1024):
    m, d = h2.shape
    n_exp = v.shape[0]
    tn = _tile(m, tn)
    rows = PEER_CHUNK_ROWS
    ec = rows * N_KEYS
    n_chunks = n_exp // ec
    assert u_t.shape == (n_chunks, d, ec)
    per = SUBLANE // rows
    last = n_chunks - 1
    return pl.pallas_call(
        functools.partial(_peer_kernel, rows=rows, n_chunks=n_chunks),
        grid=(m // tn, n_chunks + 1),
        in_specs=[pl.BlockSpec((tn, d), lambda t, k: (t, 0), pipeline_mode=pl.Buffered(1)),
                  pl.BlockSpec((None, d, ec), lambda t, k: (jnp.minimum(k, last), 0, 0)),
                  pl.BlockSpec((ec, d), lambda t, k: (jnp.maximum(k - 1, 0), 0)),
                  pl.BlockSpec((None, tn, SUBLANE, N_KEYS), lambda t, k: (jnp.minimum(k, last) // per, t, 0, 0))],
        out_specs=pl.BlockSpec((tn, d), lambda t, k: (t, 0), pipeline_mode=pl.Buffered(1)),
        out_shape=jax.ShapeDtypeStruct((m, d), F32),
        scratch_shapes=[pltpu.VMEM((tn, ec), BF16), pltpu.VMEM((tn, ec), BF16)],
        compiler_params=_params("parallel", "arbitrary"),
        name="peer_mix",
    )(h2, u_t, v, wmap)


def _final_kernel(x_ref, p_ref, m_ref, g_ref, o_ref, *, gate_row):
    x = x_ref[...] + m_ref[0][gate_row:gate_row + 1] * p_ref[...]
    o_ref[...] = _rms(x, g_ref[...])


def _final(x, po, mods, g_final, rows_per_batch, gate_row):
    m, d = x.shape
    tm = _tile(rows_per_batch, ROW_TILE // 2)
    per = rows_per_batch // tm
    row = lambda i: (i, 0)
    return pl.pallas_call(
        functools.partial(_final_kernel, gate_row=gate_row),
        grid=(m // tm,),
        in_specs=[pl.BlockSpec((tm, d), row), pl.BlockSpec((tm, d), row),
                  pl.BlockSpec((1, 8, d), lambda i: (i // per, 0, 0)),
                  pl.BlockSpec((1, d), lambda i: (0, 0))],
        out_specs=pl.BlockSpec((tm, d), row),
        out_shape=jax.ShapeDtypeStruct((m, d), F32),
        compiler_params=_params("parallel"),
        name="final_norm",
    )(x, po, mods, g_final.reshape(1, d))


def _rope_tables(n_tokens, dim):
    n_rows = n_tokens // GRID_W
    rows = jnp.broadcast_to(jnp.arange(n_rows)[:, None], (n_rows, GRID_W)).reshape(-1).astype(F32)
    cols = jnp.broadcast_to(jnp.arange(GRID_W)[None, :], (n_rows, GRID_W)).reshape(-1).astype(F32)
    n_freq = dim // 4
    freqs = ROPE_BASE ** (-jnp.arange(n_freq, dtype=F32) / n_freq)
    ang = jnp.concatenate([rows[:, None] * freqs, cols[:, None] * freqs], axis=-1)
    cos, sin = jnp.cos(ang), jnp.sin(ang)
    cos2 = jnp.repeat(cos, 2, axis=-1)
    sin2 = jnp.stack([-sin, sin], axis=-1).reshape(n_tokens, dim)
    return cos2, sin2


def _layer_weights(w_in, w_q_b, w_kv_b, w_mla_o, w_gqa_o, w_out, w_peer_q, sub_keys, peer_u, peer_v):
    q_rank, kv_rank = w_q_b.shape[0], w_kv_b.shape[0]
    k0 = q_rank + kv_rank
    lw = {}
    g0 = k0 + MLA_ROPE + (GQA_HEADS + 2 * GQA_KV_HEADS) * HEAD_DIM
    lw['w_in'] = jnp.concatenate([w_in[:, :k0], w_in[:, k0 + MLA_ROPE:g0]], axis=1).astype(BF16)
    lw['w_gates'] = w_in[:, g0:].astype(BF16)
    lw['w_kr'] = jnp.pad(w_in[:, k0:k0 + MLA_ROPE], ((0, 0), (0, LANE - MLA_ROPE))).astype(BF16)
    wq = w_q_b.reshape(q_rank, MLA_HEADS, MLA_NOPE + MLA_ROPE)
    wq = jnp.pad(wq, ((0, 0), (0, 0), (0, MLA_QPAD - MLA_NOPE - MLA_ROPE)))
    lw['w_q_b'] = wq.reshape(q_rank, MLA_HEADS * MLA_QPAD).astype(BF16)
    wkv = w_kv_b.reshape(kv_rank, MLA_HEADS, MLA_NOPE + MLA_V)
    wk = jnp.zeros((kv_rank + LANE, MLA_HEADS, MLA_QPAD), F32)
    wk = wk.at[:kv_rank, :, :MLA_NOPE].set(wkv[:, :, :MLA_NOPE])
    eye = jnp.broadcast_to(jnp.eye(MLA_ROPE, dtype=F32)[:, None, :], (MLA_ROPE, MLA_HEADS, MLA_ROPE))
    wk = wk.at[kv_rank:kv_rank + MLA_ROPE, :, MLA_NOPE:MLA_NOPE + MLA_ROPE].set(eye)
    wv = jnp.pad(wkv[:, :, MLA_NOPE:], ((0, LANE), (0, 0), (0, 0)))
    lw['w_kv'] = jnp.concatenate([wk.reshape(kv_rank + LANE, -1), wv.reshape(kv_rank + LANE, -1)],
                                 axis=1).astype(BF16)
    lw['w_mla_o'] = w_mla_o.astype(BF16)
    lw['w_gqa_o'] = w_gqa_o.astype(BF16)
    lw['w_out'] = w_out.astype(BF16)
    lw['w_peer_q'] = w_peer_q.astype(BF16)
    lw['sub_keys'] = sub_keys.astype(BF16)
    chunk = PEER_CHUNK_ROWS * N_KEYS
    lw['peer_ut'] = peer_u.reshape(-1, chunk, peer_u.shape[1]).transpose(0, 2, 1).astype(BF16)
    lw['peer_v'] = peer_v.astype(BF16)
    return lw


def _layer(x, mods, lw, gains, rope, ctx):
    b, t, d = x.shape
    m = b * t
    x2 = x.reshape(m, d)
    g_norm1, g_norm2, g_q_a, g_kv_a, g_gqa_q, g_gqa_k = gains
    q_rank, kv_rank = g_q_a.shape[0], g_kv_a.shape[0]
    nq, nk = GQA_HEADS * HEAD_DIM, GQA_KV_HEADS * HEAD_DIM
    col_gate_mla = q_rank + kv_rank + nq + 2 * nk

    rpb = t if mods.shape[0] == b else m
    h = _norm_mod(x2, g_norm1, mods, rpb, SHIFT1, SCALE1)
    proj = _mm(h, lw['w_in'], F32)
    gates = _mm(h, lw['w_gates'], BF16, sigmoid=True, tn=1024)
    kr = _mm(h, lw['w_kr'], F32)
    prep_rope = None if rope is None else rope[:4]
    qa, ckv32, ckvx, gq, gk32, gk, gv = _prep(proj, kr, g_q_a, g_kv_a, g_gqa_q, g_gqa_k, prep_rope, t)

    mla_scale = (MLA_NOPE + MLA_ROPE) ** -0.5 * LOG2E
    if rope is None:
        q_mla = _mm(qa, lw['w_q_b'], BF16, scale=mla_scale)
    else:
        q_mla = _mm(qa, lw['w_q_b'], BF16, scale=mla_scale, rope=rope[4:], rows_per_batch=t)

    kv = _mm(ckvx, lw['w_kv'], BF16, tn=2048).reshape(b, t, -1)
    mla_kv = [(kv, kv)]
    gqa_kv = [(gk.reshape(b, t, nk), gv.reshape(b, t, nk))]
    if ctx is not None:
        c_ckv, c_kr, c_k, c_v = ctx
        past = c_ckv.shape[1]
        c_x = jnp.concatenate([c_ckv, c_kr, jnp.zeros((b, past, LANE - MLA_ROPE), F32)], axis=-1).astype(BF16)
        c_kv = _mm(c_x.reshape(b * past, -1), lw['w_kv'], BF16, tn=2048).reshape(b, past, -1)
        mla_kv.append((c_kv, c_kv))
        gqa_kv.append((c_k.reshape(b, past, nk).astype(BF16), c_v.reshape(b, past, nk).astype(BF16)))

    o_mla = _attention(q_mla.reshape(b, t, -1), mla_kv, MLA_HEADS, 1, MLA_QPAD, MLA_V,
                       0, MLA_HEADS * MLA_QPAD, tq=4096)
    o_gqa = _attention(gq.reshape(b, t, -1), gqa_kv, GQA_KV_HEADS, GQA_HEADS // GQA_KV_HEADS,
                       HEAD_DIM, HEAD_DIM, 0, 0, tq=1024)
    merged = _merge(o_mla.reshape(m, -1), o_gqa.reshape(m, -1), lw['w_mla_o'], lw['w_gqa_o'],
                    gates)
    x1 = _mm_residual(merged, lw['w_out'], x2, mods, rpb, GATE1)

    h2 = _norm_mod(x1, g_norm2, mods, rpb, SHIFT2, SCALE2)
    qp = _mm(h2, lw['w_peer_q'], BF16, tn=1024)
    idx_i, idx_j, gates = _route(qp, lw['sub_keys'])
    wmap = _wbuild(idx_i, idx_j, gates)
    po = _peer_mix(h2, lw['peer_ut'], lw['peer_v'], wmap)
    own = (ckv32.reshape(b, t, kv_rank), kr[:, :MLA_ROPE].reshape(b, t, MLA_ROPE),
           gk32.reshape(b, t, GQA_KV_HEADS, HEAD_DIM),
           proj[:, col_gate_mla - nk:col_gate_mla].reshape(b, t, GQA_KV_HEADS, HEAD_DIM))
    return x1, po, own


def kernel(x_prompt, x_sample, c, cache_mla_ckv, cache_mla_krope, cache_gqa_k, cache_gqa_v, c_ctx, w_mod, b_mod, g_norm1, g_norm2, w_in, g_q_a, w_q_b, g_kv_a, w_kv_b, g_gqa_q, g_gqa_k, w_mla_o, w_gqa_o, w_out, w_peer_q, peer_sub_keys, peer_u, peer_v, g_final):
    assert w_in.shape[0] == 1, "one trunk layer is supported"
    d = x_prompt.shape[-1]
    bc, tc = x_prompt.shape[:2]
    bl, tl = x_sample.shape[:2]

    cg, sg = _rope_tables(tl, HEAD_DIM)
    cm, sm = _rope_tables(tl, MLA_ROPE)
    one = jnp.ones((tl, LANE - MLA_ROPE), F32)
    zero = jnp.zeros((tl, LANE - MLA_ROPE), F32)
    cm_k, sm_k = jnp.concatenate([cm, one], 1), jnp.concatenate([sm, zero], 1)
    ones_n, zeros_n = jnp.ones((tl, MLA_NOPE), F32), jnp.zeros((tl, MLA_NOPE), F32)
    cq = jnp.tile(jnp.concatenate([ones_n, cm, one], 1), (1, 2))
    sq = jnp.tile(jnp.concatenate([zeros_n, sm, zero], 1), (1, 2))
    rope = (cg, sg, cm_k, sm_k, cq, sq)

    n_rows = bl + 1
    pad_rows = -n_rows % 8
    cvecs = jnp.concatenate([c, c_ctx[None, :], jnp.zeros((pad_rows, d), F32)], axis=0)

    lw = _layer_weights(w_in[0], w_q_b[0], w_kv_b[0], w_mla_o[0], w_gqa_o[0], w_out[0],
                        w_peer_q[0], peer_sub_keys[0], peer_u[0], peer_v[0])
    gains = (g_norm1[0], g_norm2[0], g_q_a[0], g_kv_a[0], g_gqa_q[0], g_gqa_k[0])
    mod = _modulation(cvecs, w_mod[0], b_mod[0]).reshape(n_rows + pad_rows, N_MOD, d)
    mod = jnp.pad(mod, ((0, 0), (0, 8 - N_MOD), (0, 0)))
    mods_lat, mods_ctx = mod[:bl], mod[bl:bl + 1]

    x1, po, own = _layer(x_prompt, mods_ctx, lw, gains, None, None)
    y_prompt = _final(x1, po, mods_ctx, g_final, bc * tc, GATE2).reshape(bc, tc, d)

    ctx = (cache_mla_ckv[:, 0], cache_mla_krope[:, 0], cache_gqa_k[:, 0], cache_gqa_v[:, 0])
    x1, po, _ = _layer(x_sample, mods_lat, lw, gains, rope, ctx)
    y_sample = _final(x1, po, mods_lat, g_final, tl, GATE2).reshape(bl, tl, d)

    return (y_prompt, y_sample) + tuple(o[:, None] for o in own)
```

```python
import functools
import math

import jax
import jax.numpy as jnp
from jax import lax
from jax.experimental import pallas as pl
from jax.experimental.pallas import tpu as pltpu

F32 = jnp.float32
BF16 = jnp.bfloat16

GRID_W = 64
EPS = 1e-6
ROPE_BASE = 10000.0
N_MOD = 6
MLA_HEADS = 16
MLA_NOPE = 128
MLA_ROPE = 64
MLA_V = 128
MLA_QPAD = 256
GQA_HEADS = 16
GQA_KV_HEADS = 4
HEAD_DIM = 128
PEER_HEADS = 8
N_KEYS = 128
PEER_TOPK = 16
PEER_CHUNK_ROWS = 4
LANE = 128
SUBLANE = 8
VMEM_LIMIT = 56 * 1024 * 1024
ROW_TILE = 512
INV_SQRT2 = 1.0 / math.sqrt(2.0)
LOG2E = 1.0 / math.log(2.0)

SHIFT1, SCALE1, GATE1, SHIFT2, SCALE2, GATE2 = range(6)


def _tile(dim, pref):
    t = min(dim, pref)
    assert dim % t == 0, (dim, pref)
    return t


def _params(*sem):
    return pltpu.CompilerParams(dimension_semantics=sem, vmem_limit_bytes=VMEM_LIMIT)


def _mod_kernel(c_ref, w_ref, b_ref, o_ref):
    c = c_ref[...]
    a = (c * jax.nn.sigmoid(c)).astype(BF16)
    o_ref[...] = jnp.dot(a, w_ref[...].astype(BF16), preferred_element_type=F32) + b_ref[...]


def _modulation(cvecs, w_mod, b_mod):
    r, d = cvecs.shape
    n = w_mod.shape[1]
    tn = _tile(n, 512)
    return pl.pallas_call(
        _mod_kernel,
        grid=(n // tn,),
        in_specs=[pl.BlockSpec((r, d), lambda j: (0, 0)),
                  pl.BlockSpec((d, tn), lambda j: (0, j)),
                  pl.BlockSpec((1, tn), lambda j: (0, j))],
        out_specs=pl.BlockSpec((r, tn), lambda j: (0, j)),
        out_shape=jax.ShapeDtypeStruct((r, n), F32),
        compiler_params=_params("parallel"),
        name="modulation",
    )(cvecs, w_mod, b_mod.reshape(1, n))


def _norm_mod_kernel(x_ref, g_ref, m_ref, o_ref, *, shift_row, scale_row):
    x = x_ref[...]
    y = x * lax.rsqrt(jnp.mean(x * x, axis=-1, keepdims=True) + EPS) * g_ref[...]
    m = m_ref[0]
    o_ref[...] = (y * (1.0 + m[scale_row:scale_row + 1]) + m[shift_row:shift_row + 1]).astype(BF16)


def _norm_mod(x, g, mods, rows_per_batch, shift_row, scale_row):
    m, d = x.shape
    tm = _tile(rows_per_batch, ROW_TILE)
    per = rows_per_batch // tm
    return pl.pallas_call(
        functools.partial(_norm_mod_kernel, shift_row=shift_row, scale_row=scale_row),
        grid=(m // tm,),
        in_specs=[pl.BlockSpec((tm, d), lambda i: (i, 0)),
                  pl.BlockSpec((1, d), lambda i: (0, 0)),
                  pl.BlockSpec((1, 8, d), lambda i: (i // per, 0, 0))],
        out_specs=pl.BlockSpec((tm, d), lambda i: (i, 0)),
        out_shape=jax.ShapeDtypeStruct((m, d), BF16),
        compiler_params=_params("parallel"),
        name="norm_mod",
    )(x, g.reshape(1, d), mods)


def _swap_pairs(x):
    ax = x.ndim - 1
    n = x.shape[ax]
    lane = lax.broadcasted_iota(jnp.int32, x.shape, ax)
    nxt = pltpu.roll(x, n - 1, axis=ax)
    prv = pltpu.roll(x, 1, axis=ax)
    return jnp.where((lane & 1) == 0, nxt, prv)


def _mm_kernel(a_ref, b_ref, o_ref, *, scale, sigmoid):
    acc = jnp.dot(a_ref[...], b_ref[...], preferred_element_type=F32)
    if scale is not None:
        acc = acc * scale
    if sigmoid:
        acc = jax.nn.sigmoid(acc)
    o_ref[...] = acc.astype(o_ref.dtype)


def _mm_rope_kernel(a_ref, b_ref, cos_ref, sin_ref, o_ref, *, scale):
    acc = jnp.dot(a_ref[...], b_ref[...], preferred_element_type=F32)
    for lo in range(0, acc.shape[1], MLA_QPAD):
        mid, hi = lo + MLA_NOPE, lo + MLA_QPAD
        o_ref[:, lo:mid] = (acc[:, lo:mid] * scale).astype(o_ref.dtype)
        r = acc[:, mid:hi]
        r = r * cos_ref[:, mid:hi] + _swap_pairs(r) * sin_ref[:, mid:hi]
        o_ref[:, mid:hi] = (r * scale).astype(o_ref.dtype)


def _mm(a, b, out_dtype, scale=None, sigmoid=False, rope=None, rows_per_batch=None, tm=1024, tn=512):
    m, k = a.shape
    n = b.shape[1]
    tm = _tile(m if rows_per_batch is None else rows_per_batch, tm)
    tn = _tile(n, tn)
    in_specs = [pl.BlockSpec((tm, k), lambda i, j: (i, 0)),
                pl.BlockSpec((k, tn), lambda i, j: (0, j))]
    args = [a, b]
    if rope is None:
        body = functools.partial(_mm_kernel, scale=scale, sigmoid=sigmoid)
    else:
        per = rows_per_batch // tm
        body = functools.partial(_mm_rope_kernel, scale=scale)
        in_specs += [pl.BlockSpec((tm, tn), lambda i, j: (i % per, 0))] * 2
        args += list(rope)
    return pl.pallas_call(
        body,
        grid=(m // tm, n // tn),
        in_specs=in_specs,
        out_specs=pl.BlockSpec((tm, tn), lambda i, j: (i, j)),
        out_shape=jax.ShapeDtypeStruct((m, n), out_dtype),
        compiler_params=_params("parallel", "parallel"),
        name="matmul",
    )(*args)


def _mm_res_kernel(a_ref, b_ref, x_ref, m_ref, o_ref, *, gate_row):
    acc = jnp.dot(a_ref[...], b_ref[...], preferred_element_type=F32)
    o_ref[...] = x_ref[...] + m_ref[0][gate_row:gate_row + 1] * acc


def _mm_residual(a, b, x, mods, rows_per_batch, gate_row, tm=1024, tn=1024):
    m, k = a.shape
    n = b.shape[1]
    tm = _tile(rows_per_batch, tm)
    tn = _tile(n, tn)
    per = rows_per_batch // tm
    return pl.pallas_call(
        functools.partial(_mm_res_kernel, gate_row=gate_row),
        grid=(m // tm, n // tn),
        in_specs=[pl.BlockSpec((tm, k), lambda i, j: (i, 0)),
                  pl.BlockSpec((k, tn), lambda i, j: (0, j)),
                  pl.BlockSpec((tm, tn), lambda i, j: (i, j)),
                  pl.BlockSpec((1, 8, tn), lambda i, j: (i // per, 0, j))],
        out_specs=pl.BlockSpec((tm, tn), lambda i, j: (i, j)),
        out_shape=jax.ShapeDtypeStruct((m, n), F32),
        compiler_params=_params("parallel", "parallel"),
        name="matmul_residual",
    )(a, b, x, mods)


def _merge_kernel(oa_ref, ob_ref, wa_ref, wb_ref, ga_ref, gb_ref, o_ref):
    ya = jnp.dot(oa_ref[...], wa_ref[...], preferred_element_type=F32)
    yb = jnp.dot(ob_ref[...], wb_ref[...], preferred_element_type=F32)
    o_ref[...] = (ga_ref[...] * ya + gb_ref[...] * yb).astype(o_ref.dtype)


def _merge(o_mla, o_gqa, w_mla_o, w_gqa_o, gates, tm=1024, tn=1024):
    m, k = o_mla.shape
    n = w_mla_o.shape[1]
    tm = _tile(m, tm)
    tn = _tile(n, tn)
    ca, cb = 0, n // tn
    return pl.pallas_call(
        _merge_kernel,
        grid=(m // tm, n // tn),
        in_specs=[pl.BlockSpec((tm, k), lambda i, j: (i, 0)),
                  pl.BlockSpec((tm, k), lambda i, j: (i, 0)),
                  pl.BlockSpec((k, tn), lambda i, j: (0, j)),
                  pl.BlockSpec((k, tn), lambda i, j: (0, j)),
                  pl.BlockSpec((tm, tn), lambda i, j: (i, ca + j)),
                  pl.BlockSpec((tm, tn), lambda i, j: (i, cb + j))],
        out_specs=pl.BlockSpec((tm, tn), lambda i, j: (i, j)),
        out_shape=jax.ShapeDtypeStruct((m, n), BF16),
        compiler_params=_params("parallel", "parallel"),
        name="merge",
    )(o_mla, o_gqa, w_mla_o, w_gqa_o, gates, gates)


def _rms(x, g):
    return x * lax.rsqrt(jnp.mean(x * x, axis=-1, keepdims=True) + EPS) * g


def _prep_kernel(*refs, q_rank, kv_rank, use_rope):
    if use_rope:
        (p_ref, kr_ref, gqa_ref, gkv_ref, ggq_ref, ggk_ref, cg_ref, sg_ref, cm_ref, sm_ref,
         qa_o, ckv_o, ckvx_o, gq_o, gk32_o, gk_o, gv_o) = refs
    else:
        (p_ref, kr_ref, gqa_ref, gkv_ref, ggq_ref, ggk_ref,
         qa_o, ckv_o, ckvx_o, gq_o, gk32_o, gk_o, gv_o) = refs
    c0 = q_rank
    c1 = c0 + kv_rank
    c2 = c1 + GQA_HEADS * HEAD_DIM
    c3 = c2 + GQA_KV_HEADS * HEAD_DIM
    qa_o[...] = _rms(p_ref[:, 0:c0], gqa_ref[...]).astype(BF16)
    ckv = _rms(p_ref[:, c0:c1], gkv_ref[...])
    ckv_o[...] = ckv
    ckvx_o[:, 0:kv_rank] = ckv.astype(BF16)
    kr = kr_ref[...]
    if use_rope:
        kr = kr * cm_ref[...] + _swap_pairs(kr) * sm_ref[...]
    ckvx_o[:, kv_rank:kv_rank + LANE] = kr.astype(BF16)
    q_scale = HEAD_DIM ** -0.5 * LOG2E
    for h in range(GQA_HEADS):
        x = _rms(p_ref[:, c1 + h * HEAD_DIM:c1 + (h + 1) * HEAD_DIM], ggq_ref[...])
        if use_rope:
            x = x * cg_ref[...] + _swap_pairs(x) * sg_ref[...]
        gq_o[:, h * HEAD_DIM:(h + 1) * HEAD_DIM] = (x * q_scale).astype(BF16)
    for h in range(GQA_KV_HEADS):
        x = _rms(p_ref[:, c2 + h * HEAD_DIM:c2 + (h + 1) * HEAD_DIM], ggk_ref[...])
        if use_rope:
            x = x * cg_ref[...] + _swap_pairs(x) * sg_ref[...]
        gk32_o[:, h * HEAD_DIM:(h + 1) * HEAD_DIM] = x
        gk_o[:, h * HEAD_DIM:(h + 1) * HEAD_DIM] = x.astype(BF16)
    gv_o[...] = p_ref[:, c3:c3 + GQA_KV_HEADS * HEAD_DIM].astype(BF16)


def _prep(proj, kr, g_q_a, g_kv_a, g_gqa_q, g_gqa_k, rope, rows_per_batch):
    m = proj.shape[0]
    q_rank, kv_rank = g_q_a.shape[0], g_kv_a.shape[0]
    width = q_rank + kv_rank + (GQA_HEADS + 2 * GQA_KV_HEADS) * HEAD_DIM
    tm = _tile(rows_per_batch, ROW_TILE)
    per = rows_per_batch // tm
    nq, nk = GQA_HEADS * HEAD_DIM, GQA_KV_HEADS * HEAD_DIM
    row = lambda i: (i, 0)
    const = lambda i: (0, 0)
    in_specs = [pl.BlockSpec((tm, width), row), pl.BlockSpec((tm, LANE), row),
                pl.BlockSpec((1, q_rank), const), pl.BlockSpec((1, kv_rank), const),
                pl.BlockSpec((1, HEAD_DIM), const), pl.BlockSpec((1, HEAD_DIM), const)]
    args = [proj, kr, g_q_a.reshape(1, -1), g_kv_a.reshape(1, -1), g_gqa_q.reshape(1, -1), g_gqa_k.reshape(1, -1)]
    if rope is not None:
        in_specs += [pl.BlockSpec((tm, LANE), lambda i: (i % per, 0))] * 4
        args += list(rope)
    outs = [((m, q_rank), BF16), ((m, kv_rank), F32), ((m, kv_rank + LANE), BF16), ((m, nq), BF16),
            ((m, nk), F32), ((m, nk), BF16), ((m, nk), BF16)]
    return pl.pallas_call(
        functools.partial(_prep_kernel, q_rank=q_rank, kv_rank=kv_rank, use_rope=rope is not None),
        grid=(m // tm,),
        in_specs=in_specs,
        out_specs=[pl.BlockSpec((tm, s[1]), row) for s, _ in outs],
        out_shape=[jax.ShapeDtypeStruct(s, dt) for s, dt in outs],
        compiler_params=_params("parallel"),
        name="prep",
    )(*args)


def _attn_kernel(q_ref, *refs, rep, dq, dv, sub):
    o_ref = refs[-1]
    ks = [r[...] for r in refs[0:-1:2]]
    vs = [r[...] for r in refs[1:-1:2]]
    for r in range(rep):
        for i in range(q_ref.shape[0] // sub):
            rows = slice(i * sub, (i + 1) * sub)
            q = q_ref[rows, r * dq:(r + 1) * dq]
            ss = [lax.dot_general(q, k, (((1,), (1,)), ((), ())), preferred_element_type=F32) for k in ks]
            s = ss[0] if len(ss) == 1 else jnp.concatenate(ss, axis=1)
            p = jnp.exp2(s - jnp.max(s, axis=-1, keepdims=True))
            l = jnp.sum(p, axis=-1, keepdims=True)
            p = p.astype(BF16)
            o, off = None, 0
            for v in vs:
                c = jnp.dot(p[:, off:off + v.shape[0]], v, preferred_element_type=F32)
                o = c if o is None else o + c
                off += v.shape[0]
            o_ref[rows, r * dv:(r + 1) * dv] = (o / l).astype(o_ref.dtype)


def _attention(q, kvs, n_groups, rep, dq, dv, k_col0, v_col0, tq):
    b, t, _ = q.shape
    tq = _tile(t, tq)
    sub = _tile(tq, 256)
    kb, vb = k_col0 // dq, v_col0 // dv
    assert kb * dq == k_col0 and vb * dv == v_col0
    in_specs = [pl.BlockSpec((None, tq, rep * dq), lambda bi, g, qi: (bi, qi, g))]
    args = [q]
    for k, v in kvs:
        in_specs += [pl.BlockSpec((None, k.shape[1], dq), lambda bi, g, qi: (bi, 0, kb + g)),
                     pl.BlockSpec((None, v.shape[1], dv), lambda bi, g, qi: (bi, 0, vb + g))]
        args += [k, v]
    return pl.pallas_call(
        functools.partial(_attn_kernel, rep=rep, dq=dq, dv=dv, sub=sub),
        grid=(b, n_groups, t // tq),
        in_specs=in_specs,
        out_specs=pl.BlockSpec((None, tq, rep * dv), lambda bi, g, qi: (bi, qi, g)),
        out_shape=jax.ShapeDtypeStruct((b, t, n_groups * rep * dv), BF16),
        compiler_params=_params("parallel", "parallel", "parallel"),
        name="attention",
    )(*args)


def _top16(s, idx):
    n, c = s.shape
    row = lax.broadcasted_iota(jnp.int32, (PEER_TOPK, c), 0)
    big = float(PEER_TOPK * PEER_TOPK)

    def body(r, carry):
        s, vals, idxs = carry
        mx = jnp.max(s, axis=0, keepdims=True)
        am = jnp.min(jnp.where(s == mx, idx, big), axis=0, keepdims=True)
        vals = jnp.where(row == r, mx, vals)
        idxs = jnp.where(row == r, am, idxs)
        s = jnp.where(idx == am, -jnp.inf, s)
        return s, vals, idxs

    init = (s, jnp.zeros((PEER_TOPK, c), F32), jnp.zeros((PEER_TOPK, c), F32))
    _, vals, idxs = lax.fori_loop(0, PEER_TOPK, body, init)
    return vals, idxs


def _pick(table, pos):
    out = jnp.zeros(pos.shape, F32)
    for a in range(PEER_TOPK):
        out = jnp.where(pos == a, table[a:a + 1], out)
    return out


_CAND_ROWS = [(0, 0), (0, 8), (1, 0), (2, 0), (3, 0), (4, 0), (5, 0), (6, 0), (7, 0)]


def _route_kernel(q_ref, k_ref, oi_ref, oj_ref, og_ref):
    q = q_ref[...]
    tn = q.shape[0]
    half = q.shape[1] // 2
    nt = (((1,), (1,)), ((), ()))
    s0 = lax.dot_general(k_ref[0], q[:, :half], nt, preferred_element_type=F32)
    s1 = lax.dot_general(k_ref[1], q[:, half:], nt, preferred_element_type=F32)
    s01 = jnp.concatenate([s0, s1], axis=1)
    v01, i01 = _top16(s01, lax.broadcasted_iota(jnp.int32, s01.shape, 0).astype(F32))
    v0, v1, i0, i1 = v01[:, :tn], v01[:, tn:], i01[:, :tn], i01[:, tn:]
    sub = lax.broadcasted_iota(jnp.int32, (SUBLANE, tn), 0).astype(F32)
    cand = [v0[a:a + 1] + v1[b:b + SUBLANE] for a, b in _CAND_ROWS]
    flat = [sub + float(a * PEER_TOPK + b) for a, b in _CAND_ROWS]
    cand.append(v0[SUBLANE:] + v1[0:1])
    flat.append((sub + float(SUBLANE)) * float(PEER_TOPK))
    best, pos = _top16(jnp.concatenate(cand, axis=0), jnp.concatenate(flat, axis=0))
    pos = pos.astype(jnp.int32)
    e = jnp.exp(best - best[0:1])
    oi_ref[...] = _pick(i0, pos >> 4)
    oj_ref[...] = _pick(i1, pos & (PEER_TOPK - 1))
    og_ref[...] = e / jnp.sum(e, axis=0, keepdims=True)


def _route(qp, sub_keys, tn=512):
    m = qp.shape[0]
    tn = _tile(m, tn)
    qd = qp.shape[1] // PEER_HEADS
    shp = jax.ShapeDtypeStruct((PEER_HEADS * PEER_TOPK, m), F32)
    spec = pl.BlockSpec((PEER_TOPK, tn), lambda t, h: (h, t))
    return pl.pallas_call(
        _route_kernel,
        grid=(m // tn, PEER_HEADS),
        in_specs=[pl.BlockSpec((tn, qd), lambda t, h: (t, h)),
                  pl.BlockSpec(sub_keys.shape, lambda t, h: (0, 0, 0))],
        out_specs=[spec, spec, spec],
        out_shape=[shp, shp, shp],
        compiler_params=_params("parallel", "parallel"),
        name="peer_route",
    )(qp, sub_keys)


def _wbuild_kernel(i_ref, j_ref, g_ref, o_ref, it_s, jt_s, gt_s):
    it_s[...] = i_ref[...].T
    jt_s[...] = j_ref[...].T
    gt_s[...] = g_ref[...].T
    ne = i_ref.shape[0]
    iota_a = lax.broadcasted_iota(jnp.int32, (2 * N_KEYS, 2 * ne), 0).astype(F32).astype(BF16)
    iota_g = lax.broadcasted_iota(jnp.int32, (N_KEYS, 2 * ne), 0).astype(F32).astype(BF16)
    one, zero = jnp.ones((), BF16), jnp.zeros((), BF16)

    def body(m, carry):
        ri = it_s[pl.ds(2 * m, 2), :]
        rj = jt_s[pl.ds(2 * m, 2), :]
        rg = gt_s[pl.ds(2 * m, 2), :]
        row_i = jnp.concatenate([ri[0:1], ri[1:2] + float(N_KEYS)], axis=1).astype(BF16)
        row_j = jnp.concatenate([rj[0:1], rj[1:2]], axis=1).astype(BF16)
        row_g = jnp.concatenate([rg[0:1], rg[1:2]], axis=1).astype(BF16)
        a2 = jnp.where(iota_a == row_i, one, zero)
        g2 = jnp.where(iota_g == row_j, row_g, zero)
        w = lax.dot_general(a2, g2, (((1,), (1,)), ((), ())), preferred_element_type=F32)
        for t in range(2):
            o_ref[:, 2 * m + t] = w[t * N_KEYS:(t + 1) * N_KEYS].reshape(N_KEYS // SUBLANE, SUBLANE, N_KEYS)
        return carry

    lax.fori_loop(0, o_ref.shape[1] // 2, body, 0, unroll=16)


def _wbuild(idx_i, idx_j, gates):
    ne, m = idx_i.shape
    tw = LANE
    spec = pl.BlockSpec((ne, tw), lambda t: (0, t))
    return pl.pallas_call(
        _wbuild_kernel,
        grid=(m // tw,),
        in_specs=[spec, spec, spec],
        out_specs=pl.BlockSpec((N_KEYS // SUBLANE, tw, SUBLANE, N_KEYS), lambda t: (0, t, 0, 0)),
        out_shape=jax.ShapeDtypeStruct((N_KEYS // SUBLANE, m, SUBLANE, N_KEYS), F32),
        scratch_shapes=[pltpu.VMEM((tw, ne), F32)] * 3,
        compiler_params=_params("parallel"),
        name="peer_gate_map",
    )(idx_i, idx_j, gates)


def _peer_kernel(x_ref, ut_ref, v_ref, w_ref, o_ref, ga_s, gb_s, *, rows, n_chunks):
    k = pl.program_id(1)
    base = (k % (SUBLANE // rows)) * rows

    def gates(dst):
        s = jnp.dot(x_ref[...], ut_ref[...], preferred_element_type=F32)
        for il in range(rows):
            wg = w_ref[:, base + il, :]
            sl = s[:, il * N_KEYS:(il + 1) * N_KEYS]
            g = 0.5 * sl * (1.0 + lax.erf(sl * INV_SQRT2)) * wg
            dst[:, il * N_KEYS:(il + 1) * N_KEYS] = g.astype(BF16)

    def mix(src):
        return jnp.dot(src[...], v_ref[...], preferred_element_type=F32)

    inner = jnp.logical_and(k > 0, k < n_chunks)

    @pl.when(k == 0)
    def _():
        o_ref[...] = jnp.zeros_like(o_ref)
        gates(ga_s)

    @pl.when(jnp.logical_and(inner, k % 2 == 1))
    def _():
        gates(gb_s)
        o_ref[...] += mix(ga_s)

    @pl.when(jnp.logical_and(inner, k % 2 == 0))
    def _():
        gates(ga_s)
        o_ref[...] += mix(gb_s)

    @pl.when(k == n_chunks)
    def _():
        o_ref[...] += mix(gb_s if n_chunks % 2 == 0 else ga_s)


def _peer_mix(h2, u_t, v, wmap, tn=1024):
    m, d = h2.shape
    n_exp = v.shape[0]
    tn = _tile(m, tn)
    rows = PEER_CHUNK_ROWS
    ec = rows * N_KEYS
    n_chunks = n_exp // ec
    assert u_t.shape == (n_chunks, d, ec)
    per = SUBLANE // rows
    last = n_chunks - 1
    return pl.pallas_call(
        functools.partial(_peer_kernel, rows=rows, n_chunks=n_chunks),
        grid=(m // tn, n_chunks + 1),
        in_specs=[pl.BlockSpec((tn, d), lambda t, k: (t, 0), pipeline_mode=pl.Buffered(1)),
                  pl.BlockSpec((None, d, ec), lambda t, k: (jnp.minimum(k, last), 0, 0)),
                  pl.BlockSpec((ec, d), lambda t, k: (jnp.maximum(k - 1, 0), 0)),
                  pl.BlockSpec((None, tn, SUBLANE, N_KEYS), lambda t, k: (jnp.minimum(k, last) // per, t, 0, 0))],
        out_specs=pl.BlockSpec((tn, d), lambda t, k: (t, 0), pipeline_mode=pl.Buffered(1)),
        out_shape=jax.ShapeDtypeStruct((m, d), F32),
        scratch_shapes=[pltpu.VMEM((tn, ec), BF16), pltpu.VMEM((tn, ec), BF16)],
        compiler_params=_params("parallel", "arbitrary"),
        name="peer_mix",
    )(h2, u_t, v, wmap)


def _final_kernel(x_ref, p_ref, m_ref, g_ref, o_ref, *, gate_row):
    x = x_ref[...] + m_ref[0][gate_row:gate_row + 1] * p_ref[...]
    o_ref[...] = _rms(x, g_ref[...])


def _final(x, po, mods, g_final, rows_per_batch, gate_row):
    m, d = x.shape
    tm = _tile(rows_per_batch, ROW_TILE // 2)
    per = rows_per_batch // tm
    row = lambda i: (i, 0)
    return pl.pallas_call(
        functools.partial(_final_kernel, gate_row=gate_row),
        grid=(m // tm,),
        in_specs=[pl.BlockSpec((tm, d), row), pl.BlockSpec((tm, d), row),
                  pl.BlockSpec((1, 8, d), lambda i: (i // per, 0, 0)),
                  pl.BlockSpec((1, d), lambda i: (0, 0))],
        out_specs=pl.BlockSpec((tm, d), row),
        out_shape=jax.ShapeDtypeStruct((m, d), F32),
        compiler_params=_params("parallel"),
        name="final_norm",
    )(x, po, mods, g_final.reshape(1, d))


def _rope_tables(n_tokens, dim):
    n_rows = n_tokens // GRID_W
    rows = jnp.broadcast_to(jnp.arange(n_rows)[:, None], (n_rows, GRID_W)).reshape(-1).astype(F32)
    cols = jnp.broadcast_to(jnp.arange(GRID_W)[None, :], (n_rows, GRID_W)).reshape(-1).astype(F32)
    n_freq = dim // 4
    freqs = ROPE_BASE ** (-jnp.arange(n_freq, dtype=F32) / n_freq)
    ang = jnp.concatenate([rows[:, None] * freqs, cols[:, None] * freqs], axis=-1)
    cos, sin = jnp.cos(ang), jnp.sin(ang)
    cos2 = jnp.repeat(cos, 2, axis=-1)
    sin2 = jnp.stack([-sin, sin], axis=-1).reshape(n_tokens, dim)
    return cos2, sin2


def _layer_weights(w_in, w_q_b, w_kv_b, w_mla_o, w_gqa_o, w_out, w_peer_q, sub_keys, peer_u, peer_v):
    q_rank, kv_rank = w_q_b.shape[0], w_kv_b.shape[0]
    k0 = q_rank + kv_rank
    lw = {}
    g0 = k0 + MLA_ROPE + (GQA_HEADS + 2 * GQA_KV_HEADS) * HEAD_DIM
    main = jnp.concatenate([w_in[:, :k0], w_in[:, k0 + MLA_ROPE:g0], w_in[:, k0:k0 + MLA_ROPE]], axis=1)
    lw['w_in'] = jnp.pad(main, ((0, 0), (0, -main.shape[1] % 512))).astype(BF16)
    lw['w_gates'] = w_in[:, g0:].astype(BF16)
    wq = w_q_b.reshape(q_rank, MLA_HEADS, MLA_NOPE + MLA_ROPE)
    wq = jnp.pad(wq, ((0, 0), (0, 0), (0, MLA_QPAD - MLA_NOPE - MLA_ROPE)))
    lw['w_q_b'] = wq.reshape(q_rank, MLA_HEADS * MLA_QPAD).astype(BF16)
    wkv = w_kv_b.reshape(kv_rank, MLA_HEADS, MLA_NOPE + MLA_V)
    wk = jnp.zeros((kv_rank + LANE, MLA_HEADS, MLA_QPAD), F32)
    wk = wk.at[:kv_rank, :, :MLA_NOPE].set(wkv[:, :, :MLA_NOPE])
    eye = jnp.broadcast_to(jnp.eye(MLA_ROPE, dtype=F32)[:, None, :], (MLA_ROPE, MLA_HEADS, MLA_ROPE))
    wk = wk.at[kv_rank:kv_rank + MLA_ROPE, :, MLA_NOPE:MLA_NOPE + MLA_ROPE].set(eye)
    wv = jnp.pad(wkv[:, :, MLA_NOPE:], ((0, LANE), (0, 0), (0, 0)))
    lw['w_kv'] = jnp.concatenate([wk.reshape(kv_rank + LANE, -1), wv.reshape(kv_rank + LANE, -1)],
                                 axis=1).astype(BF16)
    lw['w_mla_o'] = w_mla_o.astype(BF16)
    lw['w_gqa_o'] = w_gqa_o.astype(BF16)
    lw['w_out'] = w_out.astype(BF16)
    lw['w_peer_q'] = w_peer_q.astype(BF16)
    lw['sub_keys'] = sub_keys.astype(BF16)
    chunk = PEER_CHUNK_ROWS * N_KEYS
    lw['peer_ut'] = peer_u.reshape(-1, chunk, peer_u.shape[1]).transpose(0, 2, 1).astype(BF16)
    lw['peer_v'] = peer_v.astype(BF16)
    return lw


def _layer(x, mods, lw, gains, rope, ctx):
    b, t, d = x.shape
    m = b * t
    x2 = x.reshape(m, d)
    g_norm1, g_norm2, g_q_a, g_kv_a, g_gqa_q, g_gqa_k = gains
    q_rank, kv_rank = g_q_a.shape[0], g_kv_a.shape[0]
    nq, nk = GQA_HEADS * HEAD_DIM, GQA_KV_HEADS * HEAD_DIM
    col_gate_mla = q_rank + kv_rank + nq + 2 * nk

    rpb = t if mods.shape[0] == b else m
    h = _norm_mod(x2, g_norm1, mods, rpb, SHIFT1, SCALE1)
    proj = _mm(h, lw['w_in'], F32)
    gates = _mm(h, lw['w_gates'], BF16, sigmoid=True, tn=1024)
    kr = proj[:, col_gate_mla:col_gate_mla + LANE]
    prep_rope = None if rope is None else rope[:4]
    qa, ckv32, ckvx, gq, gk32, gk, gv = _prep(proj, kr, g_q_a, g_kv_a, g_gqa_q, g_gqa_k, prep_rope, t)

    mla_scale = (MLA_NOPE + MLA_ROPE) ** -0.5 * LOG2E
    if rope is None:
        q_mla = _mm(qa, lw['w_q_b'], BF16, scale=mla_scale)
    else:
        q_mla = _mm(qa, lw['w_q_b'], BF16, scale=mla_scale, rope=rope[4:], rows_per_batch=t)

    kv = _mm(ckvx, lw['w_kv'], BF16, tn=2048).reshape(b, t, -1)
    mla_kv = [(kv, kv)]
    gqa_kv = [(gk.reshape(b, t, nk), gv.reshape(b, t, nk))]
    if ctx is not None:
        c_ckv, c_kr, c_k, c_v = ctx
        past = c_ckv.shape[1]
        c_x = jnp.concatenate([c_ckv, c_kr, jnp.zeros((b, past, LANE - MLA_ROPE), F32)], axis=-1).astype(BF16)
        c_kv = _mm(c_x.reshape(b * past, -1), lw['w_kv'], BF16, tn=2048).reshape(b, past, -1)
        mla_kv.append((c_kv, c_kv))
        gqa_kv.append((c_k.reshape(b, past, nk).astype(BF16), c_v.reshape(b, past, nk).astype(BF16)))

    o_mla = _attention(q_mla.reshape(b, t, -1), mla_kv, MLA_HEADS, 1, MLA_QPAD, MLA_V,
                       0, MLA_HEADS * MLA_QPAD, tq=4096)
    o_gqa = _attention(gq.reshape(b, t, -1), gqa_kv, GQA_KV_HEADS, GQA_HEADS // GQA_KV_HEADS,
                       HEAD_DIM, HEAD_DIM, 0, 0, tq=1024)
    merged = _merge(o_mla.reshape(m, -1), o_gqa.reshape(m, -1), lw['w_mla_o'], lw['w_gqa_o'],
                    gates)
    x1 = _mm_residual(merged, lw['w_out'], x2, mods, rpb, GATE1)

    h2 = _norm_mod(x1, g_norm2, mods, rpb, SHIFT2, SCALE2)
    qp = _mm(h2, lw['w_peer_q'], BF16, tn=1024)
    idx_i, idx_j, gates = _route(qp, lw['sub_keys'])
    wmap = _wbuild(idx_i, idx_j, gates)
    po = _peer_mix(h2, lw['peer_ut'], lw['peer_v'], wmap)
    own = (ckv32.reshape(b, t, kv_rank), kr[:, :MLA_ROPE].reshape(b, t, MLA_ROPE),
           gk32.reshape(b, t, GQA_KV_HEADS, HEAD_DIM),
           proj[:, col_gate_mla - nk:col_gate_mla].reshape(b, t, GQA_KV_HEADS, HEAD_DIM))
    return x1, po, own


def kernel(x_prompt, x_sample, c, cache_mla_ckv, cache_mla_krope, cache_gqa_k, cache_gqa_v, c_ctx, w_mod, b_mod, g_norm1, g_norm2, w_in, g_q_a, w_q_b, g_kv_a, w_kv_b, g_gqa_q, g_gqa_k, w_mla_o, w_gqa_o, w_out, w_peer_q, peer_sub_keys, peer_u, peer_v, g_final):
    assert w_in.shape[0] == 1, "one trunk layer is supported"
    d = x_prompt.shape[-1]
    bc, tc = x_prompt.shape[:2]
    bl, tl = x_sample.shape[:2]

    cg, sg = _rope_tables(tl, HEAD_DIM)
    cm, sm = _rope_tables(tl, MLA_ROPE)
    one = jnp.ones((tl, LANE - MLA_ROPE), F32)
    zero = jnp.zeros((tl, LANE - MLA_ROPE), F32)
    cm_k, sm_k = jnp.concatenate([cm, one], 1), jnp.concatenate([sm, zero], 1)
    ones_n, zeros_n = jnp.ones((tl, MLA_NOPE), F32), jnp.zeros((tl, MLA_NOPE), F32)
    cq = jnp.tile(jnp.concatenate([ones_n, cm, one], 1), (1, 2))
    sq = jnp.tile(jnp.concatenate([zeros_n, sm, zero], 1), (1, 2))
    rope = (cg, sg, cm_k, sm_k, cq, sq)

    n_rows = bl + 1
    pad_rows = -n_rows % 8
    cvecs = jnp.concatenate([c, c_ctx[None, :], jnp.zeros((pad_rows, d), F32)], axis=0)

    lw = _layer_weights(w_in[0], w_q_b[0], w_kv_b[0], w_mla_o[0], w_gqa_o[0], w_out[0],
                        w_peer_q[0], peer_sub_keys[0], peer_u[0], peer_v[0])
    gains = (g_norm1[0], g_norm2[0], g_q_a[0], g_kv_a[0], g_gqa_q[0], g_gqa_k[0])
    mod = _modulation(cvecs, w_mod[0], b_mod[0]).reshape(n_rows + pad_rows, N_MOD, d)
    mod = jnp.pad(mod, ((0, 0), (0, 8 - N_MOD), (0, 0)))
    mods_lat, mods_ctx = mod[:bl], mod[bl:bl + 1]

    x1, po, own = _layer(x_prompt, mods_ctx, lw, gains, None, None)
    y_prompt = _final(x1, po, mods_ctx, g_final, bc * tc, GATE2).reshape(bc, tc, d)

    ctx = (cache_mla_ckv[:, 0], cache_mla_krope[:, 0], cache_gqa_k[:, 0], cache_gqa_v[:, 0])
    x1, po, _ = _layer(x_sample, mods_lat, lw, gains, rope, ctx)
    y_sample = _final(x1, po, mods_lat, g_final, tl, GATE2).reshape(bl, tl, d)

    return (y_prompt, y_sample) + tuple(o[:, None] for o in own)
```
